```python
import jax
import jax.numpy as jnp
from jax import lax
import numpy as np

D_MODEL = 1024
BATCH = 2
SEQ = 8192
DEPTH = 1

GRID_W = 64
CTX_LEN = 256
D_MIX = D_MODEL
CONV_CH = D_MIX // 2
CONV_WIDTH = 31
N_HEADS = 8
HEAD_DIM = (D_MIX - CONV_CH) // N_HEADS
ATTN_W = N_HEADS * HEAD_DIM
NA_ROWS = 8
NA_COLS = 16
QB_W = 16
KB_W = 32
N_EXPERTS = 16
D_EXPERT = 2048
EC_FACTOR = 2
N_MOD = 6
EPS = 1e-6
NEG_INF = -1e30
W_IN_COLS = 2 * CONV_CH + 3 * ATTN_W

kernel_name = 'hybrid_conformer_natten_ec_dit'


def rms_norm(x, g):
    xf = x.astype(jnp.float32)
    y = xf * lax.rsqrt(jnp.mean(xf * xf, axis=-1, keepdims=True) + EPS)
    return (y * g.astype(jnp.float32)).astype(x.dtype)


def layer_norm(x, g, b):
    xf = x.astype(jnp.float32)
    mu = jnp.mean(xf, axis=-1, keepdims=True)
    var = jnp.mean(jnp.square(xf - mu), axis=-1, keepdims=True)
    y = (xf - mu) * lax.rsqrt(var + EPS)
    return (y * g.astype(jnp.float32) + b.astype(jnp.float32)).astype(x.dtype)


def adaln(cvec, w_mod, b_mod):
    m = (jax.nn.silu(cvec) @ w_mod + b_mod).reshape(-1, 1, N_MOD * D_MODEL)
    return jnp.split(m, N_MOD, axis=-1)


def modulate(h, shift, scale):
    return h * (1.0 + scale) + shift


def conv_mixer(u, conv_w, conv_b, ln_g, ln_b):
    a, g = jnp.split(u, 2, axis=-1)
    v = a * jax.nn.sigmoid(g)
    y = lax.conv_general_dilated(
        v, conv_w[:, None, :].astype(v.dtype), window_strides=(1,),
        padding=[(CONV_WIDTH // 2, CONV_WIDTH // 2)],
        dimension_numbers=('NWC', 'WIO', 'NWC'), feature_group_count=CONV_CH) + conv_b
    return jax.nn.silu(layer_norm(y, ln_g, ln_b))


def na_indices(rows):
    kh = min(NA_ROWS, rows)
    n_blk = GRID_W // QB_W
    r = jnp.arange(rows)
    r0 = jnp.clip(r - kh // 2, 0, rows - kh)
    key_rows = r0[:, None] + jnp.arange(kh)[None, :]
    q_cols = jnp.arange(GRID_W).reshape(n_blk, QB_W)
    band0 = jnp.clip(jnp.arange(n_blk) * QB_W - (KB_W - QB_W) // 2, 0, GRID_W - KB_W)
    key_cols = band0[:, None] + jnp.arange(KB_W)[None, :]
    c0 = jnp.clip(q_cols - NA_COLS // 2, 0, GRID_W - NA_COLS)
    kc_b = key_cols[:, None, :]
    col_valid = (kc_b >= c0[:, :, None]) & (kc_b < c0[:, :, None] + NA_COLS)
    d_row = key_rows - r[:, None] + (NA_ROWS - 1)
    d_col = jnp.clip(kc_b - q_cols[:, :, None] + (NA_COLS - 1), 0, 2 * NA_COLS - 2)
    return key_rows, key_cols, col_valid, d_row, d_col


def na_attention(q, k, v, k_ctx, v_ctx, rpb, idx):
    key_rows, key_cols, col_valid, d_row, d_col = idx
    B, T, _ = q.shape
    rows, kh = key_rows.shape
    n_blk = GRID_W // QB_W
    L = k_ctx.shape[1]
    f32 = jnp.float32
    qb = q.reshape(B, rows, n_blk, QB_W, N_HEADS, HEAD_DIM).astype(f32) * (HEAD_DIM ** -0.5)
    kg = k.reshape(B, rows, GRID_W, N_HEADS, HEAD_DIM)
    vg = v.reshape(B, rows, GRID_W, N_HEADS, HEAD_DIM)
    ri = key_rows[:, None, :, None]
    ci = key_cols[None, :, None, :]
    k_band = kg[:, ri, ci]
    v_band = vg[:, ri, ci]
    kcx = k_ctx.reshape(B, L, N_HEADS, HEAD_DIM)
    vcx = v_ctx.reshape(B, L, N_HEADS, HEAD_DIM)
    bias = rpb[:, d_row[:, None, None, :, None], d_col[None, :, :, None, :]]
    s_loc = jnp.einsum('brnqhd,brnjwhd->bhrnqjw', qb, k_band.astype(f32)) + bias.astype(f32)[None]
    s_loc = jnp.where(col_valid[:, :, None, :], s_loc, NEG_INF)
    s_ctx = jnp.einsum('brnqhd,blhd->bhrnql', qb, kcx.astype(f32))
    n_loc = kh * KB_W
    s = jnp.concatenate([s_loc.reshape(s_loc.shape[:5] + (n_loc,)), s_ctx], axis=-1)
    p = jax.nn.softmax(s, axis=-1).astype(v.dtype)
    p_loc = p[..., :n_loc].reshape(s_loc.shape)
    p_ctx = p[..., n_loc:]
    o = (jnp.einsum('bhrnqjw,brnjwhd->brnqhd', p_loc, v_band)
         + jnp.einsum('bhrnql,blhd->brnqhd', p_ctx, vcx))
    return o.reshape(B, T, ATTN_W)


def ctx_attention(q, k, v):
    B, L, _ = q.shape
    f32 = jnp.float32
    qh = q.reshape(B, L, N_HEADS, HEAD_DIM).astype(f32) * (HEAD_DIM ** -0.5)
    kh = k.reshape(B, L, N_HEADS, HEAD_DIM).astype(f32)
    vh = v.reshape(B, L, N_HEADS, HEAD_DIM)
    p = jax.nn.softmax(jnp.einsum('blhd,bmhd->bhlm', qh, kh), axis=-1).astype(v.dtype)
    return jnp.einsum('bhlm,bmhd->blhd', p, vh).reshape(B, L, ATTN_W)


def ec_moe(h, w_router, w_gate, w_up, w_down):
    B, T, _ = h.shape
    cap = EC_FACTOR * T // N_EXPERTS
    aff = jax.nn.softmax((h @ w_router).astype(jnp.float32), axis=-1)
    wts, tok = lax.top_k(jnp.swapaxes(aff, 1, 2), cap)
    bidx = jnp.arange(B)[:, None, None]
    xs = h[bidx, tok]
    hid = jax.nn.silu(jnp.einsum('becd,edf->becf', xs, w_gate)) * jnp.einsum('becd,edf->becf', xs, w_up)
    y = jnp.einsum('becf,efd->becd', hid, w_down) * wts[..., None].astype(h.dtype)
    return jnp.zeros_like(h).at[bidx, tok].add(y)


def setup_inputs(seed: int = 0) -> dict:
    key = jax.random.key(seed)
    ks = jax.random.split(key, 21)
    f32 = jnp.float32

    def nrm(k, shape, scale):
        return jax.random.normal(k, shape, f32) * scale

    return {
        'x': nrm(ks[0], (BATCH, SEQ, D_MODEL), 1.0),
        'c': nrm(ks[1], (BATCH, D_MODEL), 1.0),
        'ctx': nrm(ks[2], (BATCH, CTX_LEN, D_MODEL), 1.0),
        'c_ctx': nrm(ks[3], (D_MODEL,), 1.0),
        'w_mod': nrm(ks[4], (DEPTH, D_MODEL, N_MOD * D_MODEL), 0.5 * D_MODEL ** -0.5),
        'b_mod': nrm(ks[5], (DEPTH, N_MOD * D_MODEL), 0.02),
        'norm_mix_g': 1.0 + nrm(ks[6], (DEPTH, D_MODEL), 0.02),
        'w_in': nrm(ks[7], (DEPTH, D_MODEL, W_IN_COLS), D_MODEL ** -0.5),
        'conv_w': nrm(ks[8], (DEPTH, CONV_WIDTH, CONV_CH), CONV_WIDTH ** -0.5),
        'conv_b': nrm(ks[9], (DEPTH, CONV_CH), 0.02),
        'conv_ln_g': 1.0 + nrm(ks[10], (DEPTH, CONV_CH), 0.02),
        'conv_ln_b': nrm(ks[11], (DEPTH, CONV_CH), 0.02),
        'rpb': nrm(ks[12], (DEPTH, N_HEADS, 2 * NA_ROWS - 1, 2 * NA_COLS - 1), 0.1),
        'w_out': nrm(ks[13], (DEPTH, D_MIX, D_MODEL), D_MIX ** -0.5),
        'norm_ffn_g': 1.0 + nrm(ks[14], (DEPTH, D_MODEL), 0.02),
        'w_router': nrm(ks[15], (DEPTH, D_MODEL, N_EXPERTS), D_MODEL ** -0.5),
        'w_gate': nrm(ks[16], (DEPTH, N_EXPERTS, D_MODEL, D_EXPERT), D_MODEL ** -0.5),
        'w_up': nrm(ks[17], (DEPTH, N_EXPERTS, D_MODEL, D_EXPERT), D_MODEL ** -0.5),
        'w_down': nrm(ks[18], (DEPTH, N_EXPERTS, D_EXPERT, D_MODEL), D_EXPERT ** -0.5),
        'norm_final_g': 1.0 + nrm(ks[19], (D_MODEL,), 0.02),
    }


def reference(x, c, ctx, c_ctx, w_mod, b_mod, norm_mix_g, w_in, conv_w, conv_b, conv_ln_g, conv_ln_b,
              rpb, w_out, norm_ffn_g, w_router, w_gate, w_up, w_down, norm_final_g):
    B, T, _ = x.shape
    rows = T // GRID_W
    idx = na_indices(rows)
    splits = [2 * CONV_CH, 2 * CONV_CH + ATTN_W, 2 * CONV_CH + 2 * ATTN_W]
    for l in range(DEPTH):
        last = l == DEPTH - 1
        sh1, sc1, g1, sh2, sc2, g2 = adaln(c, w_mod[l], b_mod[l])
        csh1, csc1, cg1, csh2, csc2, cg2 = adaln(c_ctx, w_mod[l], b_mod[l])

        h = modulate(rms_norm(x, norm_mix_g[l]), sh1, sc1)
        hc = modulate(rms_norm(ctx, norm_mix_g[l]), csh1, csc1)
        u, q, k, v = jnp.split(h @ w_in[l], splits, axis=-1)
        if last:
            k_c, v_c = jnp.split(hc @ w_in[l][:, splits[1]:], 2, axis=-1)
        else:
            u_c, q_c, k_c, v_c = jnp.split(hc @ w_in[l], splits, axis=-1)
        y_conv = conv_mixer(u, conv_w[l], conv_b[l], conv_ln_g[l], conv_ln_b[l])
        y_attn = na_attention(q, k, v, k_c, v_c, rpb[l], idx)
        x = x + g1 * (jnp.concatenate([y_conv, y_attn], axis=-1) @ w_out[l])
        if not last:
            yc_conv = conv_mixer(u_c, conv_w[l], conv_b[l], conv_ln_g[l], conv_ln_b[l])
            yc_attn = ctx_attention(q_c, k_c, v_c)
            ctx = ctx + cg1 * (jnp.concatenate([yc_conv, yc_attn], axis=-1) @ w_out[l])

        h2 = modulate(rms_norm(x, norm_ffn_g[l]), sh2, sc2)
        x = x + g2 * ec_moe(h2, w_router[l], w_gate[l], w_up[l], w_down[l])
        if not last:
            hc2 = modulate(rms_norm(ctx, norm_ffn_g[l]), csh2, csc2)
            ctx = ctx + cg2 * ec_moe(hc2, w_router[l], w_gate[l], w_up[l], w_down[l])
    return rms_norm(x, norm_final_g)
```

```python
import functools

import jax
import jax.numpy as jnp
from jax import lax
from jax.experimental import pallas as pl
from jax.experimental.pallas import tpu as pltpu

F32 = jnp.float32
BF16 = jnp.bfloat16
I32 = jnp.int32
HIGHEST = lax.Precision.HIGHEST

GRID_W = 64
NA_COLS = 16
EC_FACTOR = 2
N_MOD = 6
EPS = 1e-6
NEG_INF = -1e30
LANES = 128
SUBLANES = 8
VMEM_LIMIT = 56 * 1024 * 1024

_NT = (((1,), (1,)), ((), ()))


def _params(sem, vmem=None):
    return pltpu.CompilerParams(dimension_semantics=sem, vmem_limit_bytes=vmem)


def _adaln_body(c_ref, w_ref, b_ref, o_ref):
    c = c_ref[...]
    s = c * jax.nn.sigmoid(c)
    o_ref[...] = jnp.dot(s, w_ref[...], preferred_element_type=F32, precision=HIGHEST) + b_ref[...]


def _adaln(c8, w_mod, b_mod):
    d, n = w_mod.shape
    bn = 512
    return pl.pallas_call(
        _adaln_body,
        grid=(n // bn,),
        in_specs=[pl.BlockSpec((SUBLANES, d), lambda j: (0, 0)),
                  pl.BlockSpec((d, bn), lambda j: (0, j)),
                  pl.BlockSpec((1, bn), lambda j: (0, j))],
        out_specs=pl.BlockSpec((SUBLANES, bn), lambda j: (0, j)),
        out_shape=jax.ShapeDtypeStruct((SUBLANES, n), F32),
        compiler_params=_params(("arbitrary",)),
        name="adaln",
    )(c8, w_mod, b_mod)


def _norm_mod(x, g, shift, scale):
    ms = jnp.mean(x * x, axis=-1, keepdims=True)
    h = x * lax.rsqrt(ms + EPS) * g
    return h * (1.0 + scale) + shift


def _inproj_body(x_ref, g_ref, sh_ref, sc_ref, w_ref, u_ref, q_ref, k_ref, v_ref, *, cw, aw, qscale):
    hb = _norm_mod(x_ref[...], g_ref[...], sh_ref[...], sc_ref[...]).astype(BF16)

    def proj(lo, width):
        return jnp.dot(hb, w_ref[:, lo:lo + width], preferred_element_type=F32)

    a = proj(0, cw)
    gate = proj(cw, cw)
    u_ref[...] = a * jax.nn.sigmoid(gate)
    q_ref[...] = (proj(2 * cw, aw) * qscale).astype(BF16)
    k_ref[...] = proj(2 * cw + aw, aw).astype(BF16)
    v_ref[...] = proj(2 * cw + 2 * aw, aw).astype(BF16)


def _inproj_kv_body(x_ref, g_ref, sh_ref, sc_ref, w_ref, k_ref, v_ref, *, aw):
    hb = _norm_mod(x_ref[...], g_ref[...], sh_ref[...], sc_ref[...]).astype(BF16)
    k_ref[...] = jnp.dot(hb, w_ref[:, 0:aw], preferred_element_type=F32).astype(BF16)
    v_ref[...] = jnp.dot(hb, w_ref[:, aw:2 * aw], preferred_element_type=F32).astype(BF16)


def _mod_spec(slot, row_of):
    return lambda d: pl.BlockSpec((None, None, 1, d), lambda b, i: (row_of(b), slot, 0, 0))


def _inproj(x, g, mods4, w_bf, cw, aw, qscale, tm):
    bsz, t, d = x.shape
    tok = lambda w, dt: (pl.BlockSpec((None, tm, w), lambda b, i: (b, i, 0)), jax.ShapeDtypeStruct((bsz, t, w), dt))
    outs = [tok(cw, F32), tok(aw, BF16), tok(aw, BF16), tok(aw, BF16)]
    return pl.pallas_call(
        functools.partial(_inproj_body, cw=cw, aw=aw, qscale=qscale),
        grid=(bsz, t // tm),
        in_specs=[pl.BlockSpec((None, tm, d), lambda b, i: (b, i, 0)),
                  pl.BlockSpec((1, d), lambda b, i: (0, 0)),
                  _mod_spec(0, lambda b: b)(d),
                  _mod_spec(1, lambda b: b)(d),
                  pl.BlockSpec(w_bf.shape, lambda b, i: (0, 0))],
        out_specs=[o[0] for o in outs],
        out_shape=[o[1] for o in outs],
        compiler_params=_params(("arbitrary", "arbitrary"), VMEM_LIMIT),
        name="inproj",
    )(x, g, mods4, mods4, w_bf)


def _inproj_ctx(ctx, g, mods4, w_kv_bf, aw, ctx_row):
    bsz, l, d = ctx.shape
    spec = pl.BlockSpec((None, l, aw), lambda b, i: (b, 0, 0))
    shape = jax.ShapeDtypeStruct((bsz, l, aw), BF16)
    return pl.pallas_call(
        functools.partial(_inproj_kv_body, aw=aw),
        grid=(bsz, 1),
        in_specs=[pl.BlockSpec((None, l, d), lambda b, i: (b, 0, 0)),
                  pl.BlockSpec((1, d), lambda b, i: (0, 0)),
                  _mod_spec(0, lambda b: ctx_row)(d),
                  _mod_spec(1, lambda b: ctx_row)(d),
                  pl.BlockSpec(w_kv_bf.shape, lambda b, i: (0, 0))],
        out_specs=[spec, spec],
        out_shape=[shape, shape],
        compiler_params=_params(("arbitrary", "arbitrary"), VMEM_LIMIT),
        name="inproj_ctx",
    )(ctx, g, mods4, mods4, w_kv_bf)


CONV_HALO = 16
CONV_RC = 32


def _conv_body(prev_ref, cur_ref, next_ref, w_ref, b_ref, lg_ref, lb_ref, o_ref, ext, *, width, tt):
    i = pl.program_id(1)
    n = pl.num_programs(1)
    ext[0:CONV_HALO, :] = jnp.where(i > 0, prev_ref[...], 0.0)
    ext[CONV_HALO:CONV_HALO + tt, :] = cur_ref[...]
    ext[CONV_HALO + tt:, :] = jnp.where(i < n - 1, next_ref[...], 0.0)
    half = width // 2
    for rc in range(tt // CONV_RC):
        base = rc * CONV_RC + CONV_HALO - half
        acc = jnp.zeros((CONV_RC, cur_ref.shape[-1]), F32)
        for j in range(width):
            acc = acc + w_ref[j:j + 1, :] * ext[base + j:base + j + CONV_RC, :]
        y = acc + b_ref[...]
        mu = jnp.mean(y, axis=-1, keepdims=True)
        yc = y - mu
        var = jnp.mean(yc * yc, axis=-1, keepdims=True)
        yn = yc * lax.rsqrt(var + EPS) * lg_ref[...] + lb_ref[...]
        o_ref[rc * CONV_RC:(rc + 1) * CONV_RC, :] = (yn * jax.nn.sigmoid(yn)).astype(BF16)


def _conv(u, conv_w, conv_b, ln_g, ln_b, tt):
    bsz, t, cw = u.shape
    width = conv_w.shape[0]
    assert width // 2 < CONV_HALO and tt % CONV_HALO == 0 and tt % CONV_RC == 0
    hb = tt // CONV_HALO
    nh = t // CONV_HALO
    row = lambda a: a.reshape(1, cw)
    return pl.pallas_call(
        functools.partial(_conv_body, width=width, tt=tt),
        grid=(bsz, t // tt),
        in_specs=[pl.BlockSpec((None, CONV_HALO, cw), lambda b, i: (b, jnp.maximum(i * hb - 1, 0), 0)),
                  pl.BlockSpec((None, tt, cw), lambda b, i: (b, i, 0)),
                  pl.BlockSpec((None, CONV_HALO, cw), lambda b, i: (b, jnp.minimum((i + 1) * hb, nh - 1), 0)),
                  pl.BlockSpec((width, cw), lambda b, i: (0, 0)),
                  pl.BlockSpec((1, cw), lambda b, i: (0, 0)),
                  pl.BlockSpec((1, cw), lambda b, i: (0, 0)),
                  pl.BlockSpec((1, cw), lambda b, i: (0, 0))],
        out_specs=pl.BlockSpec((None, tt, cw), lambda b, i: (b, i, 0)),
        out_shape=jax.ShapeDtypeStruct((bsz, t, cw), BF16),
        scratch_shapes=[pltpu.VMEM((tt + 2 * CONV_HALO, cw), F32)],
        compiler_params=_params(("arbitrary", "arbitrary")),
        name="conv",
    )(u, u, u, conv_w, row(conv_b), row(ln_g), row(ln_b))


ATT_RB = 8


def _bias_body(rpb_ref, onehot_ref, valid_ref, o_ref):
    t = jnp.dot(rpb_ref[...], onehot_ref[...], preferred_element_type=F32, precision=HIGHEST)
    o_ref[...] = jnp.where(valid_ref[...] > 0.5, t, NEG_INF)


def _bias_table(rpb):
    nh, ndr, ndc = rpb.shape
    qc = jnp.arange(GRID_W)[:, None]
    kc = jnp.arange(GRID_W)[None, :]
    c0 = jnp.clip(qc - NA_COLS // 2, 0, GRID_W - NA_COLS)
    valid = ((kc >= c0) & (kc < c0 + NA_COLS)).astype(F32).reshape(1, GRID_W * GRID_W)
    d_col = jnp.clip(kc - qc + (NA_COLS - 1), 0, 2 * NA_COLS - 2).reshape(1, GRID_W * GRID_W)
    ndc_pad = -(-ndc // SUBLANES) * SUBLANES
    onehot = (jnp.arange(ndc_pad)[:, None] == d_col).astype(F32)
    rpb2 = jnp.pad(rpb.reshape(nh * ndr, ndc), ((0, 0), (0, ndc_pad - ndc)))
    toeplitz = pl.pallas_call(
        _bias_body,
        out_shape=jax.ShapeDtypeStruct((nh * ndr, GRID_W * GRID_W), F32),
        name="bias",
    )(rpb2, onehot, valid).reshape(nh, ndr, GRID_W, GRID_W)
    return jnp.concatenate([toeplitz[:, :-1], toeplitz[:, 1:]], axis=-1)


def _attn_body(q_ref, kp_ref, kc_ref, kn_ref, vp_ref, vc_ref, vn_ref, kx_ref, vx_ref, bias_ref, o_ref,
               kwin, vwin, *, rows, kh, n_pairs):
    i = pl.program_id(1)
    blk = ATT_RB * GRID_W
    for w, (a, b) in enumerate(((kp_ref, vp_ref), (kc_ref, vc_ref), (kn_ref, vn_ref))):
        kwin[w * blk:(w + 1) * blk, :] = a[...]
        vwin[w * blk:(w + 1) * blk, :] = b[...]
    lo = lax.broadcasted_iota(I32, (GRID_W, LANES), 1) < LANES // 2
    band = kh * GRID_W

    def row_body(rr, carry):
        r = i * ATT_RB + rr
        r0 = jnp.clip(r - kh // 2, 0, rows - kh)
        delta = r - r0
        off = pl.multiple_of((r0 - (i - 1) * ATT_RB) * GRID_W, GRID_W)
        qoff = pl.multiple_of(rr * GRID_W, GRID_W)
        for p in range(n_pairs):
            cols = slice(p * LANES, (p + 1) * LANES)
            qp = q_ref[pl.ds(qoff, GRID_W), cols]
            kb = kwin[pl.ds(off, band), cols]
            vb = vwin[pl.ds(off, band), cols]
            kx = kx_ref[:, cols]
            vx = vx_ref[:, cols]
            outs = []
            for s in range(2):
                qz = jnp.where(lo if s == 0 else jnp.logical_not(lo), qp, jnp.zeros_like(qp))
                bias = jnp.concatenate(
                    [bias_ref[2 * p + s, 2 * g + (kh - 1) - delta] for g in range(kh // 2)], axis=-1)
                s_loc = lax.dot_general(qz, kb, _NT, preferred_element_type=F32) + bias
                s_ctx = lax.dot_general(qz, kx, _NT, preferred_element_type=F32)
                m = jnp.maximum(jnp.max(s_loc, axis=-1, keepdims=True), jnp.max(s_ctx, axis=-1, keepdims=True))
                e_loc = jnp.exp(s_loc - m)
                e_ctx = jnp.exp(s_ctx - m)
                den = jnp.sum(e_loc, axis=-1, keepdims=True) + jnp.sum(e_ctx, axis=-1, keepdims=True)
                o = (jnp.dot(e_loc.astype(BF16), vb, preferred_element_type=F32)
                     + jnp.dot(e_ctx.astype(BF16), vx, preferred_element_type=F32))
                outs.append(o / den)
            o_ref[pl.ds(qoff, GRID_W), cols] = jnp.where(lo, outs[0], outs[1]).astype(BF16)
        return carry

    lax.fori_loop(0, ATT_RB, row_body, 0)


def _attn(q, k, v, kx, vx, bias):
    bsz, t, aw = q.shape
    l = kx.shape[1]
    rows = t // GRID_W
    kh = (bias.shape[1] + 2) // 2
    nb = rows // ATT_RB
    blk = ATT_RB * GRID_W
    assert kh == ATT_RB and rows % ATT_RB == 0 and aw % LANES == 0 and 2 * GRID_W == LANES
    cur = pl.BlockSpec((None, blk, aw), lambda b, i: (b, i, 0))
    prev = pl.BlockSpec((None, blk, aw), lambda b, i: (b, jnp.maximum(i - 1, 0), 0))
    nxt = pl.BlockSpec((None, blk, aw), lambda b, i: (b, jnp.minimum(i + 1, nb - 1), 0))
    cx = pl.BlockSpec((None, l, aw), lambda b, i: (b, 0, 0))
    return pl.pallas_call(
        functools.partial(_attn_body, rows=rows, kh=kh, n_pairs=aw // LANES),
        grid=(bsz, nb),
        in_specs=[cur, prev, cur, nxt, prev, cur, nxt, cx, cx,
                  pl.BlockSpec(bias.shape, lambda b, i: (0, 0, 0, 0))],
        out_specs=cur,
        out_shape=jax.ShapeDtypeStruct((bsz, t, aw), BF16),
        scratch_shapes=[pltpu.VMEM((3 * blk, aw), BF16), pltpu.VMEM((3 * blk, aw), BF16)],
        compiler_params=_params(("arbitrary", "arbitrary"), VMEM_LIMIT),
        name="attn",
    )(q, k, k, k, v, v, v, kx, vx, bias)


def _outproj_body(yc_ref, ya_ref, x_ref, wo_ref, g1_ref, gf_ref, sh_ref, sc_ref, wr_ref,
                  x1_ref, h2_ref, aff_ref, *, cw, tm):
    acc = (jnp.dot(yc_ref[...], wo_ref[0:cw, :], preferred_element_type=F32)
           + jnp.dot(ya_ref[...], wo_ref[cw:, :], preferred_element_type=F32))
    x1 = x_ref[...] + g1_ref[...] * acc
    x1_ref[...] = x1
    h2 = _norm_mod(x1, gf_ref[...], sh_ref[...], sc_ref[...])
    for s in range(h2.shape[-1] // LANES):
        h2_ref[pl.ds(s, tm, stride=SUBLANES), :] = h2[:, s * LANES:(s + 1) * LANES]
    logits = lax.dot_general(wr_ref[...], h2, _NT, preferred_element_type=F32, precision=HIGHEST)
    e = jnp.exp(logits - jnp.max(logits, axis=0, keepdims=True))
    aff_ref[...] = e / jnp.sum(e, axis=0, keepdims=True)


def _outproj(y_conv, y_attn, x, wo_bf, mods4, g_ffn, wr_t, tm):
    bsz, t, d = x.shape
    cw = y_conv.shape[-1]
    ne = wr_t.shape[0]
    assert d == SUBLANES * LANES
    nt = t // tm
    return pl.pallas_call(
        functools.partial(_outproj_body, cw=cw, tm=tm),
        grid=(bsz, nt),
        in_specs=[pl.BlockSpec((None, tm, cw), lambda b, i: (b, i, 0)),
                  pl.BlockSpec((None, tm, y_attn.shape[-1]), lambda b, i: (b, i, 0)),
                  pl.BlockSpec((None, tm, d), lambda b, i: (b, i, 0)),
                  pl.BlockSpec(wo_bf.shape, lambda b, i: (0, 0)),
                  _mod_spec(2, lambda b: b)(d),
                  pl.BlockSpec((1, d), lambda b, i: (0, 0)),
                  _mod_spec(3, lambda b: b)(d),
                  _mod_spec(4, lambda b: b)(d),
                  pl.BlockSpec(wr_t.shape, lambda b, i: (0, 0))],
        out_specs=[pl.BlockSpec((None, tm, d), lambda b, i: (b, i, 0)),
                   pl.BlockSpec((tm * SUBLANES, LANES), lambda b, i: (b * nt + i, 0)),
                   pl.BlockSpec((None, ne, tm), lambda b, i: (b, 0, i))],
        out_shape=[jax.ShapeDtypeStruct((bsz, t, d), F32),
                   jax.ShapeDtypeStruct((bsz * t * SUBLANES, LANES), F32),
                   jax.ShapeDtypeStruct((bsz, ne, t), F32)],
        compiler_params=_params(("arbitrary", "arbitrary"), VMEM_LIMIT),
        name="outproj",
    )(y_conv, y_attn, x, wo_bf, mods4, g_ffn, mods4, mods4, wr_t)


def _route_body(affe_ref, affr_ref, u_ref, ones_ref, l_ref, vtl_ref, tok_ref, cnt_ref, sel_sc, lr_sc,
                *, cap, ne, nblk):
    aff_e = affe_ref[...]

    def search(it, bits):
        cand = bits | jnp.left_shift(jnp.int32(1), 30 - it)
        cnt = jnp.sum(jnp.where(aff_e >= pltpu.bitcast(cand, F32), 1.0, 0.0), axis=1, keepdims=True)
        return jnp.where(cnt >= cap, cand, bits)

    tau = pltpu.bitcast(lax.fori_loop(0, 31, search, jnp.zeros((ne, 1), I32)), F32)
    n_gt = jnp.sum(jnp.where(aff_e > tau, 1.0, 0.0), axis=1, keepdims=True)
    need = cap - n_gt

    def per_row(col):
        return jnp.concatenate([jnp.broadcast_to(col[e:e + 1, :], (nblk, 1)) for e in range(ne)], axis=0)

    tau_r = per_row(tau)
    need_r = per_row(need)
    kr = affr_ref[...]

    def scan(mf):
        mb = mf.astype(BF16)
        incl = jnp.dot(mb, u_ref[...], preferred_element_type=F32)
        tot = jnp.dot(mb, ones_ref[...], preferred_element_type=F32)
        off = jnp.dot(l_ref[...], tot.astype(BF16), preferred_element_type=F32)
        return incl, tot, off

    eq = jnp.where(kr == tau_r, 1.0, 0.0)
    incl_e, _, off_e = scan(eq)
    rank_eq = off_e + incl_e - eq
    sel = jnp.where((kr > tau_r) | ((eq > 0.5) & (rank_eq < need_r)), 1.0, 0.0)
    incl, tot, _ = scan(sel)
    cnt_ref[...] = tot.astype(I32)
    sel_sc[...] = sel
    lr_sc[...] = incl - 1.0
    slot = lax.broadcasted_iota(I32, (LANES, LANES), 0).astype(F32)

    def compact(r, carry):
        lrb = jnp.broadcast_to(lr_sc[pl.ds(r, 1), :], (LANES, LANES))
        sb = jnp.broadcast_to(sel_sc[pl.ds(r, 1), :], (LANES, LANES))
        onehot = jnp.where((lrb == slot) & (sb > 0.5), 1.0, 0.0).astype(BF16)
        t = lax.dot_general(vtl_ref[...], onehot, _NT, preferred_element_type=F32)
        base = jnp.asarray((r % nblk) * LANES, F32)
        tok_ref[pl.ds(r, 1), :] = (t[0:1, :] + base).astype(I32)
        return carry

    lax.fori_loop(0, ne * nblk, compact, 0)


def _route(aff, cap):
    bsz, ne, t = aff.shape
    nblk = t // LANES
    nr = ne * nblk
    aff_r = aff.reshape(bsz, nr, LANES)
    ar = jnp.arange(LANES)
    upper = (ar[:, None] <= ar[None, :]).astype(BF16)
    ones = jnp.ones((LANES, LANES), BF16)
    rr = jnp.arange(nr)
    lower = ((rr[:, None] // nblk == rr[None, :] // nblk) & (rr[None, :] < rr[:, None])).astype(BF16)
    vtl = jnp.broadcast_to(ar.astype(BF16)[None, :], (SUBLANES, LANES))
    const = lambda a: pl.BlockSpec(a.shape, lambda b: (0,) * a.ndim)
    rows = pl.BlockSpec((None, nr, LANES), lambda b: (b, 0, 0))
    return pl.pallas_call(
        functools.partial(_route_body, cap=cap, ne=ne, nblk=nblk),
        grid=(bsz,),
        in_specs=[pl.BlockSpec((None, ne, t), lambda b: (b, 0, 0)), rows,
                  const(upper), const(ones), const(lower), const(vtl)],
        out_specs=[rows, rows],
        out_shape=[jax.ShapeDtypeStruct((bsz, nr, LANES), I32), jax.ShapeDtypeStruct((bsz, nr, LANES), I32)],
        scratch_shapes=[pltpu.VMEM((nr, LANES), F32), pltpu.VMEM((nr, LANES), F32)],
        compiler_params=_params(("arbitrary",), VMEM_LIMIT),
        name="route",
    )(aff, aff_r, upper, ones, lower, vtl)


MOE_HC = 512
MOE_TM = 256


def _gather_copy(h2_hbm, xs3, sem, token, slot):
    src = h2_hbm.at[pl.ds(pl.multiple_of(token * SUBLANES, SUBLANES), SUBLANES), :]
    dst = xs3.at[pl.ds(pl.multiple_of(slot * SUBLANES, SUBLANES), SUBLANES), :]
    return pltpu.make_async_copy(src, dst, sem)


def _moe_body(tok_ref, cnt_ref, h2_hbm, wg_ref, wu_ref, wd_ref, y_ref, xs3, xsb, wgb, wub, wdb, sem,
              *, bsz, t, cap, nblk):
    c = pl.program_id(1)

    @pl.when(c == 0)
    def _gather():
        for b in range(bsz):
            def block(j, slot):
                def one(ls, s):
                    token = b * t + tok_ref[b, 0, j * LANES + ls]
                    _gather_copy(h2_hbm, xs3, sem, token, s).start()
                    return s + 1
                return lax.fori_loop(0, cnt_ref[b, 0, j], one, slot)

            lax.fori_loop(0, nblk, block, 0)
            pltpu.make_async_copy(h2_hbm.at[pl.ds(0, cap * SUBLANES), :], xs3, sem).wait()
            for s in range(SUBLANES):
                xsb[b * cap:(b + 1) * cap, s * LANES:(s + 1) * LANES] = (
                    xs3[pl.ds(s, cap, stride=SUBLANES), :].astype(BF16))

    wgb[...] = wg_ref[...].astype(BF16)
    wub[...] = wu_ref[...].astype(BF16)
    wdb[...] = wd_ref[...].astype(BF16)

    def tile(m, carry):
        rows = pl.ds(pl.multiple_of(m * MOE_TM, MOE_TM), MOE_TM)
        xm = xsb[rows, :]
        gate = jnp.dot(xm, wgb[...], preferred_element_type=F32)
        up = jnp.dot(xm, wub[...], preferred_element_type=F32)
        hid = (gate * jax.nn.sigmoid(gate) * up).astype(BF16)
        contrib = jnp.dot(hid, wdb[...], preferred_element_type=F32)

        @pl.when(c == 0)
        def _():
            y_ref[rows, :] = contrib

        @pl.when(c > 0)
        def _():
            y_ref[rows, :] += contrib

        return carry

    lax.fori_loop(0, bsz * cap // MOE_TM, tile, 0)


def _moe(tok, cnt, h2_tiles, w_gate, w_up, w_down, bsz, t, cap):
    ne, d, dh = w_gate.shape
    nblk = t // LANES
    nch = dh // MOE_HC
    smem = lambda n: pl.BlockSpec((bsz, None, 1, n), lambda e, c: (0, e, 0, 0), memory_space=pltpu.SMEM)
    return pl.pallas_call(
        functools.partial(_moe_body, bsz=bsz, t=t, cap=cap, nblk=nblk),
        grid=(ne, nch),
        in_specs=[smem(t), smem(nblk),
                  pl.BlockSpec(memory_space=pl.ANY),
                  pl.BlockSpec((None, d, MOE_HC), lambda e, c: (e, 0, c)),
                  pl.BlockSpec((None, d, MOE_HC), lambda e, c: (e, 0, c)),
                  pl.BlockSpec((None, MOE_HC, d), lambda e, c: (e, c, 0))],
        out_specs=pl.BlockSpec((None, bsz * cap, d), lambda e, c: (e, 0, 0)),
        out_shape=jax.ShapeDtypeStruct((ne, bsz * cap, d), F32),
        scratch_shapes=[pltpu.VMEM((cap * SUBLANES, LANES), F32),
                        pltpu.VMEM((bsz * cap, d), BF16),
                        pltpu.VMEM((d, MOE_HC), BF16),
                        pltpu.VMEM((d, MOE_HC), BF16),
                        pltpu.VMEM((MOE_HC, d), BF16),
                        pltpu.SemaphoreType.DMA],
        compiler_params=_params(("arbitrary", "arbitrary"), VMEM_LIMIT),
        name="moe",
    )(tok, cnt, h2_tiles, w_gate, w_up, w_down)


def _combine_body(tok_ref, cnt_ref, aff_ref, y_ref, acc_hbm, acc, ysc, sem, *, t, cap, nblk, ne):
    e = pl.program_id(1)
    b = pl.program_id(0)

    @pl.when(e == 0)
    def _():
        acc[...] = jnp.zeros_like(acc)

    for s in range(SUBLANES):
        ysc[pl.ds(s, cap, stride=SUBLANES), :] = y_ref[:, s * LANES:(s + 1) * LANES]

    def block(j, slot):
        def one(ls, s):
            token = tok_ref[0, j * LANES + ls]
            w = aff_ref[0, token]
            dst = pl.ds(pl.multiple_of(token * SUBLANES, SUBLANES), SUBLANES)
            src = pl.ds(pl.multiple_of(s * SUBLANES, SUBLANES), SUBLANES)
            acc[dst, :] += w * ysc[src, :]
            return s + 1
        return lax.fori_loop(0, cnt_ref[0, j], one, slot)

    lax.fori_loop(0, nblk, block, 0)

    @pl.when(e == ne - 1)
    def _():
        out = acc_hbm.at[pl.ds(pl.multiple_of(b * t * SUBLANES, SUBLANES), t * SUBLANES), :]
        copy = pltpu.make_async_copy(acc, out, sem)
        copy.start()
        copy.wait()


def _combine(tok, cnt, aff, y, bsz, t, cap):
    ne = aff.shape[1]
    d = y.shape[-1]
    nblk = t // LANES
    smem = lambda n: pl.BlockSpec((None, None, 1, n), lambda b, e: (b, e, 0, 0), memory_space=pltpu.SMEM)
    return pl.pallas_call(
        functools.partial(_combine_body, t=t, cap=cap, nblk=nblk, ne=ne),
        grid=(bsz, ne),
        in_specs=[smem(t), smem(nblk), smem(t),
                  pl.BlockSpec((None, cap, d), lambda b, e: (e, b, 0))],
        out_specs=pl.BlockSpec(memory_space=pl.ANY),
        out_shape=jax.ShapeDtypeStruct((bsz * t * SUBLANES, LANES), F32),
        scratch_shapes=[pltpu.VMEM((t * SUBLANES, LANES), F32),
                        pltpu.VMEM((cap * SUBLANES, LANES), F32),
                        pltpu.SemaphoreType.DMA],
        compiler_params=_params(("arbitrary", "arbitrary"), VMEM_LIMIT),
        name="combine",
    )(tok, cnt, aff, y)


def _final_body(x1_ref, acc_ref, g2_ref, gf_ref, o_ref, *, tm):
    moe = jnp.concatenate([acc_ref[pl.ds(s, tm, stride=SUBLANES), :] for s in range(SUBLANES)], axis=-1)
    x = x1_ref[...] + g2_ref[...] * moe
    ms = jnp.mean(x * x, axis=-1, keepdims=True)
    o_ref[...] = x * lax.rsqrt(ms + EPS) * gf_ref[...]


def _final(x1, acc_tiles, mods4, g_final, tm):
    bsz, t, d = x1.shape
    nt = t // tm
    return pl.pallas_call(
        functools.partial(_final_body, tm=tm),
        grid=(bsz, nt),
        in_specs=[pl.BlockSpec((None, tm, d), lambda b, i: (b, i, 0)),
                  pl.BlockSpec((tm * SUBLANES, LANES), lambda b, i: (b * nt + i, 0)),
                  _mod_spec(5, lambda b: b)(d),
                  pl.BlockSpec((1, d), lambda b, i: (0, 0))],
        out_specs=pl.BlockSpec((None, tm, d), lambda b, i: (b, i, 0)),
        out_shape=jax.ShapeDtypeStruct((bsz, t, d), F32),
        compiler_params=_params(("arbitrary", "arbitrary")),
        name="final",
    )(x1, acc_tiles, mods4, g_final)


def _layer(x, ctx, mods4, ctx_row, norm_mix_g, w_in, conv_w, conv_b, conv_ln_g, conv_ln_b, rpb, w_out,
           norm_ffn_g, w_router, w_gate, w_up, w_down):
    bsz, t, d = x.shape
    cw = conv_w.shape[1]
    n_heads = rpb.shape[0]
    aw = (w_in.shape[1] - 2 * cw) // 3
    head_dim = aw // n_heads
    rows = t // GRID_W
    kh = min((rpb.shape[1] + 1) // 2, rows)
    ne = w_router.shape[1]
    cap = EC_FACTOR * t // ne
    row = lambda a: a.reshape(1, -1)

    w_in_bf = w_in.astype(BF16)
    u, q, k, v = _inproj(x, row(norm_mix_g), mods4, w_in_bf, cw, aw, head_dim ** -0.5, tm=512)
    kx, vx = _inproj_ctx(ctx, row(norm_mix_g), mods4, w_in_bf[:, 2 * cw + aw:], aw, ctx_row)
    y_conv = _conv(u, conv_w, conv_b, conv_ln_g, conv_ln_b, tt=256)
    assert rows >= (rpb.shape[1] + 1) // 2
    y_attn = _attn(q, k, v, kx, vx, _bias_table(rpb))
    x1, h2_tiles, aff = _outproj(y_conv, y_attn, x, w_out.astype(BF16), mods4, row(norm_ffn_g),
                                 w_router.T, tm=512)
    tok, cnt = _route(aff, cap)
    tok = tok.reshape(bsz, ne, 1, t)
    cnt = cnt[:, :, 0].reshape(bsz, ne, 1, t // LANES)
    y = _moe(tok, cnt, h2_tiles, w_gate, w_up, w_down, bsz, t, cap)
    acc_tiles = _combine(tok, cnt, aff.reshape(bsz, ne, 1, t), y, bsz, t, cap)
    return x1, acc_tiles


def kernel(x, c, ctx, c_ctx, w_mod, b_mod, norm_mix_g, w_in, conv_w, conv_b, conv_ln_g, conv_ln_b, rpb,
           w_out, norm_ffn_g, w_router, w_gate, w_up, w_down, norm_final_g):
    bsz, t, d = x.shape
    depth = w_mod.shape[0]
    assert depth == 1
    assert bsz + 1 <= SUBLANES
    c8 = jnp.zeros((SUBLANES, d), F32).at[:bsz].set(c).at[bsz].set(c_ctx)
    mods4 = _adaln(c8, w_mod[0], b_mod[0].reshape(1, -1)).reshape(SUBLANES, N_MOD, 1, d)
    x1, acc_tiles = _layer(x, ctx, mods4, bsz, norm_mix_g[0], w_in[0], conv_w[0], conv_b[0], conv_ln_g[0],
                           conv_ln_b[0], rpb[0], w_out[0], norm_ffn_g[0], w_router[0], w_gate[0], w_up[0],
                           w_down[0])
    return _final(x1, acc_tiles, mods4, norm_final_g.reshape(1, -1), tm=512)
```

```python
import functools

import jax
import jax.numpy as jnp
from jax import lax
from jax.experimental import pallas as pl
from jax.experimental.pallas import tpu as pltpu

F32 = jnp.float32
BF16 = jnp.bfloat16
I32 = jnp.int32
HIGHEST = lax.Precision.HIGHEST

GRID_W = 64
NA_COLS = 16
EC_FACTOR = 2
N_MOD = 6
EPS = 1e-6
NEG_INF = -1e30
LANES = 128
SUBLANES = 8
VMEM_LIMIT = 56 * 1024 * 1024

_NT = (((1,), (1,)), ((), ()))


def _params(sem, vmem=None):
    return pltpu.CompilerParams(dimension_semantics=sem, vmem_limit_bytes=vmem)


def _adaln_body(c_ref, w_ref, b_ref, o_ref):
    c = c_ref[...]
    s = c * jax.nn.sigmoid(c)
    o_ref[...] = jnp.dot(s, w_ref[...], preferred_element_type=F32, precision=HIGHEST) + b_ref[...]


def _adaln(c8, w_mod, b_mod):
    d, n = w_mod.shape
    bn = 512
    return pl.pallas_call(
        _adaln_body,
        grid=(n // bn,),
        in_specs=[pl.BlockSpec((SUBLANES, d), lambda j: (0, 0)),
                  pl.BlockSpec((d, bn), lambda j: (0, j)),
                  pl.BlockSpec((1, bn), lambda j: (0, j))],
        out_specs=pl.BlockSpec((SUBLANES, bn), lambda j: (0, j)),
        out_shape=jax.ShapeDtypeStruct((SUBLANES, n), F32),
        compiler_params=_params(("arbitrary",)),
        name="adaln",
    )(c8, w_mod, b_mod)


def _norm_mod(x, g, shift, scale):
    ms = jnp.mean(x * x, axis=-1, keepdims=True)
    h = x * lax.rsqrt(ms + EPS) * g
    return h * (1.0 + scale) + shift


def _inproj_body(x_ref, g_ref, sh_ref, sc_ref, w_ref, u_ref, q_ref, k_ref, v_ref, *, cw, aw, qscale):
    hb = _norm_mod(x_ref[...], g_ref[...], sh_ref[...], sc_ref[...]).astype(BF16)

    def proj(lo, width):
        return jnp.dot(hb, w_ref[:, lo:lo + width], preferred_element_type=F32)

    a = proj(0, cw)
    gate = proj(cw, cw)
    u_ref[...] = a * jax.nn.sigmoid(gate)
    q_ref[...] = (proj(2 * cw, aw) * qscale).astype(BF16)
    k_ref[...] = proj(2 * cw + aw, aw).astype(BF16)
    v_ref[...] = proj(2 * cw + 2 * aw, aw).astype(BF16)


def _inproj_kv_body(x_ref, g_ref, sh_ref, sc_ref, w_ref, k_ref, v_ref, *, aw):
    hb = _norm_mod(x_ref[...], g_ref[...], sh_ref[...], sc_ref[...]).astype(BF16)
    k_ref[...] = jnp.dot(hb, w_ref[:, 0:aw], preferred_element_type=F32).astype(BF16)
    v_ref[...] = jnp.dot(hb, w_ref[:, aw:2 * aw], preferred_element_type=F32).astype(BF16)


def _mod_spec(slot, row_of):
    return lambda d: pl.BlockSpec((None, None, 1, d), lambda b, i: (row_of(b), slot, 0, 0))


def _inproj(x, g, mods4, w_bf, cw, aw, qscale, tm):
    bsz, t, d = x.shape
    tok = lambda w, dt: (pl.BlockSpec((None, tm, w), lambda b, i: (b, i, 0)), jax.ShapeDtypeStruct((bsz, t, w), dt))
    outs = [tok(cw, F32), tok(aw, BF16), tok(aw, BF16), tok(aw, BF16)]
    return pl.pallas_call(
        functools.partial(_inproj_body, cw=cw, aw=aw, qscale=qscale),
        grid=(bsz, t // tm),
        in_specs=[pl.BlockSpec((None, tm, d), lambda b, i: (b, i, 0)),
                  pl.BlockSpec((1, d), lambda b, i: (0, 0)),
                  _mod_spec(0, lambda b: b)(d),
                  _mod_spec(1, lambda b: b)(d),
                  pl.BlockSpec(w_bf.shape, lambda b, i: (0, 0))],
        out_specs=[o[0] for o in outs],
        out_shape=[o[1] for o in outs],
        compiler_params=_params(("arbitrary", "arbitrary"), VMEM_LIMIT),
        name="inproj",
    )(x, g, mods4, mods4, w_bf)


def _inproj_ctx(ctx, g, mods4, w_kv_bf, aw, ctx_row):
    bsz, l, d = ctx.shape
    spec = pl.BlockSpec((None, l, aw), lambda b, i: (b, 0, 0))
    shape = jax.ShapeDtypeStruct((bsz, l, aw), BF16)
    return pl.pallas_call(
        functools.partial(_inproj_kv_body, aw=aw),
        grid=(bsz, 1),
        in_specs=[pl.BlockSpec((None, l, d), lambda b, i: (b, 0, 0)),
                  pl.BlockSpec((1, d), lambda b, i: (0, 0)),
                  _mod_spec(0, lambda b: ctx_row)(d),
                  _mod_spec(1, lambda b: ctx_row)(d),
                  pl.BlockSpec(w_kv_bf.shape, lambda b, i: (0, 0))],
        out_specs=[spec, spec],
        out_shape=[shape, shape],
        compiler_params=_params(("arbitrary", "arbitrary"), VMEM_LIMIT),
        name="inproj_ctx",
    )(ctx, g, mods4, mods4, w_kv_bf)


CONV_HALO = 16
CONV_RC = 32


def _conv_body(prev_ref, cur_ref, next_ref, w_ref, b_ref, lg_ref, lb_ref, o_ref, ext, *, width, tt):
    i = pl.program_id(1)
    n = pl.num_programs(1)
    ext[0:CONV_HALO, :] = jnp.where(i > 0, prev_ref[...], 0.0)
    ext[CONV_HALO:CONV_HALO + tt, :] = cur_ref[...]
    ext[CONV_HALO + tt:, :] = jnp.where(i < n - 1, next_ref[...], 0.0)
    half = width // 2
    for rc in range(tt // CONV_RC):
        base = rc * CONV_RC + CONV_HALO - half
        acc = jnp.zeros((CONV_RC, cur_ref.shape[-1]), F32)
        for j in range(width):
            acc = acc + w_ref[j:j + 1, :] * ext[base + j:base + j + CONV_RC, :]
        y = acc + b_ref[...]
        mu = jnp.mean(y, axis=-1, keepdims=True)
        yc = y - mu
        var = jnp.mean(yc * yc, axis=-1, keepdims=True)
        yn = yc * lax.rsqrt(var + EPS) * lg_ref[...] + lb_ref[...]
        o_ref[rc * CONV_RC:(rc + 1) * CONV_RC, :] = (yn * jax.nn.sigmoid(yn)).astype(BF16)


def _conv(u, conv_w, conv_b, ln_g, ln_b, tt):
    bsz, t, cw = u.shape
    width = conv_w.shape[0]
    assert width // 2 < CONV_HALO and tt % CONV_HALO == 0 and tt % CONV_RC == 0
    hb = tt // CONV_HALO
    nh = t // CONV_HALO
    row = lambda a: a.reshape(1, cw)
    return pl.pallas_call(
        functools.partial(_conv_body, width=width, tt=tt),
        grid=(bsz, t // tt),
        in_specs=[pl.BlockSpec((None, CONV_HALO, cw), lambda b, i: (b, jnp.maximum(i * hb - 1, 0), 0)),
                  pl.BlockSpec((None, tt, cw), lambda b, i: (b, i, 0)),
                  pl.BlockSpec((None, CONV_HALO, cw), lambda b, i: (b, jnp.minimum((i + 1) * hb, nh - 1), 0)),
                  pl.BlockSpec((width, cw), lambda b, i: (0, 0)),
                  pl.BlockSpec((1, cw), lambda b, i: (0, 0)),
                  pl.BlockSpec((1, cw), lambda b, i: (0, 0)),
                  pl.BlockSpec((1, cw), lambda b, i: (0, 0))],
        out_specs=pl.BlockSpec((None, tt, cw), lambda b, i: (b, i, 0)),
        out_shape=jax.ShapeDtypeStruct((bsz, t, cw), BF16),
        scratch_shapes=[pltpu.VMEM((tt + 2 * CONV_HALO, cw), F32)],
        compiler_params=_params(("arbitrary", "arbitrary")),
        name="conv",
    )(u, u, u, conv_w, row(conv_b), row(ln_g), row(ln_b))


ATT_RB = 8


def _bias_body(rpb_ref, onehot_ref, valid_ref, o_ref):
    t = jnp.dot(rpb_ref[...], onehot_ref[...], preferred_element_type=F32, precision=HIGHEST)
    o_ref[...] = jnp.where(valid_ref[...] > 0.5, t, NEG_INF)


def _bias_table(rpb):
    nh, ndr, ndc = rpb.shape
    qc = jnp.arange(GRID_W)[:, None]
    kc = jnp.arange(GRID_W)[None, :]
    c0 = jnp.clip(qc - NA_COLS // 2, 0, GRID_W - NA_COLS)
    valid = ((kc >= c0) & (kc < c0 + NA_COLS)).astype(F32).reshape(1, GRID_W * GRID_W)
    d_col = jnp.clip(kc - qc + (NA_COLS - 1), 0, 2 * NA_COLS - 2).reshape(1, GRID_W * GRID_W)
    ndc_pad = -(-ndc // SUBLANES) * SUBLANES
    onehot = (jnp.arange(ndc_pad)[:, None] == d_col).astype(F32)
    rpb2 = jnp.pad(rpb.reshape(nh * ndr, ndc), ((0, 0), (0, ndc_pad - ndc)))
    toeplitz = pl.pallas_call(
        _bias_body,
        out_shape=jax.ShapeDtypeStruct((nh * ndr, GRID_W * GRID_W), F32),
        name="bias",
    )(rpb2, onehot, valid).reshape(nh, ndr, GRID_W, GRID_W)
    return jnp.concatenate([toeplitz[:, :-1], toeplitz[:, 1:]], axis=-1)


def _attn_body(q_ref, kp_ref, kc_ref, kn_ref, vp_ref, vc_ref, vn_ref, kx_ref, vx_ref, bias_ref, o_ref,
               kwin, vwin, s_sc, p_sc, m_sc, l_sc, *, rows, kh, n_pairs):
    i = pl.program_id(1)
    blk = ATT_RB * GRID_W
    for w, (a, b) in enumerate(((kp_ref, vp_ref), (kc_ref, vc_ref), (kn_ref, vn_ref))):
        kwin[w * blk:(w + 1) * blk, :] = a[...]
        vwin[w * blk:(w + 1) * blk, :] = b[...]
    lo = lax.broadcasted_iota(I32, (GRID_W, LANES), 1) < LANES // 2
    band = kh * GRID_W

    def row_body(rr, carry):
        r = i * ATT_RB + rr
        r0 = jnp.clip(r - kh // 2, 0, rows - kh)
        delta = r - r0
        off = pl.multiple_of((r0 - (i - 1) * ATT_RB) * GRID_W, GRID_W)
        qoff = pl.multiple_of(rr * GRID_W, GRID_W)
        def lane_groups(x, op):
            parts = [x[:, g * LANES:(g + 1) * LANES] for g in range(x.shape[-1] // LANES)]
            while len(parts) > 1:
                parts = [op(parts[a], parts[a + 1]) for a in range(0, len(parts) - 1, 2)] + parts[len(parts) & ~1:]
            return parts[0]

        for p in range(n_pairs):
            cols = slice(p * LANES, (p + 1) * LANES)
            qp = q_ref[pl.ds(qoff, GRID_W), cols]
            kb = kwin[pl.ds(off, band), cols]
            kx = kx_ref[:, cols]
            for s in range(2):
                h = 2 * p + s
                qz = jnp.where(lo if s == 0 else jnp.logical_not(lo), qp, jnp.zeros_like(qp))
                bias = jnp.concatenate(
                    [bias_ref[h, 2 * g + (kh - 1) - delta] for g in range(kh // 2)], axis=-1)
                s_loc = lax.dot_general(qz, kb, _NT, preferred_element_type=F32) + bias
                s_ctx = lax.dot_general(qz, kx, _NT, preferred_element_type=F32)
                s_sc[h, :, 0:band] = s_loc
                s_sc[h, :, band:] = s_ctx
                m_sc[h] = jnp.maximum(lane_groups(s_loc, jnp.maximum), lane_groups(s_ctx, jnp.maximum))
        for h in range(2 * n_pairs):
            m = jnp.max(m_sc[h], axis=-1, keepdims=True)
            e = jnp.exp(s_sc[h] - m)
            p_sc[h] = e.astype(BF16)
            l_sc[h] = lane_groups(e, jnp.add)
        for p in range(n_pairs):
            cols = slice(p * LANES, (p + 1) * LANES)
            vb = vwin[pl.ds(off, band), cols]
            vx = vx_ref[:, cols]
            outs = []
            for s in range(2):
                h = 2 * p + s
                o = (jnp.dot(p_sc[h, :, 0:band], vb, preferred_element_type=F32)
                     + jnp.dot(p_sc[h, :, band:], vx, preferred_element_type=F32))
                outs.append(o / jnp.sum(l_sc[h], axis=-1, keepdims=True))
            o_ref[pl.ds(qoff, GRID_W), cols] = jnp.where(lo, outs[0], outs[1]).astype(BF16)
        return carry

    lax.fori_loop(0, ATT_RB, row_body, 0)


def _attn(q, k, v, kx, vx, bias):
    bsz, t, aw = q.shape
    l = kx.shape[1]
    rows = t // GRID_W
    kh = (bias.shape[1] + 2) // 2
    nb = rows // ATT_RB
    blk = ATT_RB * GRID_W
    assert kh == ATT_RB and rows % ATT_RB == 0 and aw % LANES == 0 and 2 * GRID_W == LANES
    cur = pl.BlockSpec((None, blk, aw), lambda b, i: (b, i, 0))
    prev = pl.BlockSpec((None, blk, aw), lambda b, i: (b, jnp.maximum(i - 1, 0), 0))
    nxt = pl.BlockSpec((None, blk, aw), lambda b, i: (b, jnp.minimum(i + 1, nb - 1), 0))
    cx = pl.BlockSpec((None, l, aw), lambda b, i: (b, 0, 0))
    nh = bias.shape[0]
    nkeys = kh * GRID_W + l
    assert l % LANES == 0 and nh == 2 * (aw // LANES)
    return pl.pallas_call(
        functools.partial(_attn_body, rows=rows, kh=kh, n_pairs=aw // LANES),
        grid=(bsz, nb),
        in_specs=[cur, prev, cur, nxt, prev, cur, nxt, cx, cx,
                  pl.BlockSpec(bias.shape, lambda b, i: (0, 0, 0, 0))],
        out_specs=cur,
        out_shape=jax.ShapeDtypeStruct((bsz, t, aw), BF16),
        scratch_shapes=[pltpu.VMEM((3 * blk, aw), BF16), pltpu.VMEM((3 * blk, aw), BF16),
                        pltpu.VMEM((nh, GRID_W, nkeys), F32), pltpu.VMEM((nh, GRID_W, nkeys), BF16),
                        pltpu.VMEM((nh, GRID_W, LANES), F32), pltpu.VMEM((nh, GRID_W, LANES), F32)],
        compiler_params=_params(("arbitrary", "arbitrary"), VMEM_LIMIT),
        name="attn",
    )(q, k, k, k, v, v, v, kx, vx, bias)


def _outproj_body(yc_ref, ya_ref, x_ref, wo_ref, g1_ref, gf_ref, sh_ref, sc_ref, wr_ref,
                  x1_ref, h2_ref, aff_ref, *, cw, tm):
    acc = (jnp.dot(yc_ref[...], wo_ref[0:cw, :], preferred_element_type=F32)
           + jnp.dot(ya_ref[...], wo_ref[cw:, :], preferred_element_type=F32))
    x1 = x_ref[...] + g1_ref[...] * acc
    x1_ref[...] = x1
    h2 = _norm_mod(x1, gf_ref[...], sh_ref[...], sc_ref[...])
    for s in range(h2.shape[-1] // LANES):
        h2_ref[pl.ds(s, tm, stride=SUBLANES), :] = h2[:, s * LANES:(s + 1) * LANES]
    logits = lax.dot_general(wr_ref[...], h2, _NT, preferred_element_type=F32, precision=HIGHEST)
    e = jnp.exp(logits - jnp.max(logits, axis=0, keepdims=True))
    aff_ref[...] = e / jnp.sum(e, axis=0, keepdims=True)


def _outproj(y_conv, y_attn, x, wo_bf, mods4, g_ffn, wr_t, tm):
    bsz, t, d = x.shape
    cw = y_conv.shape[-1]
    ne = wr_t.shape[0]
    assert d == SUBLANES * LANES
    nt = t // tm
    return pl.pallas_call(
        functools.partial(_outproj_body, cw=cw, tm=tm),
        grid=(bsz, nt),
        in_specs=[pl.BlockSpec((None, tm, cw), lambda b, i: (b, i, 0)),
                  pl.BlockSpec((None, tm, y_attn.shape[-1]), lambda b, i: (b, i, 0)),
                  pl.BlockSpec((None, tm, d), lambda b, i: (b, i, 0)),
                  pl.BlockSpec(wo_bf.shape, lambda b, i: (0, 0)),
                  _mod_spec(2, lambda b: b)(d),
                  pl.BlockSpec((1, d), lambda b, i: (0, 0)),
                  _mod_spec(3, lambda b: b)(d),
                  _mod_spec(4, lambda b: b)(d),
                  pl.BlockSpec(wr_t.shape, lambda b, i: (0, 0))],
        out_specs=[pl.BlockSpec((None, tm, d), lambda b, i: (b, i, 0)),
                   pl.BlockSpec((tm * SUBLANES, LANES), lambda b, i: (b * nt + i, 0)),
                   pl.BlockSpec((None, ne, tm), lambda b, i: (b, 0, i))],
        out_shape=[jax.ShapeDtypeStruct((bsz, t, d), F32),
                   jax.ShapeDtypeStruct((bsz * t * SUBLANES, LANES), F32),
                   jax.ShapeDtypeStruct((bsz, ne, t), F32)],
        compiler_params=_params(("arbitrary", "arbitrary"), VMEM_LIMIT),
        name="outproj",
    )(y_conv, y_attn, x, wo_bf, mods4, g_ffn, mods4, mods4, wr_t)


def _route_body(affe_ref, affr_ref, u_ref, ones_ref, l_ref, tok_ref, *, cap, ne, nblk):
    aff_e = affe_ref[...]

    def search(it, bits):
        cand = bits | jnp.left_shift(jnp.int32(1), 30 - it)
        cnt = jnp.sum(jnp.where(aff_e >= pltpu.bitcast(cand, F32), 1.0, 0.0), axis=1, keepdims=True)
        return jnp.where(cnt >= cap, cand, bits)

    tau = pltpu.bitcast(lax.fori_loop(0, 31, search, jnp.zeros((ne, 1), I32)), F32)
    n_gt = jnp.sum(jnp.where(aff_e > tau, 1.0, 0.0), axis=1, keepdims=True)
    need = cap - n_gt

    def per_row(col):
        return jnp.concatenate([jnp.broadcast_to(col[e:e + 1, :], (nblk, 1)) for e in range(ne)], axis=0)

    tau_r = per_row(tau)
    need_r = per_row(need)
    kr = affr_ref[...]

    def scan(mf):
        mb = mf.astype(BF16)
        incl = jnp.dot(mb, u_ref[...], preferred_element_type=F32)
        tot = jnp.dot(mb, ones_ref[...], preferred_element_type=F32)
        off = jnp.dot(l_ref[...], tot.astype(BF16), preferred_element_type=F32)
        return incl, tot, off

    eq = jnp.where(kr == tau_r, 1.0, 0.0)
    incl_e, _, off_e = scan(eq)
    rank_eq = off_e + incl_e - eq
    sel = jnp.where((kr > tau_r) | ((eq > 0.5) & (rank_eq < need_r)), 1.0, 0.0)
    incl, _, off = scan(sel)

    nbits = LANES.bit_length() - 1
    lane = lax.broadcasted_iota(I32, kr.shape, 1)
    row = lax.broadcasted_iota(I32, kr.shape, 0)
    dist = lane - (incl.astype(I32) - 1)
    word = jnp.where(sel > 0.5, (1 << (2 * nbits)) | (dist << nbits) | lane, 0)

    def bit(x, pos):
        return (x >> pos) & 1

    for k in range(nbits):
        arriving = pltpu.roll(word, LANES - (1 << k), axis=1)
        take = bit(arriving, 2 * nbits) & bit(arriving, nbits + k)
        stay = bit(word, 2 * nbits) & (1 - bit(word, nbits + k))
        word = jnp.where(take == 1, arriving, jnp.where(stay == 1, word, 0))

    off_i = off.astype(I32)
    shift = off_i & (LANES - 1)
    for k in range(nbits):
        word = jnp.where(bit(shift, k) == 1, pltpu.roll(word, 1 << k, axis=1), word)
    valid = bit(word, 2 * nbits) == 1
    wrapped = lane < shift
    blk_bits = nblk.bit_length() - 1
    q0 = (row >> blk_bits) * (cap // LANES) + (off_i >> nbits)
    digits = (jnp.where(valid, row & (nblk - 1), 0), jnp.where(valid, word & (LANES - 1), 0))

    def place(sel_rows, q):
        onehot_t = jnp.where(q == lane, 1.0, 0.0).astype(BF16)
        outs = []
        for dgt in digits:
            x = jnp.where(sel_rows, dgt, 0).astype(F32).astype(BF16)
            outs.append(lax.dot_general(onehot_t, x, (((0,), (0,)), ((), ())), preferred_element_type=F32))
        return outs[0] * LANES + outs[1]

    tok_ref[...] = (place(jnp.logical_not(wrapped), q0) + place(wrapped, q0 + 1)).astype(I32)


def _route(aff, cap):
    bsz, ne, t = aff.shape
    nblk = t // LANES
    nr = ne * nblk
    nq = ne * cap // LANES
    assert nq == LANES and cap % LANES == 0 and nblk & (nblk - 1) == 0 and nblk <= 256
    aff_r = aff.reshape(bsz, nr, LANES)
    ar = jnp.arange(LANES)
    upper = (ar[:, None] <= ar[None, :]).astype(BF16)
    ones = jnp.ones((LANES, LANES), BF16)
    rr = jnp.arange(nr)
    lower = ((rr[:, None] // nblk == rr[None, :] // nblk) & (rr[None, :] < rr[:, None])).astype(BF16)
    const = lambda a: pl.BlockSpec(a.shape, lambda b: (0,) * a.ndim)
    return pl.pallas_call(
        functools.partial(_route_body, cap=cap, ne=ne, nblk=nblk),
        grid=(bsz,),
        in_specs=[pl.BlockSpec((None, ne, t), lambda b: (b, 0, 0)),
                  pl.BlockSpec((None, nr, LANES), lambda b: (b, 0, 0)),
                  const(upper), const(ones), const(lower)],
        out_specs=pl.BlockSpec((None, nq, LANES), lambda b: (b, 0, 0)),
        out_shape=jax.ShapeDtypeStruct((bsz, nq, LANES), I32),
        compiler_params=_params(("arbitrary",), VMEM_LIMIT),
        name="route",
    )(aff, aff_r, upper, ones, lower)


MOE_HC = 512
MOE_TM = 512
ROW_UNROLL = 8


def _gather_copy(h2_hbm, xs3, sem, token, slot):
    src = h2_hbm.at[pl.ds(pl.multiple_of(token * SUBLANES, SUBLANES), SUBLANES), :]
    dst = xs3.at[pl.ds(pl.multiple_of(slot * SUBLANES, SUBLANES), SUBLANES), :]
    return pltpu.make_async_copy(src, dst, sem)


def _moe_body(tok_ref, h2_hbm, wg_ref, wu_ref, wd_ref, y_ref, xs3, xsb, wgb, wub, wdb, sem,
              *, bsz, t, cap):
    c = pl.program_id(1)

    @pl.when(c == 0)
    def _gather():
        for b in range(bsz):
            def rows(g, carry):
                for k in range(ROW_UNROLL):
                    slot = g * ROW_UNROLL + k
                    _gather_copy(h2_hbm, xs3, sem, b * t + tok_ref[b, 0, slot], slot).start()
                return carry

            lax.fori_loop(0, cap // ROW_UNROLL, rows, 0)
            pltpu.make_async_copy(h2_hbm.at[pl.ds(0, cap * SUBLANES), :], xs3, sem).wait()
            for s in range(SUBLANES):
                xsb[b * cap:(b + 1) * cap, s * LANES:(s + 1) * LANES] = (
                    xs3[pl.ds(s, cap, stride=SUBLANES), :].astype(BF16))

    wgb[...] = wg_ref[...].astype(BF16)
    wub[...] = wu_ref[...].astype(BF16)
    wdb[...] = wd_ref[...].astype(BF16)

    def tile(m, carry):
        rows = pl.ds(pl.multiple_of(m * MOE_TM, MOE_TM), MOE_TM)
        xm = xsb[rows, :]
        gate = jnp.dot(xm, wgb[...], preferred_element_type=F32)
        up = jnp.dot(xm, wub[...], preferred_element_type=F32)
        hid = (gate * jax.nn.sigmoid(gate) * up).astype(BF16)
        contrib = jnp.dot(hid, wdb[...], preferred_element_type=F32)

        @pl.when(c == 0)
        def _():
            y_ref[rows, :] = contrib

        @pl.when(c > 0)
        def _():
            y_ref[rows, :] += contrib

        return carry

    lax.fori_loop(0, bsz * cap // MOE_TM, tile, 0)


def _moe(tok, h2_tiles, w_gate, w_up, w_down, bsz, t, cap):
    ne, d, dh = w_gate.shape
    nch = dh // MOE_HC
    assert cap % ROW_UNROLL == 0 and (bsz * cap) % MOE_TM == 0
    return pl.pallas_call(
        functools.partial(_moe_body, bsz=bsz, t=t, cap=cap),
        grid=(ne, nch),
        in_specs=[pl.BlockSpec((bsz, None, 1, cap), lambda e, c: (0, e, 0, 0), memory_space=pltpu.SMEM),
                  pl.BlockSpec(memory_space=pl.ANY),
                  pl.BlockSpec((None, d, MOE_HC), lambda e, c: (e, 0, c)),
                  pl.BlockSpec((None, d, MOE_HC), lambda e, c: (e, 0, c)),
                  pl.BlockSpec((None, MOE_HC, d), lambda e, c: (e, c, 0))],
        out_specs=pl.BlockSpec((None, bsz * cap, d), lambda e, c: (e, 0, 0)),
        out_shape=jax.ShapeDtypeStruct((ne, bsz * cap, d), F32),
        scratch_shapes=[pltpu.VMEM((cap * SUBLANES, LANES), F32),
                        pltpu.VMEM((bsz * cap, d), BF16),
                        pltpu.VMEM((d, MOE_HC), BF16),
                        pltpu.VMEM((d, MOE_HC), BF16),
                        pltpu.VMEM((MOE_HC, d), BF16),
                        pltpu.SemaphoreType.DMA],
        compiler_params=_params(("arbitrary", "arbitrary"), VMEM_LIMIT),
        name="moe",
    )(tok, h2_tiles, w_gate, w_up, w_down)


def _combine_body(tok_ref, aff_ref, y_ref, acc_hbm, acc, ysc, sem, *, t, cap, ne):
    e = pl.program_id(1)
    b = pl.program_id(0)

    @pl.when(e == 0)
    def _():
        acc[...] = jnp.zeros_like(acc)

    for s in range(SUBLANES):
        ysc[pl.ds(s, cap, stride=SUBLANES), :] = y_ref[:, s * LANES:(s + 1) * LANES]

    def tile_of(row):
        return pl.ds(pl.multiple_of(row * SUBLANES, SUBLANES), SUBLANES)

    def rows(g, carry):
        tokens = [tok_ref[0, g * ROW_UNROLL + k] for k in range(ROW_UNROLL)]
        new = [acc[tile_of(tk), :] + aff_ref[0, tk] * ysc[tile_of(g * ROW_UNROLL + k), :]
               for k, tk in enumerate(tokens)]
        for tk, val in zip(tokens, new):
            acc[tile_of(tk), :] = val
        return carry

    lax.fori_loop(0, cap // ROW_UNROLL, rows, 0)

    @pl.when(e == ne - 1)
    def _():
        out = acc_hbm.at[pl.ds(pl.multiple_of(b * t * SUBLANES, SUBLANES), t * SUBLANES), :]
        copy = pltpu.make_async_copy(acc, out, sem)
        copy.start()
        copy.wait()


def _combine(tok, aff, y, bsz, t, cap):
    ne = aff.shape[1]
    d = y.shape[-1]
    smem = lambda n: pl.BlockSpec((None, None, 1, n), lambda b, e: (b, e, 0, 0), memory_space=pltpu.SMEM)
    return pl.pallas_call(
        functools.partial(_combine_body, t=t, cap=cap, ne=ne),
        grid=(bsz, ne),
        in_specs=[smem(cap), smem(t),
                  pl.BlockSpec((None, cap, d), lambda b, e: (e, b, 0))],
        out_specs=pl.BlockSpec(memory_space=pl.ANY),
        out_shape=jax.ShapeDtypeStruct((bsz * t * SUBLANES, LANES), F32),
        scratch_shapes=[pltpu.VMEM((t * SUBLANES, LANES), F32),
                        pltpu.VMEM((cap * SUBLANES, LANES), F32),
                        pltpu.SemaphoreType.DMA],
        compiler_params=_params(("arbitrary", "arbitrary"), VMEM_LIMIT),
        name="combine",
    )(tok, aff, y)


def _final_body(x1_ref, acc_ref, g2_ref, gf_ref, o_ref, *, tm):
    moe = jnp.concatenate([acc_ref[pl.ds(s, tm, stride=SUBLANES), :] for s in range(SUBLANES)], axis=-1)
    x = x1_ref[...] + g2_ref[...] * moe
    ms = jnp.mean(x * x, axis=-1, keepdims=True)
    o_ref[...] = x * lax.rsqrt(ms + EPS) * gf_ref[...]


def _final(x1, acc_tiles, mods4, g_final, tm):
    bsz, t, d = x1.shape
    nt = t // tm
    return pl.pallas_call(
        functools.partial(_final_body, tm=tm),
        grid=(bsz, nt),
        in_specs=[pl.BlockSpec((None, tm, d), lambda b, i: (b, i, 0)),
                  pl.BlockSpec((tm * SUBLANES, LANES), lambda b, i: (b * nt + i, 0)),
                  _mod_spec(5, lambda b: b)(d),
                  pl.BlockSpec((1, d), lambda b, i: (0, 0))],
        out_specs=pl.BlockSpec((None, tm, d), lambda b, i: (b, i, 0)),
        out_shape=jax.ShapeDtypeStruct((bsz, t, d), F32),
        compiler_params=_params(("arbitrary", "arbitrary")),
        name="final",
    )(x1, acc_tiles, mods4, g_final)


def _layer(x, ctx, mods4, ctx_row, norm_mix_g, w_in, conv_w, conv_b, conv_ln_g, conv_ln_b, rpb, w_out,
           norm_ffn_g, w_router, w_gate, w_up, w_down):
    bsz, t, d = x.shape
    cw = conv_w.shape[1]
    n_heads = rpb.shape[0]
    aw = (w_in.shape[1] - 2 * cw) // 3
    head_dim = aw // n_heads
    rows = t // GRID_W
    kh = min((rpb.shape[1] + 1) // 2, rows)
    ne = w_router.shape[1]
    cap = EC_FACTOR * t // ne
    row = lambda a: a.reshape(1, -1)

    w_in_bf = w_in.astype(BF16)
    u, q, k, v = _inproj(x, row(norm_mix_g), mods4, w_in_bf, cw, aw, head_dim ** -0.5, tm=512)
    kx, vx = _inproj_ctx(ctx, row(norm_mix_g), mods4, w_in_bf[:, 2 * cw + aw:], aw, ctx_row)
    y_conv = _conv(u, conv_w, conv_b, conv_ln_g, conv_ln_b, tt=256)
    assert rows >= (rpb.shape[1] + 1) // 2
    y_attn = _attn(q, k, v, kx, vx, _bias_table(rpb))
    x1, h2_tiles, aff = _outproj(y_conv, y_attn, x, w_out.astype(BF16), mods4, row(norm_ffn_g),
                                 w_router.T, tm=512)
    tok = _route(aff, cap).reshape(bsz, ne, 1, cap)
    y = _moe(tok, h2_tiles, w_gate, w_up, w_down, bsz, t, cap)
    acc_tiles = _combine(tok, aff.reshape(bsz, ne, 1, t), y, bsz, t, cap)
    return x1, acc_tiles


def kernel(x, c, ctx, c_ctx, w_mod, b_mod, norm_mix_g, w_in, conv_w, conv_b, conv_ln_g, conv_ln_b, rpb,
           w_out, norm_ffn_g, w_router, w_gate, w_up, w_down, norm_final_g):
    bsz, t, d = x.shape
    depth = w_mod.shape[0]
    assert depth == 1
    assert bsz + 1 <= SUBLANES
    c8 = jnp.zeros((SUBLANES, d), F32).at[:bsz].set(c).at[bsz].set(c_ctx)
    mods4 = _adaln(c8, w_mod[0], b_mod[0].reshape(1, -1)).reshape(SUBLANES, N_MOD, 1, d)
    x1, acc_tiles = _layer(x, ctx, mods4, bsz, norm_mix_g[0], w_in[0], conv_w[0], conv_b[0], conv_ln_g[0],
                           conv_ln_b[0], rpb[0], w_out[0], norm_ffn_g[0], w_router[0], w_gate[0], w_up[0],
                           w_down[0])
    return _final(x1, acc_tiles, mods4, norm_final_g.reshape(1, -1), tm=512)
```

```python
import functools

import jax
import jax.numpy as jnp
from jax import lax
from jax.experimental import pallas as pl
from jax.experimental.pallas import tpu as pltpu

F32 = jnp.float32
BF16 = jnp.bfloat16
I32 = jnp.int32
HIGHEST = lax.Precision.HIGHEST

GRID_W = 64
NA_COLS = 16
EC_FACTOR = 2
N_MOD = 6
EPS = 1e-6
NEG_INF = -1e30
LANES = 128
SUBLANES = 8
VMEM_LIMIT = 56 * 1024 * 1024

_NT = (((1,), (1,)), ((), ()))


def _params(sem, vmem=None):
    return pltpu.CompilerParams(dimension_semantics=sem, vmem_limit_bytes=vmem)


def _adaln_body(c_ref, w_ref, b_ref, o_ref):
    c = c_ref[...]
    s = c * jax.nn.sigmoid(c)
    o_ref[...] = jnp.dot(s, w_ref[...], preferred_element_type=F32, precision=HIGHEST) + b_ref[...]


def _adaln(c8, w_mod, b_mod):
    d, n = w_mod.shape
    bn = 512
    return pl.pallas_call(
        _adaln_body,
        grid=(n // bn,),
        in_specs=[pl.BlockSpec((SUBLANES, d), lambda j: (0, 0)),
                  pl.BlockSpec((d, bn), lambda j: (0, j)),
                  pl.BlockSpec((1, bn), lambda j: (0, j))],
        out_specs=pl.BlockSpec((SUBLANES, bn), lambda j: (0, j)),
        out_shape=jax.ShapeDtypeStruct((SUBLANES, n), F32),
        compiler_params=_params(("arbitrary",)),
        name="adaln",
    )(c8, w_mod, b_mod)


def _norm_mod(x, g, shift, scale):
    ms = jnp.mean(x * x, axis=-1, keepdims=True)
    h = x * lax.rsqrt(ms + EPS) * g
    return h * (1.0 + scale) + shift


def _inproj_body(x_ref, g_ref, sh_ref, sc_ref, w_ref, u_ref, q_ref, k_ref, v_ref, *, cw, aw, qscale):
    hb = _norm_mod(x_ref[...], g_ref[...], sh_ref[...], sc_ref[...]).astype(BF16)

    def proj(lo, width):
        return jnp.dot(hb, w_ref[:, lo:lo + width], preferred_element_type=F32)

    a = proj(0, cw)
    gate = proj(cw, cw)
    u_ref[...] = a * jax.nn.sigmoid(gate)
    q_ref[...] = (proj(2 * cw, aw) * qscale).astype(BF16)
    k_ref[...] = proj(2 * cw + aw, aw).astype(BF16)
    v_ref[...] = proj(2 * cw + 2 * aw, aw).astype(BF16)


def _inproj_kv_body(x_ref, g_ref, sh_ref, sc_ref, w_ref, k_ref, v_ref, *, aw):
    hb = _norm_mod(x_ref[...], g_ref[...], sh_ref[...], sc_ref[...]).astype(BF16)
    k_ref[...] = jnp.dot(hb, w_ref[:, 0:aw], preferred_element_type=F32).astype(BF16)
    v_ref[...] = jnp.dot(hb, w_ref[:, aw:2 * aw], preferred_element_type=F32).astype(BF16)


def _mod_spec(slot, row_of):
    return lambda d: pl.BlockSpec((None, None, 1, d), lambda b, i: (row_of(b), slot, 0, 0))


def _inproj(x, g, mods4, w_bf, cw, aw, qscale, tm):
    bsz, t, d = x.shape
    tok = lambda w, dt: (pl.BlockSpec((None, tm, w), lambda b, i: (b, i, 0)), jax.ShapeDtypeStruct((bsz, t, w), dt))
    outs = [tok(cw, F32), tok(aw, BF16), tok(aw, BF16), tok(aw, BF16)]
    return pl.pallas_call(
        functools.partial(_inproj_body, cw=cw, aw=aw, qscale=qscale),
        grid=(bsz, t // tm),
        in_specs=[pl.BlockSpec((None, tm, d), lambda b, i: (b, i, 0)),
                  pl.BlockSpec((1, d), lambda b, i: (0, 0)),
                  _mod_spec(0, lambda b: b)(d),
                  _mod_spec(1, lambda b: b)(d),
                  pl.BlockSpec(w_bf.shape, lambda b, i: (0, 0))],
        out_specs=[o[0] for o in outs],
        out_shape=[o[1] for o in outs],
        compiler_params=_params(("arbitrary", "arbitrary"), VMEM_LIMIT),
        name="inproj",
    )(x, g, mods4, mods4, w_bf)


def _inproj_ctx(ctx, g, mods4, w_kv_bf, aw, ctx_row):
    bsz, l, d = ctx.shape
    spec = pl.BlockSpec((None, l, aw), lambda b, i: (b, 0, 0))
    shape = jax.ShapeDtypeStruct((bsz, l, aw), BF16)
    return pl.pallas_call(
        functools.partial(_inproj_kv_body, aw=aw),
        grid=(bsz, 1),
        in_specs=[pl.BlockSpec((None, l, d), lambda b, i: (b, 0, 0)),
                  pl.BlockSpec((1, d), lambda b, i: (0, 0)),
                  _mod_spec(0, lambda b: ctx_row)(d),
                  _mod_spec(1, lambda b: ctx_row)(d),
                  pl.BlockSpec(w_kv_bf.shape, lambda b, i: (0, 0))],
        out_specs=[spec, spec],
        out_shape=[shape, shape],
        compiler_params=_params(("arbitrary", "arbitrary"), VMEM_LIMIT),
        name="inproj_ctx",
    )(ctx, g, mods4, mods4, w_kv_bf)


CONV_HALO = 16
CONV_RC = 64
CONV_CW = 256


def _conv_body(prev_ref, cur_ref, next_ref, w_ref, b_ref, lg_ref, lb_ref, o_ref, ext, ysc, *, width, tt):
    i = pl.program_id(1)
    n = pl.num_programs(1)
    ext[0:CONV_HALO, :] = jnp.where(i > 0, prev_ref[...], 0.0)
    ext[CONV_HALO:CONV_HALO + tt, :] = cur_ref[...]
    ext[CONV_HALO + tt:, :] = jnp.where(i < n - 1, next_ref[...], 0.0)
    lead = CONV_HALO - width // 2
    for rc in range(tt // CONV_RC):
        r0 = rc * CONV_RC
        for ch in range(cur_ref.shape[-1] // CONV_CW):
            cs = slice(ch * CONV_CW, (ch + 1) * CONV_CW)
            out = None
            for m in range(SUBLANES):
                nrow = CONV_RC + (SUBLANES if m else 0)
                part = None
                for j in range(width):
                    if (lead + j) % SUBLANES != m:
                        continue
                    a = r0 + lead + j - m
                    term = w_ref[j:j + 1, cs] * ext[a:a + nrow, cs]
                    part = term if part is None else part + term
                piece = part[m:m + CONV_RC]
                out = piece if out is None else out + piece
            ysc[r0:r0 + CONV_RC, cs] = out
    y = ysc[...] + b_ref[...]
    mu = jnp.mean(y, axis=-1, keepdims=True)
    yc = y - mu
    var = jnp.mean(yc * yc, axis=-1, keepdims=True)
    yn = yc * lax.rsqrt(var + EPS) * lg_ref[...] + lb_ref[...]
    o_ref[...] = (yn * jax.nn.sigmoid(yn)).astype(BF16)


def _conv(u, conv_w, conv_b, ln_g, ln_b, tt):
    bsz, t, cw = u.shape
    width = conv_w.shape[0]
    assert width // 2 < CONV_HALO and tt % CONV_HALO == 0 and tt % CONV_RC == 0 and cw % CONV_CW == 0
    hb = tt // CONV_HALO
    nh = t // CONV_HALO
    row = lambda a: a.reshape(1, cw)
    return pl.pallas_call(
        functools.partial(_conv_body, width=width, tt=tt),
        grid=(bsz, t // tt),
        in_specs=[pl.BlockSpec((None, CONV_HALO, cw), lambda b, i: (b, jnp.maximum(i * hb - 1, 0), 0)),
                  pl.BlockSpec((None, tt, cw), lambda b, i: (b, i, 0)),
                  pl.BlockSpec((None, CONV_HALO, cw), lambda b, i: (b, jnp.minimum((i + 1) * hb, nh - 1), 0)),
                  pl.BlockSpec((width, cw), lambda b, i: (0, 0)),
                  pl.BlockSpec((1, cw), lambda b, i: (0, 0)),
                  pl.BlockSpec((1, cw), lambda b, i: (0, 0)),
                  pl.BlockSpec((1, cw), lambda b, i: (0, 0))],
        out_specs=pl.BlockSpec((None, tt, cw), lambda b, i: (b, i, 0)),
        out_shape=jax.ShapeDtypeStruct((bsz, t, cw), BF16),
        scratch_shapes=[pltpu.VMEM((tt + 2 * CONV_HALO, cw), F32), pltpu.VMEM((tt, cw), F32)],
        compiler_params=_params(("arbitrary", "arbitrary")),
        name="conv",
    )(u, u, u, conv_w, row(conv_b), row(ln_g), row(ln_b))


ATT_RB = 8


def _bias_body(rpb_ref, onehot_ref, valid_ref, o_ref):
    t = jnp.dot(rpb_ref[...], onehot_ref[...], preferred_element_type=F32, precision=HIGHEST)
    o_ref[...] = jnp.where(valid_ref[...] > 0.5, t, NEG_INF)


def _bias_table(rpb):
    nh, ndr, ndc = rpb.shape
    qc = jnp.arange(GRID_W)[:, None]
    kc = jnp.arange(GRID_W)[None, :]
    c0 = jnp.clip(qc - NA_COLS // 2, 0, GRID_W - NA_COLS)
    valid = ((kc >= c0) & (kc < c0 + NA_COLS)).astype(F32).reshape(1, GRID_W * GRID_W)
    d_col = jnp.clip(kc - qc + (NA_COLS - 1), 0, 2 * NA_COLS - 2).reshape(1, GRID_W * GRID_W)
    ndc_pad = -(-ndc // SUBLANES) * SUBLANES
    onehot = (jnp.arange(ndc_pad)[:, None] == d_col).astype(F32)
    rpb2 = jnp.pad(rpb.reshape(nh * ndr, ndc), ((0, 0), (0, ndc_pad - ndc)))
    toeplitz = pl.pallas_call(
        _bias_body,
        out_shape=jax.ShapeDtypeStruct((nh * ndr, GRID_W * GRID_W), F32),
        name="bias",
    )(rpb2, onehot, valid).reshape(nh, ndr, GRID_W, GRID_W)
    return jnp.concatenate([toeplitz[:, :-1], toeplitz[:, 1:]], axis=-1)


def _attn_body(q_ref, kp_ref, kc_ref, kn_ref, vp_ref, vc_ref, vn_ref, kx_ref, vx_ref, bias_ref, o_ref,
               kwin, vwin, s_sc, p_sc, m_sc, l_sc, *, rows, kh, n_pairs):
    i = pl.program_id(1)
    blk = ATT_RB * GRID_W
    for w, (a, b) in enumerate(((kp_ref, vp_ref), (kc_ref, vc_ref), (kn_ref, vn_ref))):
        kwin[w * blk:(w + 1) * blk, :] = a[...]
        vwin[w * blk:(w + 1) * blk, :] = b[...]
    lo = lax.broadcasted_iota(I32, (GRID_W, LANES), 1) < LANES // 2
    band = kh * GRID_W

    def row_body(rr, carry):
        r = i * ATT_RB + rr
        r0 = jnp.clip(r - kh // 2, 0, rows - kh)
        delta = r - r0
        off = pl.multiple_of((r0 - (i - 1) * ATT_RB) * GRID_W, GRID_W)
        qoff = pl.multiple_of(rr * GRID_W, GRID_W)
        def lane_groups(x, op):
            parts = [x[:, g * LANES:(g + 1) * LANES] for g in range(x.shape[-1] // LANES)]
            while len(parts) > 1:
                parts = [op(parts[a], parts[a + 1]) for a in range(0, len(parts) - 1, 2)] + parts[len(parts) & ~1:]
            return parts[0]

        for p in range(n_pairs):
            cols = slice(p * LANES, (p + 1) * LANES)
            qp = q_ref[pl.ds(qoff, GRID_W), cols]
            kb = kwin[pl.ds(off, band), cols]
            kx = kx_ref[:, cols]
            for s in range(2):
                h = 2 * p + s
                qz = jnp.where(lo if s == 0 else jnp.logical_not(lo), qp, jnp.zeros_like(qp))
                bias = jnp.concatenate(
                    [bias_ref[h, 2 * g + (kh - 1) - delta] for g in range(kh // 2)], axis=-1)
                s_loc = lax.dot_general(qz, kb, _NT, preferred_element_type=F32) + bias
                s_ctx = lax.dot_general(qz, kx, _NT, preferred_element_type=F32)
                s_sc[h, :, 0:band] = s_loc
                s_sc[h, :, band:] = s_ctx
                m_sc[h] = jnp.maximum(lane_groups(s_loc, jnp.maximum), lane_groups(s_ctx, jnp.maximum))
        for h in range(2 * n_pairs):
            m = jnp.max(m_sc[h], axis=-1, keepdims=True)
            e = jnp.exp(s_sc[h] - m)
            p_sc[h] = e.astype(BF16)
            l_sc[h] = lane_groups(e, jnp.add)
        for p in range(n_pairs):
            cols = slice(p * LANES, (p + 1) * LANES)
            vb = vwin[pl.ds(off, band), cols]
            vx = vx_ref[:, cols]
            outs = []
            for s in range(2):
                h = 2 * p + s
                o = (jnp.dot(p_sc[h, :, 0:band], vb, preferred_element_type=F32)
                     + jnp.dot(p_sc[h, :, band:], vx, preferred_element_type=F32))
                outs.append(o / jnp.sum(l_sc[h], axis=-1, keepdims=True))
            o_ref[pl.ds(qoff, GRID_W), cols] = jnp.where(lo, outs[0], outs[1]).astype(BF16)
        return carry

    lax.fori_loop(0, ATT_RB, row_body, 0, unroll=2)


def _attn(q, k, v, kx, vx, bias):
    bsz, t, aw = q.shape
    l = kx.shape[1]
    rows = t // GRID_W
    kh = (bias.shape[1] + 2) // 2
    nb = rows // ATT_RB
    blk = ATT_RB * GRID_W
    assert kh == ATT_RB and rows % ATT_RB == 0 and aw % LANES == 0 and 2 * GRID_W == LANES
    cur = pl.BlockSpec((None, blk, aw), lambda b, i: (b, i, 0))
    prev = pl.BlockSpec((None, blk, aw), lambda b, i: (b, jnp.maximum(i - 1, 0), 0))
    nxt = pl.BlockSpec((None, blk, aw), lambda b, i: (b, jnp.minimum(i + 1, nb - 1), 0))
    cx = pl.BlockSpec((None, l, aw), lambda b, i: (b, 0, 0))
    nh = bias.shape[0]
    nkeys = kh * GRID_W + l
    assert l % LANES == 0 and nh == 2 * (aw // LANES)
    return pl.pallas_call(
        functools.partial(_attn_body, rows=rows, kh=kh, n_pairs=aw // LANES),
        grid=(bsz, nb),
        in_specs=[cur, prev, cur, nxt, prev, cur, nxt, cx, cx,
                  pl.BlockSpec(bias.shape, lambda b, i: (0, 0, 0, 0))],
        out_specs=cur,
        out_shape=jax.ShapeDtypeStruct((bsz, t, aw), BF16),
        scratch_shapes=[pltpu.VMEM((3 * blk, aw), BF16), pltpu.VMEM((3 * blk, aw), BF16),
                        pltpu.VMEM((nh, GRID_W, nkeys), F32), pltpu.VMEM((nh, GRID_W, nkeys), BF16),
                        pltpu.VMEM((nh, GRID_W, LANES), F32), pltpu.VMEM((nh, GRID_W, LANES), F32)],
        compiler_params=_params(("arbitrary", "arbitrary"), VMEM_LIMIT),
        name="attn",
    )(q, k, k, k, v, v, v, kx, vx, bias)


def _outproj_body(yc_ref, ya_ref, x_ref, wo_ref, g1_ref, gf_ref, sh_ref, sc_ref, wr_ref,
                  x1_ref, h2_ref, aff_ref, *, cw, tm):
    acc = (jnp.dot(yc_ref[...], wo_ref[0:cw, :], preferred_element_type=F32)
           + jnp.dot(ya_ref[...], wo_ref[cw:, :], preferred_element_type=F32))
    x1 = x_ref[...] + g1_ref[...] * acc
    x1_ref[...] = x1
    h2 = _norm_mod(x1, gf_ref[...], sh_ref[...], sc_ref[...])
    for s in range(h2.shape[-1] // LANES):
        h2_ref[pl.ds(s, tm, stride=SUBLANES), :] = h2[:, s * LANES:(s + 1) * LANES]
    logits = lax.dot_general(wr_ref[...], h2, _NT, preferred_element_type=F32, precision=HIGHEST)
    e = jnp.exp(logits - jnp.max(logits, axis=0, keepdims=True))
    aff_ref[...] = e / jnp.sum(e, axis=0, keepdims=True)


def _outproj(y_conv, y_attn, x, wo_bf, mods4, g_ffn, wr_t, tm):
    bsz, t, d = x.shape
    cw = y_conv.shape[-1]
    ne = wr_t.shape[0]
    assert d == SUBLANES * LANES
    nt = t // tm
    return pl.pallas_call(
        functools.partial(_outproj_body, cw=cw, tm=tm),
        grid=(bsz, nt),
        in_specs=[pl.BlockSpec((None, tm, cw), lambda b, i: (b, i, 0)),
                  pl.BlockSpec((None, tm, y_attn.shape[-1]), lambda b, i: (b, i, 0)),
                  pl.BlockSpec((None, tm, d), lambda b, i: (b, i, 0)),
                  pl.BlockSpec(wo_bf.shape, lambda b, i: (0, 0)),
                  _mod_spec(2, lambda b: b)(d),
                  pl.BlockSpec((1, d), lambda b, i: (0, 0)),
                  _mod_spec(3, lambda b: b)(d),
                  _mod_spec(4, lambda b: b)(d),
                  pl.BlockSpec(wr_t.shape, lambda b, i: (0, 0))],
        out_specs=[pl.BlockSpec((None, tm, d), lambda b, i: (b, i, 0)),
                   pl.BlockSpec((tm * SUBLANES, LANES), lambda b, i: (b * nt + i, 0)),
                   pl.BlockSpec((None, ne, tm), lambda b, i: (b, 0, i))],
        out_shape=[jax.ShapeDtypeStruct((bsz, t, d), F32),
                   jax.ShapeDtypeStruct((bsz * t * SUBLANES, LANES), F32),
                   jax.ShapeDtypeStruct((bsz, ne, t), F32)],
        compiler_params=_params(("arbitrary", "arbitrary"), VMEM_LIMIT),
        name="outproj",
    )(y_conv, y_attn, x, wo_bf, mods4, g_ffn, mods4, mods4, wr_t)


def _route_body(affe_ref, affr_ref, u_ref, ones_ref, l_ref, tok_ref, *, cap, ne, nblk):
    aff_e = affe_ref[...]

    def search(it, bits):
        cand = bits | jnp.left_shift(jnp.int32(1), 30 - it)
        cnt = jnp.sum(jnp.where(aff_e >= pltpu.bitcast(cand, F32), 1.0, 0.0), axis=1, keepdims=True)
        return jnp.where(cnt >= cap, cand, bits)

    tau = pltpu.bitcast(lax.fori_loop(0, 31, search, jnp.zeros((ne, 1), I32)), F32)
    n_gt = jnp.sum(jnp.where(aff_e > tau, 1.0, 0.0), axis=1, keepdims=True)
    need = cap - n_gt

    def per_row(col):
        return jnp.concatenate([jnp.broadcast_to(col[e:e + 1, :], (nblk, 1)) for e in range(ne)], axis=0)

    tau_r = per_row(tau)
    need_r = per_row(need)
    kr = affr_ref[...]

    def scan(mf):
        mb = mf.astype(BF16)
        incl = jnp.dot(mb, u_ref[...], preferred_element_type=F32)
        tot = jnp.dot(mb, ones_ref[...], preferred_element_type=F32)
        off = jnp.dot(l_ref[...], tot.astype(BF16), preferred_element_type=F32)
        return incl, tot, off

    eq = jnp.where(kr == tau_r, 1.0, 0.0)
    incl_e, _, off_e = scan(eq)
    rank_eq = off_e + incl_e - eq
    sel = jnp.where((kr > tau_r) | ((eq > 0.5) & (rank_eq < need_r)), 1.0, 0.0)
    incl, _, off = scan(sel)

    nbits = LANES.bit_length() - 1
    lane = lax.broadcasted_iota(I32, kr.shape, 1)
    row = lax.broadcasted_iota(I32, kr.shape, 0)
    dist = lane - (incl.astype(I32) - 1)
    word = jnp.where(sel > 0.5, (1 << (2 * nbits)) | (dist << nbits) | lane, 0)

    def bit(x, pos):
        return (x >> pos) & 1

    for k in range(nbits):
        arriving = pltpu.roll(word, LANES - (1 << k), axis=1)
        take = bit(arriving, 2 * nbits) & bit(arriving, nbits + k)
        stay = bit(word, 2 * nbits) & (1 - bit(word, nbits + k))
        word = jnp.where(take == 1, arriving, jnp.where(stay == 1, word, 0))

    off_i = off.astype(I32)
    shift = off_i & (LANES - 1)
    for k in range(nbits):
        word = jnp.where(bit(shift, k) == 1, pltpu.roll(word, 1 << k, axis=1), word)
    valid = bit(word, 2 * nbits) == 1
    wrapped = lane < shift
    blk_bits = nblk.bit_length() - 1
    q0 = (row >> blk_bits) * (cap // LANES) + (off_i >> nbits)
    digits = (jnp.where(valid, row & (nblk - 1), 0), jnp.where(valid, word & (LANES - 1), 0))

    def place(sel_rows, q):
        onehot_t = jnp.where(q == lane, 1.0, 0.0).astype(BF16)
        outs = []
        for dgt in digits:
            x = jnp.where(sel_rows, dgt, 0).astype(F32).astype(BF16)
            outs.append(lax.dot_general(onehot_t, x, (((0,), (0,)), ((), ())), preferred_element_type=F32))
        return outs[0] * LANES + outs[1]

    tok_ref[...] = (place(jnp.logical_not(wrapped), q0) + place(wrapped, q0 + 1)).astype(I32)


def _route(aff, cap):
    bsz, ne, t = aff.shape
    nblk = t // LANES
    nr = ne * nblk
    nq = ne * cap // LANES
    assert nq == LANES and cap % LANES == 0 and nblk & (nblk - 1) == 0 and nblk <= 256
    aff_r = aff.reshape(bsz, nr, LANES)
    ar = jnp.arange(LANES)
    upper = (ar[:, None] <= ar[None, :]).astype(BF16)
    ones = jnp.ones((LANES, LANES), BF16)
    rr = jnp.arange(nr)
    lower = ((rr[:, None] // nblk == rr[None, :] // nblk) & (rr[None, :] < rr[:, None])).astype(BF16)
    const = lambda a: pl.BlockSpec(a.shape, lambda b: (0,) * a.ndim)
    return pl.pallas_call(
        functools.partial(_route_body, cap=cap, ne=ne, nblk=nblk),
        grid=(bsz,),
        in_specs=[pl.BlockSpec((None, ne, t), lambda b: (b, 0, 0)),
                  pl.BlockSpec((None, nr, LANES), lambda b: (b, 0, 0)),
                  const(upper), const(ones), const(lower)],
        out_specs=pl.BlockSpec((None, nq, LANES), lambda b: (b, 0, 0)),
        out_shape=jax.ShapeDtypeStruct((bsz, nq, LANES), I32),
        compiler_params=_params(("arbitrary",), VMEM_LIMIT),
        name="route",
    )(aff, aff_r, upper, ones, lower)


MOE_HC = 512
MOE_TM = 512
ROW_UNROLL = 8


def _gather_copy(h2_hbm, xs3, sem, token, slot):
    src = h2_hbm.at[pl.ds(pl.multiple_of(token * SUBLANES, SUBLANES), SUBLANES), :]
    dst = xs3.at[pl.ds(pl.multiple_of(slot * SUBLANES, SUBLANES), SUBLANES), :]
    return pltpu.make_async_copy(src, dst, sem)


def _moe_body(tok_ref, nxt_ref, h2_hbm, wg_ref, wu_ref, wd_ref, y_ref, xs3, xsb, wgb, wub, wdb, sem,
              *, bsz, t, cap, ne, nch):
    e = pl.program_id(0)
    c = pl.program_id(1)
    n_rows = bsz * cap
    n_tiles = n_rows // MOE_TM
    per_tile = n_rows // (nch * n_tiles)

    def fetch(ref, b, slot):
        _gather_copy(h2_hbm, xs3, sem, b * t + ref[b, 0, slot], b * cap + slot).start()

    def wait_staging():
        pltpu.make_async_copy(h2_hbm.at[pl.ds(0, n_rows * SUBLANES), :], xs3, sem).wait()

    @pl.when((e == 0) & (c == 0))
    def _first_expert():
        for b in range(bsz):
            def rows(g, carry):
                for k in range(ROW_UNROLL):
                    fetch(tok_ref, b, g * ROW_UNROLL + k)
                return carry

            lax.fori_loop(0, cap // ROW_UNROLL, rows, 0)

    @pl.when(c == 0)
    def _stage():
        wait_staging()
        for s in range(SUBLANES):
            xsb[:, s * LANES:(s + 1) * LANES] = xs3[pl.ds(s, n_rows, stride=SUBLANES), :].astype(BF16)

    wgb[...] = wg_ref[...].astype(BF16)
    wub[...] = wu_ref[...].astype(BF16)
    wdb[...] = wd_ref[...].astype(BF16)

    def tile(m, carry):
        rows = pl.ds(pl.multiple_of(m * MOE_TM, MOE_TM), MOE_TM)
        xm = xsb[rows, :]
        gate = jnp.dot(xm, wgb[...], preferred_element_type=F32)
        first = (c * n_tiles + m) * per_tile
        b = first // cap
        for k in range(per_tile):
            fetch(nxt_ref, b, first - b * cap + k)
        up = jnp.dot(xm, wub[...], preferred_element_type=F32)
        hid = (gate * jax.nn.sigmoid(gate) * up).astype(BF16)
        contrib = jnp.dot(hid, wdb[...], preferred_element_type=F32)

        @pl.when(c == 0)
        def _():
            y_ref[rows, :] = contrib

        @pl.when(c > 0)
        def _():
            y_ref[rows, :] += contrib

        return carry

    lax.fori_loop(0, n_tiles, tile, 0)

    @pl.when((e == ne - 1) & (c == nch - 1))
    def _drain():
        wait_staging()


def _moe(tok, h2_tiles, w_gate, w_up, w_down, bsz, t, cap):
    ne, d, dh = w_gate.shape
    nch = dh // MOE_HC
    n_rows = bsz * cap
    per_tile = n_rows // (nch * (n_rows // MOE_TM))
    assert cap % ROW_UNROLL == 0 and n_rows % MOE_TM == 0 and cap % per_tile == 0
    tok_spec = lambda ahead: pl.BlockSpec((bsz, None, 1, cap),
                                          lambda e, c: (0, jnp.minimum(e + ahead, ne - 1), 0, 0),
                                          memory_space=pltpu.SMEM)
    return pl.pallas_call(
        functools.partial(_moe_body, bsz=bsz, t=t, cap=cap, ne=ne, nch=nch),
        grid=(ne, nch),
        in_specs=[tok_spec(0), tok_spec(1),
                  pl.BlockSpec(memory_space=pl.ANY),
                  pl.BlockSpec((None, d, MOE_HC), lambda e, c: (e, 0, c)),
                  pl.BlockSpec((None, d, MOE_HC), lambda e, c: (e, 0, c)),
                  pl.BlockSpec((None, MOE_HC, d), lambda e, c: (e, c, 0))],
        out_specs=pl.BlockSpec((None, n_rows, d), lambda e, c: (e, 0, 0)),
        out_shape=jax.ShapeDtypeStruct((ne, n_rows, d), F32),
        scratch_shapes=[pltpu.VMEM((n_rows * SUBLANES, LANES), F32),
                        pltpu.VMEM((n_rows, d), BF16),
                        pltpu.VMEM((d, MOE_HC), BF16),
                        pltpu.VMEM((d, MOE_HC), BF16),
                        pltpu.VMEM((MOE_HC, d), BF16),
                        pltpu.SemaphoreType.DMA],
        compiler_params=_params(("arbitrary", "arbitrary"), VMEM_LIMIT),
        name="moe",
    )(tok, tok, h2_tiles, w_gate, w_up, w_down)


def _combine_body(tok_ref, aff_ref, y_ref, acc_hbm, acc, ysc, sem, *, t, cap, ne):
    e = pl.program_id(1)
    b = pl.program_id(0)

    @pl.when(e == 0)
    def _():
        acc[...] = jnp.zeros_like(acc)

    for s in range(SUBLANES):
        ysc[pl.ds(s, cap, stride=SUBLANES), :] = y_ref[:, s * LANES:(s + 1) * LANES]

    def tile_of(row):
        return pl.ds(pl.multiple_of(row * SUBLANES, SUBLANES), SUBLANES)

    def rows(g, carry):
        tokens = [tok_ref[0, g * ROW_UNROLL + k] for k in range(ROW_UNROLL)]
        new = [acc[tile_of(tk), :] + aff_ref[0, tk] * ysc[tile_of(g * ROW_UNROLL + k), :]
               for k, tk in enumerate(tokens)]
        for tk, val in zip(tokens, new):
            acc[tile_of(tk), :] = val
        return carry

    lax.fori_loop(0, cap // ROW_UNROLL, rows, 0)

    @pl.when(e == ne - 1)
    def _():
        out = acc_hbm.at[pl.ds(pl.multiple_of(b * t * SUBLANES, SUBLANES), t * SUBLANES), :]
        copy = pltpu.make_async_copy(acc, out, sem)
        copy.start()
        copy.wait()


def _combine(tok, aff, y, bsz, t, cap):
    ne = aff.shape[1]
    d = y.shape[-1]
    smem = lambda n: pl.BlockSpec((None, None, 1, n), lambda b, e: (b, e, 0, 0), memory_space=pltpu.SMEM)
    return pl.pallas_call(
        functools.partial(_combine_body, t=t, cap=cap, ne=ne),
        grid=(bsz, ne),
        in_specs=[smem(cap), smem(t),
                  pl.BlockSpec((None, cap, d), lambda b, e: (e, b, 0))],
        out_specs=pl.BlockSpec(memory_space=pl.ANY),
        out_shape=jax.ShapeDtypeStruct((bsz * t * SUBLANES, LANES), F32),
        scratch_shapes=[pltpu.VMEM((t * SUBLANES, LANES), F32),
                        pltpu.VMEM((cap * SUBLANES, LANES), F32),
                        pltpu.SemaphoreType.DMA],
        compiler_params=_params(("arbitrary", "arbitrary"), VMEM_LIMIT),
        name="combine",
    )(tok, aff, y)


def _final_body(x1_ref, acc_ref, g2_ref, gf_ref, o_ref, *, tm):
    moe = jnp.concatenate([acc_ref[pl.ds(s, tm, stride=SUBLANES), :] for s in range(SUBLANES)], axis=-1)
    x = x1_ref[...] + g2_ref[...] * moe
    ms = jnp.mean(x * x, axis=-1, keepdims=True)
    o_ref[...] = x * lax.rsqrt(ms + EPS) * gf_ref[...]


def _final(x1, acc_tiles, mods4, g_final, tm):
    bsz, t, d = x1.shape
    nt = t // tm
    return pl.pallas_call(
        functools.partial(_final_body, tm=tm),
        grid=(bsz, nt),
        in_specs=[pl.BlockSpec((None, tm, d), lambda b, i: (b, i, 0)),
                  pl.BlockSpec((tm * SUBLANES, LANES), lambda b, i: (b * nt + i, 0)),
                  _mod_spec(5, lambda b: b)(d),
                  pl.BlockSpec((1, d), lambda b, i: (0, 0))],
        out_specs=pl.BlockSpec((None, tm, d), lambda b, i: (b, i, 0)),
        out_shape=jax.ShapeDtypeStruct((bsz, t, d), F32),
        compiler_params=_params(("arbitrary", "arbitrary")),
        name="final",
    )(x1, acc_tiles, mods4, g_final)


def _layer(x, ctx, mods4, ctx_row, norm_mix_g, w_in, conv_w, conv_b, conv_ln_g, conv_ln_b, rpb, w_out,
           norm_ffn_g, w_router, w_gate, w_up, w_down):
    bsz, t, d = x.shape
    cw = conv_w.shape[1]
    n_heads = rpb.shape[0]
    aw = (w_in.shape[1] - 2 * cw) // 3
    head_dim = aw // n_heads
    rows = t // GRID_W
    kh = min((rpb.shape[1] + 1) // 2, rows)
    ne = w_router.shape[1]
    cap = EC_FACTOR * t // ne
    row = lambda a: a.reshape(1, -1)

    w_in_bf = w_in.astype(BF16)
    u, q, k, v = _inproj(x, row(norm_mix_g), mods4, w_in_bf, cw, aw, head_dim ** -0.5, tm=512)
    kx, vx = _inproj_ctx(ctx, row(norm_mix_g), mods4, w_in_bf[:, 2 * cw + aw:], aw, ctx_row)
    y_conv = _conv(u, conv_w, conv_b, conv_ln_g, conv_ln_b, tt=256)
    assert rows >= (rpb.shape[1] + 1) // 2
    y_attn = _attn(q, k, v, kx, vx, _bias_table(rpb))
    x1, h2_tiles, aff = _outproj(y_conv, y_attn, x, w_out.astype(BF16), mods4, row(norm_ffn_g),
                                 w_router.T, tm=512)
    tok = _route(aff, cap).reshape(bsz, ne, 1, cap)
    y = _moe(tok, h2_tiles, w_gate, w_up, w_down, bsz, t, cap)
    acc_tiles = _combine(tok, aff.reshape(bsz, ne, 1, t), y, bsz, t, cap)
    return x1, acc_tiles


def kernel(x, c, ctx, c_ctx, w_mod, b_mod, norm_mix_g, w_in, conv_w, conv_b, conv_ln_g, conv_ln_b, rpb,
           w_out, norm_ffn_g, w_router, w_gate, w_up, w_down, norm_final_g):
    bsz, t, d = x.shape
    depth = w_mod.shape[0]
    assert depth == 1
    assert bsz + 1 <= SUBLANES
    c8 = jnp.zeros((SUBLANES, d), F32).at[:bsz].set(c).at[bsz].set(c_ctx)
    mods4 = _adaln(c8, w_mod[0], b_mod[0].reshape(1, -1)).reshape(SUBLANES, N_MOD, 1, d)
    x1, acc_tiles = _layer(x, ctx, mods4, bsz, norm_mix_g[0], w_in[0], conv_w[0], conv_b[0], conv_ln_g[0],
                           conv_ln_b[0], rpb[0], w_out[0], norm_ffn_g[0], w_router[0], w_gate[0], w_up[0],
                           w_down[0])
    return _final(x1, acc_tiles, mods4, norm_final_g.reshape(1, -1), tm=512)
```

```python
import functools

import jax
import jax.numpy as jnp
from jax import lax
from jax.experimental import pallas as pl
from jax.experimental.pallas import tpu as pltpu

F32 = jnp.float32
BF16 = jnp.bfloat16
I32 = jnp.int32
HIGHEST = lax.Precision.HIGHEST

GRID_W = 64
NA_COLS = 16
EC_FACTOR = 2
N_MOD = 6
EPS = 1e-6
NEG_INF = -1e30
LANES = 128
SUBLANES = 8
VMEM_LIMIT = 56 * 1024 * 1024

_NT = (((1,), (1,)), ((), ()))


def _params(sem, vmem=None):
    return pltpu.CompilerParams(dimension_semantics=sem, vmem_limit_bytes=vmem)


def _adaln_body(c_ref, w_ref, b_ref, o_ref):
    c = c_ref[...]
    s = c * jax.nn.sigmoid(c)
    o_ref[...] = jnp.dot(s, w_ref[...], preferred_element_type=F32, precision=HIGHEST) + b_ref[...]


def _adaln(c8, w_mod, b_mod):
    d, n = w_mod.shape
    bn = 512
    return pl.pallas_call(
        _adaln_body,
        grid=(n // bn,),
        in_specs=[pl.BlockSpec((SUBLANES, d), lambda j: (0, 0)),
                  pl.BlockSpec((d, bn), lambda j: (0, j)),
                  pl.BlockSpec((1, bn), lambda j: (0, j))],
        out_specs=pl.BlockSpec((SUBLANES, bn), lambda j: (0, j)),
        out_shape=jax.ShapeDtypeStruct((SUBLANES, n), F32),
        compiler_params=_params(("arbitrary",)),
        name="adaln",
    )(c8, w_mod, b_mod)


def _norm_mod(x, g, shift, scale):
    ms = jnp.mean(x * x, axis=-1, keepdims=True)
    h = x * lax.rsqrt(ms + EPS) * g
    return h * (1.0 + scale) + shift


def _inproj_body(x_ref, g_ref, sh_ref, sc_ref, w_ref, u_ref, q_ref, k_ref, v_ref, *, cw, aw, qscale):
    hb = _norm_mod(x_ref[...], g_ref[...], sh_ref[...], sc_ref[...]).astype(BF16)

    def proj(lo, width):
        return jnp.dot(hb, w_ref[:, lo:lo + width], preferred_element_type=F32)

    a = proj(0, cw)
    gate = proj(cw, cw)
    u_ref[...] = a * jax.nn.sigmoid(gate)
    q_ref[...] = (proj(2 * cw, aw) * qscale).astype(BF16)
    k_ref[...] = proj(2 * cw + aw, aw).astype(BF16)
    v_ref[...] = proj(2 * cw + 2 * aw, aw).astype(BF16)


def _inproj_kv_body(x_ref, g_ref, sh_ref, sc_ref, w_ref, k_ref, v_ref, *, aw):
    hb = _norm_mod(x_ref[...], g_ref[...], sh_ref[...], sc_ref[...]).astype(BF16)
    k_ref[...] = jnp.dot(hb, w_ref[:, 0:aw], preferred_element_type=F32).astype(BF16)
    v_ref[...] = jnp.dot(hb, w_ref[:, aw:2 * aw], preferred_element_type=F32).astype(BF16)


def _mod_spec(slot, row_of):
    return lambda d: pl.BlockSpec((None, None, 1, d), lambda b, i: (row_of(b), slot, 0, 0))


def _inproj(x, g, mods4, w_bf, cw, aw, qscale, tm):
    bsz, t, d = x.shape
    tok = lambda w, dt: (pl.BlockSpec((None, tm, w), lambda b, i: (b, i, 0)), jax.ShapeDtypeStruct((bsz, t, w), dt))
    outs = [tok(cw, F32), tok(aw, BF16), tok(aw, BF16), tok(aw, BF16)]
    return pl.pallas_call(
        functools.partial(_inproj_body, cw=cw, aw=aw, qscale=qscale),
        grid=(bsz, t // tm),
        in_specs=[pl.BlockSpec((None, tm, d), lambda b, i: (b, i, 0)),
                  pl.BlockSpec((1, d), lambda b, i: (0, 0)),
                  _mod_spec(0, lambda b: b)(d),
                  _mod_spec(1, lambda b: b)(d),
                  pl.BlockSpec(w_bf.shape, lambda b, i: (0, 0))],
        out_specs=[o[0] for o in outs],
        out_shape=[o[1] for o in outs],
        compiler_params=_params(("arbitrary", "arbitrary"), VMEM_LIMIT),
        name="inproj",
    )(x, g, mods4, mods4, w_bf)


def _inproj_ctx(ctx, g, mods4, w_kv_bf, aw, ctx_row):
    bsz, l, d = ctx.shape
    spec = pl.BlockSpec((None, l, aw), lambda b, i: (b, 0, 0))
    shape = jax.ShapeDtypeStruct((bsz, l, aw), BF16)
    return pl.pallas_call(
        functools.partial(_inproj_kv_body, aw=aw),
        grid=(bsz, 1),
        in_specs=[pl.BlockSpec((None, l, d), lambda b, i: (b, 0, 0)),
                  pl.BlockSpec((1, d), lambda b, i: (0, 0)),
                  _mod_spec(0, lambda b: ctx_row)(d),
                  _mod_spec(1, lambda b: ctx_row)(d),
                  pl.BlockSpec(w_kv_bf.shape, lambda b, i: (0, 0))],
        out_specs=[spec, spec],
        out_shape=[shape, shape],
        compiler_params=_params(("arbitrary", "arbitrary"), VMEM_LIMIT),
        name="inproj_ctx",
    )(ctx, g, mods4, mods4, w_kv_bf)


CONV_HALO = 16
CONV_RC = 64
CONV_CW = 256


def _conv_body(prev_ref, cur_ref, next_ref, w_ref, b_ref, lg_ref, lb_ref, o_ref, ext, ysc, *, width, tt):
    i = pl.program_id(1)
    n = pl.num_programs(1)
    ext[0:CONV_HALO, :] = jnp.where(i > 0, prev_ref[...], 0.0)
    ext[CONV_HALO:CONV_HALO + tt, :] = cur_ref[...]
    ext[CONV_HALO + tt:, :] = jnp.where(i < n - 1, next_ref[...], 0.0)
    lead = CONV_HALO - width // 2
    for rc in range(tt // CONV_RC):
        r0 = rc * CONV_RC
        for ch in range(cur_ref.shape[-1] // CONV_CW):
            cs = slice(ch * CONV_CW, (ch + 1) * CONV_CW)
            out = None
            for m in range(SUBLANES):
                nrow = CONV_RC + (SUBLANES if m else 0)
                part = None
                for j in range(width):
                    if (lead + j) % SUBLANES != m:
                        continue
                    a = r0 + lead + j - m
                    term = w_ref[j:j + 1, cs] * ext[a:a + nrow, cs]
                    part = term if part is None else part + term
                piece = part[m:m + CONV_RC]
                out = piece if out is None else out + piece
            ysc[r0:r0 + CONV_RC, cs] = out
    y = ysc[...] + b_ref[...]
    mu = jnp.mean(y, axis=-1, keepdims=True)
    yc = y - mu
    var = jnp.mean(yc * yc, axis=-1, keepdims=True)
    yn = yc * lax.rsqrt(var + EPS) * lg_ref[...] + lb_ref[...]
    o_ref[...] = (yn * jax.nn.sigmoid(yn)).astype(BF16)


def _conv(u, conv_w, conv_b, ln_g, ln_b, tt):
    bsz, t, cw = u.shape
    width = conv_w.shape[0]
    assert width // 2 < CONV_HALO and tt % CONV_HALO == 0 and tt % CONV_RC == 0 and cw % CONV_CW == 0
    hb = tt // CONV_HALO
    nh = t // CONV_HALO
    row = lambda a: a.reshape(1, cw)
    return pl.pallas_call(
        functools.partial(_conv_body, width=width, tt=tt),
        grid=(bsz, t // tt),
        in_specs=[pl.BlockSpec((None, CONV_HALO, cw), lambda b, i: (b, jnp.maximum(i * hb - 1, 0), 0)),
                  pl.BlockSpec((None, tt, cw), lambda b, i: (b, i, 0)),
                  pl.BlockSpec((None, CONV_HALO, cw), lambda b, i: (b, jnp.minimum((i + 1) * hb, nh - 1), 0)),
                  pl.BlockSpec((width, cw), lambda b, i: (0, 0)),
                  pl.BlockSpec((1, cw), lambda b, i: (0, 0)),
                  pl.BlockSpec((1, cw), lambda b, i: (0, 0)),
                  pl.BlockSpec((1, cw), lambda b, i: (0, 0))],
        out_specs=pl.BlockSpec((None, tt, cw), lambda b, i: (b, i, 0)),
        out_shape=jax.ShapeDtypeStruct((bsz, t, cw), BF16),
        scratch_shapes=[pltpu.VMEM((tt + 2 * CONV_HALO, cw), F32), pltpu.VMEM((tt, cw), F32)],
        compiler_params=_params(("arbitrary", "arbitrary")),
        name="conv",
    )(u, u, u, conv_w, row(conv_b), row(ln_g), row(ln_b))


ATT_RB = 8


def _bias_body(rpb_ref, onehot_ref, valid_ref, o_ref):
    t = jnp.dot(rpb_ref[...], onehot_ref[...], preferred_element_type=F32, precision=HIGHEST)
    o_ref[...] = jnp.where(valid_ref[...] > 0.5, t, NEG_INF)


def _bias_table(rpb):
    nh, ndr, ndc = rpb.shape
    qc = jnp.arange(GRID_W)[:, None]
    kc = jnp.arange(GRID_W)[None, :]
    c0 = jnp.clip(qc - NA_COLS // 2, 0, GRID_W - NA_COLS)
    valid = ((kc >= c0) & (kc < c0 + NA_COLS)).astype(F32).reshape(1, GRID_W * GRID_W)
    d_col = jnp.clip(kc - qc + (NA_COLS - 1), 0, 2 * NA_COLS - 2).reshape(1, GRID_W * GRID_W)
    ndc_pad = -(-ndc // SUBLANES) * SUBLANES
    onehot = (jnp.arange(ndc_pad)[:, None] == d_col).astype(F32)
    rpb2 = jnp.pad(rpb.reshape(nh * ndr, ndc), ((0, 0), (0, ndc_pad - ndc)))
    toeplitz = pl.pallas_call(
        _bias_body,
        out_shape=jax.ShapeDtypeStruct((nh * ndr, GRID_W * GRID_W), F32),
        name="bias",
    )(rpb2, onehot, valid).reshape(nh, ndr, GRID_W, GRID_W)
    return jnp.concatenate([toeplitz[:, :-1], toeplitz[:, 1:]], axis=-1)


def _attn_body(q_ref, kp_ref, kc_ref, kn_ref, vp_ref, vc_ref, vn_ref, kx_ref, vx_ref, bias_ref, o_ref,
               kwin, vwin, s_a, p_a, m_a, l_a, *, rows, kh, n_pairs):
    s_sc, p_sc, m_sc, l_sc = (s_a,), (p_a,), (m_a,), (l_a,)
    i = pl.program_id(1)
    blk = ATT_RB * GRID_W
    for w, (a, b) in enumerate(((kp_ref, vp_ref), (kc_ref, vc_ref), (kn_ref, vn_ref))):
        kwin[w * blk:(w + 1) * blk, :] = a[...]
        vwin[w * blk:(w + 1) * blk, :] = b[...]
    lo = lax.broadcasted_iota(I32, (GRID_W, LANES), 1) < LANES // 2
    band = kh * GRID_W

    def lane_groups(x, op):
        parts = [x[:, g * LANES:(g + 1) * LANES] for g in range(x.shape[-1] // LANES)]
        while len(parts) > 1:
            parts = [op(parts[a], parts[a + 1]) for a in range(0, len(parts) - 1, 2)] + parts[len(parts) & ~1:]
        return parts[0]

    def band_of(rr):
        r = i * ATT_RB + rr
        r0 = jnp.clip(r - kh // 2, 0, rows - kh)
        off = pl.multiple_of((r0 - (i - 1) * ATT_RB) * GRID_W, GRID_W)
        qoff = rr * GRID_W
        return r - r0, off, qoff if isinstance(qoff, int) else pl.multiple_of(qoff, GRID_W)

    def scores(rr, slot):
        delta, off, qoff = band_of(rr)
        for p in range(n_pairs):
            cols = slice(p * LANES, (p + 1) * LANES)
            qp = q_ref[pl.ds(qoff, GRID_W), cols]
            kb = kwin[pl.ds(off, band), cols]
            kx = kx_ref[:, cols]
            for s in range(2):
                h = 2 * p + s
                qz = jnp.where(lo if s == 0 else jnp.logical_not(lo), qp, jnp.zeros_like(qp))
                bias = jnp.concatenate(
                    [bias_ref[h, 2 * g + (kh - 1) - delta] for g in range(kh // 2)], axis=-1)
                s_loc = lax.dot_general(qz, kb, _NT, preferred_element_type=F32) + bias
                s_ctx = lax.dot_general(qz, kx, _NT, preferred_element_type=F32)
                s_sc[slot][h,:, 0:band] = s_loc
                s_sc[slot][h,:, band:] = s_ctx
                m_sc[slot][h] = jnp.maximum(lane_groups(s_loc, jnp.maximum), lane_groups(s_ctx, jnp.maximum))

    def numerators(slot):
        for h in range(2 * n_pairs):
            m = jnp.max(m_sc[slot][h], axis=-1, keepdims=True)
            e = jnp.exp(s_sc[slot][h] - m)
            p_sc[slot][h] = e.astype(BF16)
            l_sc[slot][h] = lane_groups(e, jnp.add)

    def values(rr, slot):
        _, off, qoff = band_of(rr)
        for p in range(n_pairs):
            cols = slice(p * LANES, (p + 1) * LANES)
            vb = vwin[pl.ds(off, band), cols]
            vx = vx_ref[:, cols]
            outs = []
            for s in range(2):
                h = 2 * p + s
                o = (jnp.dot(p_sc[slot][h,:, 0:band], vb, preferred_element_type=F32)
                     + jnp.dot(p_sc[slot][h,:, band:], vx, preferred_element_type=F32))
                outs.append(o / jnp.sum(l_sc[slot][h], axis=-1, keepdims=True))
            o_ref[pl.ds(qoff, GRID_W), cols] = jnp.where(lo, outs[0], outs[1]).astype(BF16)

    def row_body(rr, carry):
        scores(rr, 0)
        numerators(0)
        values(rr, 0)
        return carry

    lax.fori_loop(0, ATT_RB, row_body, 0, unroll=2)


def _attn(q, k, v, kx, vx, bias):
    bsz, t, aw = q.shape
    l = kx.shape[1]
    rows = t // GRID_W
    kh = (bias.shape[1] + 2) // 2
    nb = rows // ATT_RB
    blk = ATT_RB * GRID_W
    assert kh == ATT_RB and rows % ATT_RB == 0 and aw % LANES == 0 and 2 * GRID_W == LANES
    cur = pl.BlockSpec((None, blk, aw), lambda b, i: (b, i, 0))
    prev = pl.BlockSpec((None, blk, aw), lambda b, i: (b, jnp.maximum(i - 1, 0), 0))
    nxt = pl.BlockSpec((None, blk, aw), lambda b, i: (b, jnp.minimum(i + 1, nb - 1), 0))
    cx = pl.BlockSpec((None, l, aw), lambda b, i: (b, 0, 0))
    nh = bias.shape[0]
    nkeys = kh * GRID_W + l
    assert l % LANES == 0 and nh == 2 * (aw // LANES)
    return pl.pallas_call(
        functools.partial(_attn_body, rows=rows, kh=kh, n_pairs=aw // LANES),
        grid=(bsz, nb),
        in_specs=[cur, prev, cur, nxt, prev, cur, nxt, cx, cx,
                  pl.BlockSpec(bias.shape, lambda b, i: (0, 0, 0, 0))],
        out_specs=cur,
        out_shape=jax.ShapeDtypeStruct((bsz, t, aw), BF16),
        scratch_shapes=[pltpu.VMEM((3 * blk, aw), BF16), pltpu.VMEM((3 * blk, aw), BF16),
                        pltpu.VMEM((nh, GRID_W, nkeys), F32), pltpu.VMEM((nh, GRID_W, nkeys), BF16),
                        pltpu.VMEM((nh, GRID_W, LANES), F32), pltpu.VMEM((nh, GRID_W, LANES), F32)],
        compiler_params=_params(("arbitrary", "arbitrary"), VMEM_LIMIT),
        name="attn",
    )(q, k, k, k, v, v, v, kx, vx, bias)


def _outproj_body(yc_ref, ya_ref, x_ref, wo_ref, g1_ref, gf_ref, sh_ref, sc_ref, wr_ref,
                  x1_ref, h2_ref, aff_ref, *, cw, tm, ne):
    acc = (jnp.dot(yc_ref[...], wo_ref[0:cw, :], preferred_element_type=F32)
           + jnp.dot(ya_ref[...], wo_ref[cw:, :], preferred_element_type=F32))
    x1 = x_ref[...] + g1_ref[...] * acc
    x1_ref[...] = x1
    h2 = _norm_mod(x1, gf_ref[...], sh_ref[...], sc_ref[...])
    for s in range(h2.shape[-1] // LANES):
        h2_ref[pl.ds(s, tm, stride=SUBLANES), :] = h2[:, s * LANES:(s + 1) * LANES]
    hi = h2.astype(BF16)
    lo = (h2 - hi.astype(F32)).astype(BF16)
    part_hi = jnp.dot(hi, wr_ref[...], preferred_element_type=F32)
    part_lo = jnp.dot(lo, wr_ref[...], preferred_element_type=F32)
    logits = part_hi + pltpu.roll(part_hi, LANES - ne, axis=1) + part_lo
    logits = jnp.where(lax.broadcasted_iota(I32, logits.shape, 1) < ne, logits, NEG_INF)
    e = jnp.exp(logits - jnp.max(logits, axis=-1, keepdims=True))
    aff_ref[...] = (e / jnp.sum(e, axis=-1, keepdims=True))[:, 0:ne]


def _outproj(y_conv, y_attn, x, wo_bf, mods4, g_ffn, w_router, tm):
    bsz, t, d = x.shape
    cw = y_conv.shape[-1]
    ne = w_router.shape[1]
    assert d == SUBLANES * LANES and 2 * ne <= LANES
    nt = t // tm
    w_hi = w_router.astype(BF16)
    w_lo = (w_router - w_hi.astype(F32)).astype(BF16)
    wr_t = jnp.zeros((d, LANES), BF16).at[:, 0:ne].set(w_hi).at[:, ne:2 * ne].set(w_lo)
    return pl.pallas_call(
        functools.partial(_outproj_body, cw=cw, tm=tm, ne=ne),
        grid=(bsz, nt),
        in_specs=[pl.BlockSpec((None, tm, cw), lambda b, i: (b, i, 0)),
                  pl.BlockSpec((None, tm, y_attn.shape[-1]), lambda b, i: (b, i, 0)),
                  pl.BlockSpec((None, tm, d), lambda b, i: (b, i, 0)),
                  pl.BlockSpec(wo_bf.shape, lambda b, i: (0, 0)),
                  _mod_spec(2, lambda b: b)(d),
                  pl.BlockSpec((1, d), lambda b, i: (0, 0)),
                  _mod_spec(3, lambda b: b)(d),
                  _mod_spec(4, lambda b: b)(d),
                  pl.BlockSpec(wr_t.shape, lambda b, i: (0, 0))],
        out_specs=[pl.BlockSpec((None, tm, d), lambda b, i: (b, i, 0)),
                   pl.BlockSpec((tm * SUBLANES, LANES), lambda b, i: (b * nt + i, 0)),
                   pl.BlockSpec((None, tm, ne), lambda b, i: (b, i, 0))],
        out_shape=[jax.ShapeDtypeStruct((bsz, t, d), F32),
                   jax.ShapeDtypeStruct((bsz * t * SUBLANES, LANES), F32),
                   jax.ShapeDtypeStruct((bsz, t, ne), F32)],
        compiler_params=_params(("arbitrary", "arbitrary"), VMEM_LIMIT),
        name="outproj",
    )(y_conv, y_attn, x, wo_bf, mods4, g_ffn, mods4, mods4, wr_t)


def _route_body(affe_ref, affr_ref, u_ref, ones_ref, l_ref, tok_ref, *, cap, ne, nblk):
    aff_e = affe_ref[...]

    def search(it, bits):
        cand = bits | jnp.left_shift(jnp.int32(1), 30 - it)
        cnt = jnp.sum(jnp.where(aff_e >= pltpu.bitcast(cand, F32), 1.0, 0.0), axis=1, keepdims=True)
        return jnp.where(cnt >= cap, cand, bits)

    tau = pltpu.bitcast(lax.fori_loop(0, 31, search, jnp.zeros((ne, 1), I32)), F32)
    n_gt = jnp.sum(jnp.where(aff_e > tau, 1.0, 0.0), axis=1, keepdims=True)
    need = cap - n_gt

    def per_row(col):
        return jnp.concatenate([jnp.broadcast_to(col[e:e + 1, :], (nblk, 1)) for e in range(ne)], axis=0)

    tau_r = per_row(tau)
    need_r = per_row(need)
    kr = affr_ref[...]

    def scan(mf):
        mb = mf.astype(BF16)
        incl = jnp.dot(mb, u_ref[...], preferred_element_type=F32)
        tot = jnp.dot(mb, ones_ref[...], preferred_element_type=F32)
        off = jnp.dot(l_ref[...], tot.astype(BF16), preferred_element_type=F32)
        return incl, tot, off

    eq = jnp.where(kr == tau_r, 1.0, 0.0)
    incl_e, _, off_e = scan(eq)
    rank_eq = off_e + incl_e - eq
    sel = jnp.where((kr > tau_r) | ((eq > 0.5) & (rank_eq < need_r)), 1.0, 0.0)
    incl, _, off = scan(sel)

    nbits = LANES.bit_length() - 1
    lane = lax.broadcasted_iota(I32, kr.shape, 1)
    row = lax.broadcasted_iota(I32, kr.shape, 0)
    dist = lane - (incl.astype(I32) - 1)
    word = jnp.where(sel > 0.5, (1 << (2 * nbits)) | (dist << nbits) | lane, 0)

    def bit(x, pos):
        return (x >> pos) & 1

    for k in range(nbits):
        arriving = pltpu.roll(word, LANES - (1 << k), axis=1)
        take = bit(arriving, 2 * nbits) & bit(arriving, nbits + k)
        stay = bit(word, 2 * nbits) & (1 - bit(word, nbits + k))
        word = jnp.where(take == 1, arriving, jnp.where(stay == 1, word, 0))

    off_i = off.astype(I32)
    shift = off_i & (LANES - 1)
    for k in range(nbits):
        word = jnp.where(bit(shift, k) == 1, pltpu.roll(word, 1 << k, axis=1), word)
    valid = bit(word, 2 * nbits) == 1
    wrapped = lane < shift
    blk_bits = nblk.bit_length() - 1
    q0 = (row >> blk_bits) * (cap // LANES) + (off_i >> nbits)
    digits = (jnp.where(valid, row & (nblk - 1), 0), jnp.where(valid, word & (LANES - 1), 0))

    def place(sel_rows, q):
        onehot_t = jnp.where(q == lane, 1.0, 0.0).astype(BF16)
        outs = []
        for dgt in digits:
            x = jnp.where(sel_rows, dgt, 0).astype(F32).astype(BF16)
            outs.append(lax.dot_general(onehot_t, x, (((0,), (0,)), ((), ())), preferred_element_type=F32))
        return outs[0] * LANES + outs[1]

    tok_ref[...] = (place(jnp.logical_not(wrapped), q0) + place(wrapped, q0 + 1)).astype(I32)


def _route(aff, cap):
    bsz, ne, t = aff.shape
    nblk = t // LANES
    nr = ne * nblk
    nq = ne * cap // LANES
    assert nq == LANES and cap % LANES == 0 and nblk & (nblk - 1) == 0 and nblk <= 256
    aff_r = aff.reshape(bsz, nr, LANES)
    ar = jnp.arange(LANES)
    upper = (ar[:, None] <= ar[None, :]).astype(BF16)
    ones = jnp.ones((LANES, LANES), BF16)
    rr = jnp.arange(nr)
    lower = ((rr[:, None] // nblk == rr[None, :] // nblk) & (rr[None, :] < rr[:, None])).astype(BF16)
    const = lambda a: pl.BlockSpec(a.shape, lambda b: (0,) * a.ndim)
    return pl.pallas_call(
        functools.partial(_route_body, cap=cap, ne=ne, nblk=nblk),
        grid=(bsz,),
        in_specs=[pl.BlockSpec((None, ne, t), lambda b: (b, 0, 0)),
                  pl.BlockSpec((None, nr, LANES), lambda b: (b, 0, 0)),
                  const(upper), const(ones), const(lower)],
        out_specs=pl.BlockSpec((None, nq, LANES), lambda b: (b, 0, 0)),
        out_shape=jax.ShapeDtypeStruct((bsz, nq, LANES), I32),
        compiler_params=_params(("arbitrary",), VMEM_LIMIT),
        name="route",
    )(aff, aff_r, upper, ones, lower)


MOE_HC = 512
MOE_TM = 512
ROW_UNROLL = 8


def _gather_copy(h2_hbm, xs3, sem, token, slot):
    src = h2_hbm.at[pl.ds(pl.multiple_of(token * SUBLANES, SUBLANES), SUBLANES), :]
    dst = xs3.at[pl.ds(pl.multiple_of(slot * SUBLANES, SUBLANES), SUBLANES), :]
    return pltpu.make_async_copy(src, dst, sem)


def _moe_body(tok_ref, nxt_ref, h2_hbm, wg_ref, wu_ref, wd_ref, y_ref, xs3, xsb, wgb, wub, wdb, sem,
              *, bsz, t, cap, ne, nch):
    e = pl.program_id(0)
    c = pl.program_id(1)
    n_rows = bsz * cap
    n_tiles = n_rows // MOE_TM
    per_tile = n_rows // (nch * n_tiles)

    def fetch(ref, b, slot):
        _gather_copy(h2_hbm, xs3, sem, b * t + ref[b, 0, slot], b * cap + slot).start()

    def wait_staging():
        pltpu.make_async_copy(h2_hbm.at[pl.ds(0, n_rows * SUBLANES), :], xs3, sem).wait()

    @pl.when((e == 0) & (c == 0))
    def _first_expert():
        for b in range(bsz):
            def rows(g, carry):
                for k in range(ROW_UNROLL):
                    fetch(tok_ref, b, g * ROW_UNROLL + k)
                return carry

            lax.fori_loop(0, cap // ROW_UNROLL, rows, 0)

    @pl.when(c == 0)
    def _stage():
        wait_staging()
        for s in range(SUBLANES):
            xsb[:, s * LANES:(s + 1) * LANES] = xs3[pl.ds(s, n_rows, stride=SUBLANES), :].astype(BF16)
        y_ref[...] = jnp.zeros_like(y_ref)

    wgb[...] = wg_ref[...].astype(BF16)
    wub[...] = wu_ref[...].astype(BF16)
    wdb[...] = wd_ref[...].astype(BF16)

    def tile(m, carry):
        rows = pl.ds(pl.multiple_of(m * MOE_TM, MOE_TM), MOE_TM)
        xm = xsb[rows, :]
        gate = jnp.dot(xm, wgb[...], preferred_element_type=F32)
        first = (c * n_tiles + m) * per_tile
        b = first // cap
        for k in range(per_tile):
            fetch(nxt_ref, b, first - b * cap + k)
        up = jnp.dot(xm, wub[...], preferred_element_type=F32)
        hid = (gate * jax.nn.sigmoid(gate) * up).astype(BF16)
        y_ref[rows, :] += jnp.dot(hid, wdb[...], preferred_element_type=F32)
        return carry

    lax.fori_loop(0, n_tiles, tile, 0, unroll=True)

    @pl.when((e == ne - 1) & (c == nch - 1))
    def _drain():
        wait_staging()


def _moe(tok, h2_tiles, w_gate, w_up, w_down, bsz, t, cap):
    ne, d, dh = w_gate.shape
    nch = dh // MOE_HC
    n_rows = bsz * cap
    per_tile = n_rows // (nch * (n_rows // MOE_TM))
    assert cap % ROW_UNROLL == 0 and n_rows % MOE_TM == 0 and cap % per_tile == 0
    tok_spec = lambda ahead: pl.BlockSpec((bsz, None, 1, cap),
                                          lambda e, c: (0, jnp.minimum(e + ahead, ne - 1), 0, 0),
                                          memory_space=pltpu.SMEM)
    return pl.pallas_call(
        functools.partial(_moe_body, bsz=bsz, t=t, cap=cap, ne=ne, nch=nch),
        grid=(ne, nch),
        in_specs=[tok_spec(0), tok_spec(1),
                  pl.BlockSpec(memory_space=pl.ANY),
                  pl.BlockSpec((None, d, MOE_HC), lambda e, c: (e, 0, c)),
                  pl.BlockSpec((None, d, MOE_HC), lambda e, c: (e, 0, c)),
                  pl.BlockSpec((None, MOE_HC, d), lambda e, c: (e, c, 0))],
        out_specs=pl.BlockSpec((None, n_rows, d), lambda e, c: (e, 0, 0)),
        out_shape=jax.ShapeDtypeStruct((ne, n_rows, d), F32),
        scratch_shapes=[pltpu.VMEM((n_rows * SUBLANES, LANES), F32),
                        pltpu.VMEM((n_rows, d), BF16),
                        pltpu.VMEM((d, MOE_HC), BF16),
                        pltpu.VMEM((d, MOE_HC), BF16),
                        pltpu.VMEM((MOE_HC, d), BF16),
                        pltpu.SemaphoreType.DMA],
        compiler_params=_params(("arbitrary", "arbitrary"), VMEM_LIMIT),
        name="moe",
    )(tok, tok, h2_tiles, w_gate, w_up, w_down)


def _combine_body(tok_ref, aff_ref, y_ref, acc_hbm, acc, ysc, sem, *, t, cap, ne):
    e = pl.program_id(1)
    b = pl.program_id(0)

    @pl.when(e == 0)
    def _():
        acc[...] = jnp.zeros_like(acc)

    for s in range(SUBLANES):
        ysc[pl.ds(s, cap, stride=SUBLANES), :] = y_ref[:, s * LANES:(s + 1) * LANES]

    def tile_of(row):
        return pl.ds(pl.multiple_of(row * SUBLANES, SUBLANES), SUBLANES)

    def rows(g, carry):
        tokens = [tok_ref[0, g * ROW_UNROLL + k] for k in range(ROW_UNROLL)]
        new = [acc[tile_of(tk), :] + aff_ref[0, tk] * ysc[tile_of(g * ROW_UNROLL + k), :]
               for k, tk in enumerate(tokens)]
        for tk, val in zip(tokens, new):
            acc[tile_of(tk), :] = val
        return carry

    lax.fori_loop(0, cap // ROW_UNROLL, rows, 0)

    @pl.when(e == ne - 1)
    def _():
        out = acc_hbm.at[pl.ds(pl.multiple_of(b * t * SUBLANES, SUBLANES), t * SUBLANES), :]
        copy = pltpu.make_async_copy(acc, out, sem)
        copy.start()
        copy.wait()


def _combine(tok, aff, y, bsz, t, cap):
    ne = aff.shape[1]
    d = y.shape[-1]
    smem = lambda n: pl.BlockSpec((None, None, 1, n), lambda b, e: (b, e, 0, 0), memory_space=pltpu.SMEM)
    return pl.pallas_call(
        functools.partial(_combine_body, t=t, cap=cap, ne=ne),
        grid=(bsz, ne),
        in_specs=[smem(cap), smem(t),
                  pl.BlockSpec((None, cap, d), lambda b, e: (e, b, 0))],
        out_specs=pl.BlockSpec(memory_space=pl.ANY),
        out_shape=jax.ShapeDtypeStruct((bsz * t * SUBLANES, LANES), F32),
        scratch_shapes=[pltpu.VMEM((t * SUBLANES, LANES), F32),
                        pltpu.VMEM((cap * SUBLANES, LANES), F32),
                        pltpu.SemaphoreType.DMA],
        compiler_params=_params(("arbitrary", "arbitrary"), VMEM_LIMIT),
        name="combine",
    )(tok, aff, y)


def _final_body(x1_ref, acc_ref, g2_ref, gf_ref, o_ref, *, tm):
    moe = jnp.concatenate([acc_ref[pl.ds(s, tm, stride=SUBLANES), :] for s in range(SUBLANES)], axis=-1)
    x = x1_ref[...] + g2_ref[...] * moe
    ms = jnp.mean(x * x, axis=-1, keepdims=True)
    o_ref[...] = x * lax.rsqrt(ms + EPS) * gf_ref[...]


def _final(x1, acc_tiles, mods4, g_final, tm):
    bsz, t, d = x1.shape
    nt = t // tm
    return pl.pallas_call(
        functools.partial(_final_body, tm=tm),
        grid=(bsz, nt),
        in_specs=[pl.BlockSpec((None, tm, d), lambda b, i: (b, i, 0)),
                  pl.BlockSpec((tm * SUBLANES, LANES), lambda b, i: (b * nt + i, 0)),
                  _mod_spec(5, lambda b: b)(d),
                  pl.BlockSpec((1, d), lambda b, i: (0, 0))],
        out_specs=pl.BlockSpec((None, tm, d), lambda b, i: (b, i, 0)),
        out_shape=jax.ShapeDtypeStruct((bsz, t, d), F32),
        compiler_params=_params(("arbitrary", "arbitrary")),
        name="final",
    )(x1, acc_tiles, mods4, g_final)


def _layer(x, ctx, mods4, ctx_row, norm_mix_g, w_in, conv_w, conv_b, conv_ln_g, conv_ln_b, rpb, w_out,
           norm_ffn_g, w_router, w_gate, w_up, w_down):
    bsz, t, d = x.shape
    cw = conv_w.shape[1]
    n_heads = rpb.shape[0]
    aw = (w_in.shape[1] - 2 * cw) // 3
    head_dim = aw // n_heads
    rows = t // GRID_W
    kh = min((rpb.shape[1] + 1) // 2, rows)
    ne = w_router.shape[1]
    cap = EC_FACTOR * t // ne
    row = lambda a: a.reshape(1, -1)

    w_in_bf = w_in.astype(BF16)
    u, q, k, v = _inproj(x, row(norm_mix_g), mods4, w_in_bf, cw, aw, head_dim ** -0.5, tm=512)
    kx, vx = _inproj_ctx(ctx, row(norm_mix_g), mods4, w_in_bf[:, 2 * cw + aw:], aw, ctx_row)
    y_conv = _conv(u, conv_w, conv_b, conv_ln_g, conv_ln_b, tt=256)
    assert rows >= (rpb.shape[1] + 1) // 2
    y_attn = _attn(q, k, v, kx, vx, _bias_table(rpb))
    x1, h2_tiles, aff_te = _outproj(y_conv, y_attn, x, w_out.astype(BF16), mods4, row(norm_ffn_g),
                                    w_router, tm=512)
    aff = jnp.swapaxes(aff_te, 1, 2)
    tok = _route(aff, cap).reshape(bsz, ne, 1, cap)
    y = _moe(tok, h2_tiles, w_gate, w_up, w_down, bsz, t, cap)
    acc_tiles = _combine(tok, aff.reshape(bsz, ne, 1, t), y, bsz, t, cap)
    return x1, acc_tiles


def kernel(x, c, ctx, c_ctx, w_mod, b_mod, norm_mix_g, w_in, conv_w, conv_b, conv_ln_g, conv_ln_b, rpb,
           w_out, norm_ffn_g, w_router, w_gate, w_up, w_down, norm_final_g):
    bsz, t, d = x.shape
    depth = w_mod.shape[0]
    assert depth == 1
    assert bsz + 1 <= SUBLANES
    c8 = jnp.zeros((SUBLANES, d), F32).at[:bsz].set(c).at[bsz].set(c_ctx)
    mods4 = _adaln(c8, w_mod[0], b_mod[0].reshape(1, -1)).reshape(SUBLANES, N_MOD, 1, d)
    x1, acc_tiles = _layer(x, ctx, mods4, bsz, norm_mix_g[0], w_in[0], conv_w[0], conv_b[0], conv_ln_g[0],
                           conv_ln_b[0], rpb[0], w_out[0], norm_ffn_g[0], w_router[0], w_gate[0], w_up[0],
                           w_down[0])
    return _final(x1, acc_tiles, mods4, norm_final_g.reshape(1, -1), tm=512)
```

```python
import functools

import jax
import jax.numpy as jnp
from jax import lax
from jax.experimental import pallas as pl
from jax.experimental.pallas import tpu as pltpu

F32 = jnp.float32
BF16 = jnp.bfloat16
I32 = jnp.int32
HIGHEST = lax.Precision.HIGHEST

GRID_W = 64
NA_COLS = 16
EC_FACTOR = 2
N_MOD = 6
EPS = 1e-6
NEG_INF = -1e30
LANES = 128
SUBLANES = 8
VMEM_LIMIT = 56 * 1024 * 1024

_NT = (((1,), (1,)), ((), ()))


def _params(sem, vmem=None):
    return pltpu.CompilerParams(dimension_semantics=sem, vmem_limit_bytes=vmem)


def _adaln_body(c_ref, w_ref, b_ref, o_ref):
    c = c_ref[...]
    s = c * jax.nn.sigmoid(c)
    o_ref[...] = jnp.dot(s, w_ref[...], preferred_element_type=F32, precision=HIGHEST) + b_ref[...]


def _adaln(c8, w_mod, b_mod):
    d, n = w_mod.shape
    bn = 512
    return pl.pallas_call(
        _adaln_body,
        grid=(n // bn,),
        in_specs=[pl.BlockSpec((SUBLANES, d), lambda j: (0, 0)),
                  pl.BlockSpec((d, bn), lambda j: (0, j)),
                  pl.BlockSpec((1, bn), lambda j: (0, j))],
        out_specs=pl.BlockSpec((SUBLANES, bn), lambda j: (0, j)),
        out_shape=jax.ShapeDtypeStruct((SUBLANES, n), F32),
        compiler_params=_params(("arbitrary",)),
        name="adaln",
    )(c8, w_mod, b_mod)


def _norm_mod(x, g, shift, scale):
    ms = jnp.mean(x * x, axis=-1, keepdims=True)
    h = x * lax.rsqrt(ms + EPS) * g
    return h * (1.0 + scale) + shift


def _inproj_body(x_ref, g_ref, sh_ref, sc_ref, w_ref, u_ref, q_ref, k_ref, v_ref, *, cw, aw, qscale):
    hb = _norm_mod(x_ref[...], g_ref[...], sh_ref[...], sc_ref[...]).astype(BF16)

    def proj(lo, width):
        return jnp.dot(hb, w_ref[:, lo:lo + width], preferred_element_type=F32)

    a = proj(0, cw)
    gate = proj(cw, cw)
    u_ref[...] = a * jax.nn.sigmoid(gate)
    q_ref[...] = (proj(2 * cw, aw) * qscale).astype(BF16)
    k_ref[...] = proj(2 * cw + aw, aw).astype(BF16)
    v_ref[...] = proj(2 * cw + 2 * aw, aw).astype(BF16)


def _inproj_kv_body(x_ref, g_ref, sh_ref, sc_ref, w_ref, k_ref, v_ref, *, aw):
    hb = _norm_mod(x_ref[...], g_ref[...], sh_ref[...], sc_ref[...]).astype(BF16)
    k_ref[...] = jnp.dot(hb, w_ref[:, 0:aw], preferred_element_type=F32).astype(BF16)
    v_ref[...] = jnp.dot(hb, w_ref[:, aw:2 * aw], preferred_element_type=F32).astype(BF16)


def _mod_spec(slot, row_of):
    return lambda d: pl.BlockSpec((None, None, 1, d), lambda b, i: (row_of(b), slot, 0, 0))


def _inproj(x, g, mods4, w_bf, cw, aw, qscale, tm):
    bsz, t, d = x.shape
    tok = lambda w, dt: (pl.BlockSpec((None, tm, w), lambda b, i: (b, i, 0)), jax.ShapeDtypeStruct((bsz, t, w), dt))
    outs = [tok(cw, F32), tok(aw, BF16), tok(aw, BF16), tok(aw, BF16)]
    return pl.pallas_call(
        functools.partial(_inproj_body, cw=cw, aw=aw, qscale=qscale),
        grid=(bsz, t // tm),
        in_specs=[pl.BlockSpec((None, tm, d), lambda b, i: (b, i, 0)),
                  pl.BlockSpec((1, d), lambda b, i: (0, 0)),
                  _mod_spec(0, lambda b: b)(d),
                  _mod_spec(1, lambda b: b)(d),
                  pl.BlockSpec(w_bf.shape, lambda b, i: (0, 0))],
        out_specs=[o[0] for o in outs],
        out_shape=[o[1] for o in outs],
        compiler_params=_params(("arbitrary", "arbitrary"), VMEM_LIMIT),
        name="inproj",
    )(x, g, mods4, mods4, w_bf)


def _inproj_ctx(ctx, g, mods4, w_kv_bf, aw, ctx_row):
    bsz, l, d = ctx.shape
    spec = pl.BlockSpec((None, l, aw), lambda b, i: (b, 0, 0))
    shape = jax.ShapeDtypeStruct((bsz, l, aw), BF16)
    return pl.pallas_call(
        functools.partial(_inproj_kv_body, aw=aw),
        grid=(bsz, 1),
        in_specs=[pl.BlockSpec((None, l, d), lambda b, i: (b, 0, 0)),
                  pl.BlockSpec((1, d), lambda b, i: (0, 0)),
                  _mod_spec(0, lambda b: ctx_row)(d),
                  _mod_spec(1, lambda b: ctx_row)(d),
                  pl.BlockSpec(w_kv_bf.shape, lambda b, i: (0, 0))],
        out_specs=[spec, spec],
        out_shape=[shape, shape],
        compiler_params=_params(("arbitrary", "arbitrary"), VMEM_LIMIT),
        name="inproj_ctx",
    )(ctx, g, mods4, mods4, w_kv_bf)


CONV_HALO = 16
CONV_RC = 64
CONV_CW = 256


def _conv_body(prev_ref, cur_ref, next_ref, w_ref, b_ref, lg_ref, lb_ref, o_ref, ext, ysc, *, width, tt):
    i = pl.program_id(1)
    n = pl.num_programs(1)
    ext[0:CONV_HALO, :] = jnp.where(i > 0, prev_ref[...], 0.0)
    ext[CONV_HALO:CONV_HALO + tt, :] = cur_ref[...]
    ext[CONV_HALO + tt:, :] = jnp.where(i < n - 1, next_ref[...], 0.0)
    lead = CONV_HALO - width // 2
    for rc in range(tt // CONV_RC):
        r0 = rc * CONV_RC
        for ch in range(cur_ref.shape[-1] // CONV_CW):
            cs = slice(ch * CONV_CW, (ch + 1) * CONV_CW)
            out = None
            for m in range(SUBLANES):
                nrow = CONV_RC + (SUBLANES if m else 0)
                part = None
                for j in range(width):
                    if (lead + j) % SUBLANES != m:
                        continue
                    a = r0 + lead + j - m
                    term = w_ref[j:j + 1, cs] * ext[a:a + nrow, cs]
                    part = term if part is None else part + term
                piece = part[m:m + CONV_RC]
                out = piece if out is None else out + piece
            ysc[r0:r0 + CONV_RC, cs] = out
    y = ysc[...] + b_ref[...]
    mu = jnp.mean(y, axis=-1, keepdims=True)
    yc = y - mu
    var = jnp.mean(yc * yc, axis=-1, keepdims=True)
    yn = yc * lax.rsqrt(var + EPS) * lg_ref[...] + lb_ref[...]
    o_ref[...] = (yn * jax.nn.sigmoid(yn)).astype(BF16)


def _conv(u, conv_w, conv_b, ln_g, ln_b, tt):
    bsz, t, cw = u.shape
    width = conv_w.shape[0]
    assert width // 2 < CONV_HALO and tt % CONV_HALO == 0 and tt % CONV_RC == 0 and cw % CONV_CW == 0
    hb = tt // CONV_HALO
    nh = t // CONV_HALO
    row = lambda a: a.reshape(1, cw)
    return pl.pallas_call(
        functools.partial(_conv_body, width=width, tt=tt),
        grid=(bsz, t // tt),
        in_specs=[pl.BlockSpec((None, CONV_HALO, cw), lambda b, i: (b, jnp.maximum(i * hb - 1, 0), 0)),
                  pl.BlockSpec((None, tt, cw), lambda b, i: (b, i, 0)),
                  pl.BlockSpec((None, CONV_HALO, cw), lambda b, i: (b, jnp.minimum((i + 1) * hb, nh - 1), 0)),
                  pl.BlockSpec((width, cw), lambda b, i: (0, 0)),
                  pl.BlockSpec((1, cw), lambda b, i: (0, 0)),
                  pl.BlockSpec((1, cw), lambda b, i: (0, 0)),
                  pl.BlockSpec((1, cw), lambda b, i: (0, 0))],
        out_specs=pl.BlockSpec((None, tt, cw), lambda b, i: (b, i, 0)),
        out_shape=jax.ShapeDtypeStruct((bsz, t, cw), BF16),
        scratch_shapes=[pltpu.VMEM((tt + 2 * CONV_HALO, cw), F32), pltpu.VMEM((tt, cw), F32)],
        compiler_params=_params(("arbitrary", "arbitrary")),
        name="conv",
    )(u, u, u, conv_w, row(conv_b), row(ln_g), row(ln_b))


ATT_RB = 8


def _bias_body(rpb_ref, onehot_ref, valid_ref, o_ref):
    t = jnp.dot(rpb_ref[...], onehot_ref[...], preferred_element_type=F32, precision=HIGHEST)
    o_ref[...] = jnp.where(valid_ref[...] > 0.5, t, NEG_INF)


def _bias_table(rpb):
    nh, ndr, ndc = rpb.shape
    qc = jnp.arange(GRID_W)[:, None]
    kc = jnp.arange(GRID_W)[None, :]
    c0 = jnp.clip(qc - NA_COLS // 2, 0, GRID_W - NA_COLS)
    valid = ((kc >= c0) & (kc < c0 + NA_COLS)).astype(F32).reshape(1, GRID_W * GRID_W)
    d_col = jnp.clip(kc - qc + (NA_COLS - 1), 0, 2 * NA_COLS - 2).reshape(1, GRID_W * GRID_W)
    ndc_pad = -(-ndc // SUBLANES) * SUBLANES
    onehot = (jnp.arange(ndc_pad)[:, None] == d_col).astype(F32)
    rpb2 = jnp.pad(rpb.reshape(nh * ndr, ndc), ((0, 0), (0, ndc_pad - ndc)))
    toeplitz = pl.pallas_call(
        _bias_body,
        out_shape=jax.ShapeDtypeStruct((nh * ndr, GRID_W * GRID_W), F32),
        name="bias",
    )(rpb2, onehot, valid).reshape(nh, ndr, GRID_W, GRID_W)
    return jnp.concatenate([toeplitz[:, :-1], toeplitz[:, 1:]], axis=-1)


def _attn_body(q_ref, kp_ref, kc_ref, kn_ref, vp_ref, vc_ref, vn_ref, kx_ref, vx_ref, bias_ref, o_ref,
               kwin, vwin, s_sc, p_sc, m_sc, l_sc, *, rows, kh, n_pairs):
    i = pl.program_id(1)
    blk = ATT_RB * GRID_W
    for w, (a, b) in enumerate(((kp_ref, vp_ref), (kc_ref, vc_ref), (kn_ref, vn_ref))):
        kwin[w * blk:(w + 1) * blk, :] = a[...]
        vwin[w * blk:(w + 1) * blk, :] = b[...]
    lo = lax.broadcasted_iota(I32, (GRID_W, LANES), 1) < LANES // 2
    band = kh * GRID_W

    def lane_groups(x, op):
        parts = [x[:, g * LANES:(g + 1) * LANES] for g in range(x.shape[-1] // LANES)]
        while len(parts) > 1:
            parts = [op(parts[a], parts[a + 1]) for a in range(0, len(parts) - 1, 2)] + parts[len(parts) & ~1:]
        return parts[0]

    def row_body(rr, carry):
        r = i * ATT_RB + rr
        r0 = jnp.clip(r - kh // 2, 0, rows - kh)
        delta = r - r0
        off = pl.multiple_of((r0 - (i - 1) * ATT_RB) * GRID_W, GRID_W)
        qoff = pl.multiple_of(rr * GRID_W, GRID_W)
        for p in range(n_pairs):
            cols = slice(p * LANES, (p + 1) * LANES)
            qp = q_ref[pl.ds(qoff, GRID_W), cols]
            kb = kwin[pl.ds(off, band), cols]
            kx = kx_ref[:, cols]
            for s in range(2):
                h = 2 * p + s
                qz = jnp.where(lo if s == 0 else jnp.logical_not(lo), qp, jnp.zeros_like(qp))
                bias = jnp.concatenate(
                    [bias_ref[h, 2 * g + (kh - 1) - delta] for g in range(kh // 2)], axis=-1)
                s_loc = lax.dot_general(qz, kb, _NT, preferred_element_type=F32) + bias
                s_ctx = lax.dot_general(qz, kx, _NT, preferred_element_type=F32)
                s_sc[h, :, 0:band] = s_loc
                s_sc[h, :, band:] = s_ctx
                m_sc[h] = jnp.maximum(lane_groups(s_loc, jnp.maximum), lane_groups(s_ctx, jnp.maximum))
        for h in range(2 * n_pairs):
            m = jnp.max(m_sc[h], axis=-1, keepdims=True)
            e = jnp.exp(s_sc[h] - m)
            p_sc[h] = e.astype(BF16)
            l_sc[h] = lane_groups(e, jnp.add)
        for p in range(n_pairs):
            cols = slice(p * LANES, (p + 1) * LANES)
            vb = vwin[pl.ds(off, band), cols]
            vx = vx_ref[:, cols]
            outs = []
            for s in range(2):
                h = 2 * p + s
                o = (jnp.dot(p_sc[h, :, 0:band], vb, preferred_element_type=F32)
                     + jnp.dot(p_sc[h, :, band:], vx, preferred_element_type=F32))
                outs.append(o / jnp.sum(l_sc[h], axis=-1, keepdims=True))
            o_ref[pl.ds(qoff, GRID_W), cols] = jnp.where(lo, outs[0], outs[1]).astype(BF16)
        return carry

    lax.fori_loop(0, ATT_RB, row_body, 0, unroll=2)


def _attn(q, k, v, kx, vx, bias):
    bsz, t, aw = q.shape
    l = kx.shape[1]
    rows = t // GRID_W
    kh = (bias.shape[1] + 2) // 2
    nb = rows // ATT_RB
    blk = ATT_RB * GRID_W
    assert kh == ATT_RB and rows % ATT_RB == 0 and aw % LANES == 0 and 2 * GRID_W == LANES
    cur = pl.BlockSpec((None, blk, aw), lambda b, i: (b, i, 0))
    prev = pl.BlockSpec((None, blk, aw), lambda b, i: (b, jnp.maximum(i - 1, 0), 0))
    nxt = pl.BlockSpec((None, blk, aw), lambda b, i: (b, jnp.minimum(i + 1, nb - 1), 0))
    cx = pl.BlockSpec((None, l, aw), lambda b, i: (b, 0, 0))
    nh = bias.shape[0]
    nkeys = kh * GRID_W + l
    assert l % LANES == 0 and nh == 2 * (aw // LANES)
    return pl.pallas_call(
        functools.partial(_attn_body, rows=rows, kh=kh, n_pairs=aw // LANES),
        grid=(bsz, nb),
        in_specs=[cur, prev, cur, nxt, prev, cur, nxt, cx, cx,
                  pl.BlockSpec(bias.shape, lambda b, i: (0, 0, 0, 0))],
        out_specs=cur,
        out_shape=jax.ShapeDtypeStruct((bsz, t, aw), BF16),
        scratch_shapes=[pltpu.VMEM((3 * blk, aw), BF16), pltpu.VMEM((3 * blk, aw), BF16),
                        pltpu.VMEM((nh, GRID_W, nkeys), F32), pltpu.VMEM((nh, GRID_W, nkeys), BF16),
                        pltpu.VMEM((nh, GRID_W, LANES), F32), pltpu.VMEM((nh, GRID_W, LANES), F32)],
        compiler_params=_params(("arbitrary", "arbitrary"), VMEM_LIMIT),
        name="attn",
    )(q, k, k, k, v, v, v, kx, vx, bias)


OUT_SPLIT = 2


def _outproj_body(yc_ref, ya_ref, x_ref, wo_ref, g1_ref, gf_ref, sh_ref, sc_ref, wr_ref,
                  x1_ref, h2_ref, aff_ref, *, cw, tm, ne):
    rg = tm // OUT_SPLIT

    def project(g):
        rows = slice(g * rg, (g + 1) * rg)
        acc = (jnp.dot(yc_ref[rows, :], wo_ref[0:cw, :], preferred_element_type=F32)
               + jnp.dot(ya_ref[rows, :], wo_ref[cw:, :], preferred_element_type=F32))
        x1 = x_ref[rows, :] + g1_ref[...] * acc
        x1_ref[rows, :] = x1
        return x1

    def route(g, x1):
        rows = slice(g * rg, (g + 1) * rg)
        h2 = _norm_mod(x1, gf_ref[...], sh_ref[...], sc_ref[...])
        for s in range(h2.shape[-1] // LANES):
            h2_ref[pl.ds(g * rg * SUBLANES + s, rg, stride=SUBLANES), :] = h2[:, s * LANES:(s + 1) * LANES]
        hi = h2.astype(BF16)
        lo = (h2 - hi.astype(F32)).astype(BF16)
        part_hi = jnp.dot(hi, wr_ref[...], preferred_element_type=F32)
        part_lo = jnp.dot(lo, wr_ref[...], preferred_element_type=F32)
        logits = part_hi + pltpu.roll(part_hi, LANES - ne, axis=1) + part_lo
        logits = jnp.where(lax.broadcasted_iota(I32, logits.shape, 1) < ne, logits, NEG_INF)
        e = jnp.exp(logits - jnp.max(logits, axis=-1, keepdims=True))
        aff_ref[rows, :] = (e / jnp.sum(e, axis=-1, keepdims=True))[:, 0:ne]

    x1s = [project(g) for g in range(OUT_SPLIT)]
    for g in range(OUT_SPLIT):
        route(g, x1s[g])


def _outproj(y_conv, y_attn, x, wo_bf, mods4, g_ffn, w_router, tm):
    bsz, t, d = x.shape
    cw = y_conv.shape[-1]
    ne = w_router.shape[1]
    assert d == SUBLANES * LANES and 2 * ne <= LANES and tm % OUT_SPLIT == 0
    nt = t // tm
    w_hi = w_router.astype(BF16)
    w_lo = (w_router - w_hi.astype(F32)).astype(BF16)
    wr_t = jnp.zeros((d, LANES), BF16).at[:, 0:ne].set(w_hi).at[:, ne:2 * ne].set(w_lo)
    return pl.pallas_call(
        functools.partial(_outproj_body, cw=cw, tm=tm, ne=ne),
        grid=(bsz, nt),
        in_specs=[pl.BlockSpec((None, tm, cw), lambda b, i: (b, i, 0)),
                  pl.BlockSpec((None, tm, y_attn.shape[-1]), lambda b, i: (b, i, 0)),
                  pl.BlockSpec((None, tm, d), lambda b, i: (b, i, 0)),
                  pl.BlockSpec(wo_bf.shape, lambda b, i: (0, 0)),
                  _mod_spec(2, lambda b: b)(d),
                  pl.BlockSpec((1, d), lambda b, i: (0, 0)),
                  _mod_spec(3, lambda b: b)(d),
                  _mod_spec(4, lambda b: b)(d),
                  pl.BlockSpec(wr_t.shape, lambda b, i: (0, 0))],
        out_specs=[pl.BlockSpec((None, tm, d), lambda b, i: (b, i, 0)),
                   pl.BlockSpec((tm * SUBLANES, LANES), lambda b, i: (b * nt + i, 0)),
                   pl.BlockSpec((None, tm, ne), lambda b, i: (b, i, 0))],
        out_shape=[jax.ShapeDtypeStruct((bsz, t, d), F32),
                   jax.ShapeDtypeStruct((bsz * t * SUBLANES, LANES), F32),
                   jax.ShapeDtypeStruct((bsz, t, ne), F32)],
        compiler_params=_params(("arbitrary", "arbitrary"), VMEM_LIMIT),
        name="outproj",
    )(y_conv, y_attn, x, wo_bf, mods4, g_ffn, mods4, mods4, wr_t)


def _route_body(affe_ref, affr_ref, u_ref, ones_ref, l_ref, tok_ref, *, cap, ne, nblk):
    aff_e = affe_ref[...]

    def search(it, bits):
        cand = bits | jnp.left_shift(jnp.int32(1), 30 - it)
        cnt = jnp.sum(jnp.where(aff_e >= pltpu.bitcast(cand, F32), 1.0, 0.0), axis=1, keepdims=True)
        return jnp.where(cnt >= cap, cand, bits)

    tau = pltpu.bitcast(lax.fori_loop(0, 31, search, jnp.zeros((ne, 1), I32)), F32)
    n_gt = jnp.sum(jnp.where(aff_e > tau, 1.0, 0.0), axis=1, keepdims=True)
    need = cap - n_gt

    def per_row(col):
        return jnp.concatenate([jnp.broadcast_to(col[e:e + 1, :], (nblk, 1)) for e in range(ne)], axis=0)

    tau_r = per_row(tau)
    need_r = per_row(need)
    kr = affr_ref[...]

    def scan(mf):
        mb = mf.astype(BF16)
        incl = jnp.dot(mb, u_ref[...], preferred_element_type=F32)
        tot = jnp.dot(mb, ones_ref[...], preferred_element_type=F32)
        off = jnp.dot(l_ref[...], tot.astype(BF16), preferred_element_type=F32)
        return incl, tot, off

    eq = jnp.where(kr == tau_r, 1.0, 0.0)
    incl_e, _, off_e = scan(eq)
    rank_eq = off_e + incl_e - eq
    sel = jnp.where((kr > tau_r) | ((eq > 0.5) & (rank_eq < need_r)), 1.0, 0.0)
    incl, _, off = scan(sel)

    nbits = LANES.bit_length() - 1
    lane = lax.broadcasted_iota(I32, kr.shape, 1)
    row = lax.broadcasted_iota(I32, kr.shape, 0)
    dist = lane - (incl.astype(I32) - 1)
    word = jnp.where(sel > 0.5, (1 << (2 * nbits)) | (dist << nbits) | lane, 0)

    def bit(x, pos):
        return (x >> pos) & 1

    for k in range(nbits):
        arriving = pltpu.roll(word, LANES - (1 << k), axis=1)
        take = bit(arriving, 2 * nbits) & bit(arriving, nbits + k)
        stay = bit(word, 2 * nbits) & (1 - bit(word, nbits + k))
        word = jnp.where(take == 1, arriving, jnp.where(stay == 1, word, 0))

    off_i = off.astype(I32)
    shift = off_i & (LANES - 1)
    for k in range(nbits):
        word = jnp.where(bit(shift, k) == 1, pltpu.roll(word, 1 << k, axis=1), word)
    valid = bit(word, 2 * nbits) == 1
    wrapped = lane < shift
    blk_bits = nblk.bit_length() - 1
    q0 = (row >> blk_bits) * (cap // LANES) + (off_i >> nbits)
    digits = (jnp.where(valid, row & (nblk - 1), 0), jnp.where(valid, word & (LANES - 1), 0))

    def place(sel_rows, q):
        onehot_t = jnp.where(q == lane, 1.0, 0.0).astype(BF16)
        outs = []
        for dgt in digits:
            x = jnp.where(sel_rows, dgt, 0).astype(F32).astype(BF16)
            outs.append(lax.dot_general(onehot_t, x, (((0,), (0,)), ((), ())), preferred_element_type=F32))
        return outs[0] * LANES + outs[1]

    tok_ref[...] = (place(jnp.logical_not(wrapped), q0) + place(wrapped, q0 + 1)).astype(I32)


def _route(aff, cap):
    bsz, ne, t = aff.shape
    nblk = t // LANES
    nr = ne * nblk
    nq = ne * cap // LANES
    assert nq == LANES and cap % LANES == 0 and nblk & (nblk - 1) == 0 and nblk <= 256
    aff_r = aff.reshape(bsz, nr, LANES)
    ar = jnp.arange(LANES)
    upper = (ar[:, None] <= ar[None, :]).astype(BF16)
    ones = jnp.ones((LANES, LANES), BF16)
    rr = jnp.arange(nr)
    lower = ((rr[:, None] // nblk == rr[None, :] // nblk) & (rr[None, :] < rr[:, None])).astype(BF16)
    const = lambda a: pl.BlockSpec(a.shape, lambda b: (0,) * a.ndim)
    return pl.pallas_call(
        functools.partial(_route_body, cap=cap, ne=ne, nblk=nblk),
        grid=(bsz,),
        in_specs=[pl.BlockSpec((None, ne, t), lambda b: (b, 0, 0)),
                  pl.BlockSpec((None, nr, LANES), lambda b: (b, 0, 0)),
                  const(upper), const(ones), const(lower)],
        out_specs=pl.BlockSpec((None, nq, LANES), lambda b: (b, 0, 0)),
        out_shape=jax.ShapeDtypeStruct((bsz, nq, LANES), I32),
        compiler_params=_params(("arbitrary",), VMEM_LIMIT),
        name="route",
    )(aff, aff_r, upper, ones, lower)


MOE_HC = 512
MOE_TM = 512
ROW_UNROLL = 8


def _tile_rows(ref, first, n, s):
    return ref.at[pl.ds(first * SUBLANES + s, n, stride=SUBLANES), :]


def _gather_copy(h2_hbm, xs3, sem, token, slot):
    src = h2_hbm.at[pl.ds(pl.multiple_of(token * SUBLANES, SUBLANES), SUBLANES), :]
    dst = xs3.at[pl.ds(pl.multiple_of(slot * SUBLANES, SUBLANES), SUBLANES), :]
    return pltpu.make_async_copy(src, dst, sem)


def _moe_body(tok_ref, nxt_ref, h2_hbm, wg_ref, wu_ref, wd_ref, y_ref, xs3, xsb, wgb, wub, wdb, sem,
              *, bsz, t, cap, ne, nch):
    e = pl.program_id(0)
    c = pl.program_id(1)
    n_rows = bsz * cap
    n_tiles = n_rows // MOE_TM
    per_tile = n_rows // (nch * n_tiles)
    n_lane_groups = xsb.shape[-1] // LANES

    def fetch(ref, b, slot):
        _gather_copy(h2_hbm, xs3, sem, b * t + ref[b, 0, slot], b * cap + slot).start()

    def wait_staging():
        pltpu.make_async_copy(h2_hbm.at[pl.ds(0, n_rows * SUBLANES), :], xs3, sem).wait()

    @pl.when((e == 0) & (c == 0))
    def _first_expert():
        for b in range(bsz):
            def rows(g, carry):
                for k in range(ROW_UNROLL):
                    fetch(tok_ref, b, g * ROW_UNROLL + k)
                return carry

            lax.fori_loop(0, cap // ROW_UNROLL, rows, 0)

    def tiles(first_chunk):
        wgb[...] = wg_ref[...].astype(BF16)
        wub[...] = wu_ref[...].astype(BF16)
        wdb[...] = wd_ref[...].astype(BF16)
        for m in range(n_tiles):
            r0 = m * MOE_TM
            if first_chunk:
                xm = jnp.concatenate([_tile_rows(xs3, r0, MOE_TM, s)[...].astype(BF16)
                                      for s in range(n_lane_groups)], axis=-1)
                xsb[r0:r0 + MOE_TM, :] = xm
            else:
                xm = xsb[r0:r0 + MOE_TM, :]
            gate = jnp.dot(xm, wgb[...], preferred_element_type=F32)
            first = (c * n_tiles + m) * per_tile
            b = first // cap
            for k in range(per_tile):
                fetch(nxt_ref, b, first - b * cap + k)
            up = jnp.dot(xm, wub[...], preferred_element_type=F32)
            hid = (gate * jax.nn.sigmoid(gate) * up).astype(BF16)
            contrib = jnp.dot(hid, wdb[...], preferred_element_type=F32)
            if first_chunk:
                y_ref[r0:r0 + MOE_TM, :] = contrib
            else:
                y_ref[r0:r0 + MOE_TM, :] += contrib

    @pl.when(c == 0)
    def _():
        wait_staging()
        tiles(True)

    @pl.when(c > 0)
    def _():
        tiles(False)

    @pl.when((e == ne - 1) & (c == nch - 1))
    def _drain():
        wait_staging()


def _moe(tok, h2_tiles, w_gate, w_up, w_down, bsz, t, cap):
    ne, d, dh = w_gate.shape
    nch = dh // MOE_HC
    n_rows = bsz * cap
    per_tile = n_rows // (nch * (n_rows // MOE_TM))
    assert cap % ROW_UNROLL == 0 and n_rows % MOE_TM == 0 and cap % per_tile == 0 and n_rows // nch <= MOE_TM
    assert d == SUBLANES * LANES
    tok_spec = lambda ahead: pl.BlockSpec((bsz, None, 1, cap),
                                          lambda e, c: (0, jnp.minimum(e + ahead, ne - 1), 0, 0),
                                          memory_space=pltpu.SMEM)
    return pl.pallas_call(
        functools.partial(_moe_body, bsz=bsz, t=t, cap=cap, ne=ne, nch=nch),
        grid=(ne, nch),
        in_specs=[tok_spec(0), tok_spec(1),
                  pl.BlockSpec(memory_space=pl.ANY),
                  pl.BlockSpec((None, d, MOE_HC), lambda e, c: (e, 0, c)),
                  pl.BlockSpec((None, d, MOE_HC), lambda e, c: (e, 0, c)),
                  pl.BlockSpec((None, MOE_HC, d), lambda e, c: (e, c, 0))],
        out_specs=pl.BlockSpec((None, n_rows, d), lambda e, c: (e, 0, 0)),
        out_shape=jax.ShapeDtypeStruct((ne, n_rows, d), F32),
        scratch_shapes=[pltpu.VMEM((n_rows * SUBLANES, LANES), F32),
                        pltpu.VMEM((n_rows, d), BF16),
                        pltpu.VMEM((d, MOE_HC), BF16),
                        pltpu.VMEM((d, MOE_HC), BF16),
                        pltpu.VMEM((MOE_HC, d), BF16),
                        pltpu.SemaphoreType.DMA],
        compiler_params=_params(("arbitrary", "arbitrary"), VMEM_LIMIT),
        name="moe",
    )(tok, tok, h2_tiles, w_gate, w_up, w_down)


def _combine_body(tok_ref, aff_ref, y_ref, acc_hbm, acc, ysc, sem, *, t, cap, ne):
    e = pl.program_id(1)
    b = pl.program_id(0)

    @pl.when(e == 0)
    def _():
        acc[...] = jnp.zeros_like(acc)

    for s in range(SUBLANES):
        _tile_rows(ysc, 0, cap, s)[...] = y_ref[:, s * LANES:(s + 1) * LANES]

    def tile_of(row):
        return pl.ds(pl.multiple_of(row * SUBLANES, SUBLANES), SUBLANES)

    def rows(g, carry):
        tokens = [tok_ref[0, g * ROW_UNROLL + k] for k in range(ROW_UNROLL)]
        new = [acc[tile_of(tk), :] + aff_ref[0, tk] * ysc[tile_of(g * ROW_UNROLL + k), :]
               for k, tk in enumerate(tokens)]
        for tk, val in zip(tokens, new):
            acc[tile_of(tk), :] = val
        return carry

    lax.fori_loop(0, cap // ROW_UNROLL, rows, 0)

    @pl.when(e == ne - 1)
    def _():
        out = acc_hbm.at[pl.ds(pl.multiple_of(b * t * SUBLANES, SUBLANES), t * SUBLANES), :]
        copy = pltpu.make_async_copy(acc, out, sem)
        copy.start()
        copy.wait()


def _combine(tok, aff, y, bsz, t, cap):
    ne = aff.shape[1]
    d = y.shape[-1]
    assert d == SUBLANES * LANES
    smem = lambda n: pl.BlockSpec((None, None, 1, n), lambda b, e: (b, e, 0, 0), memory_space=pltpu.SMEM)
    return pl.pallas_call(
        functools.partial(_combine_body, t=t, cap=cap, ne=ne),
        grid=(bsz, ne),
        in_specs=[smem(cap), smem(t),
                  pl.BlockSpec((None, cap, d), lambda b, e: (e, b, 0))],
        out_specs=pl.BlockSpec(memory_space=pl.ANY),
        out_shape=jax.ShapeDtypeStruct((bsz * t * SUBLANES, LANES), F32),
        scratch_shapes=[pltpu.VMEM((t * SUBLANES, LANES), F32),
                        pltpu.VMEM((cap * SUBLANES, LANES), F32),
                        pltpu.SemaphoreType.DMA],
        compiler_params=_params(("arbitrary", "arbitrary"), VMEM_LIMIT),
        name="combine",
    )(tok, aff, y)


def _final_body(x1_ref, acc_ref, g2_ref, gf_ref, o_ref, *, tm):
    moe = jnp.concatenate([acc_ref[pl.ds(s, tm, stride=SUBLANES), :] for s in range(SUBLANES)], axis=-1)
    x = x1_ref[...] + g2_ref[...] * moe
    ms = jnp.mean(x * x, axis=-1, keepdims=True)
    o_ref[...] = x * lax.rsqrt(ms + EPS) * gf_ref[...]


def _final(x1, acc_tiles, mods4, g_final, tm):
    bsz, t, d = x1.shape
    nt = t // tm
    return pl.pallas_call(
        functools.partial(_final_body, tm=tm),
        grid=(bsz, nt),
        in_specs=[pl.BlockSpec((None, tm, d), lambda b, i: (b, i, 0)),
                  pl.BlockSpec((tm * SUBLANES, LANES), lambda b, i: (b * nt + i, 0)),
                  _mod_spec(5, lambda b: b)(d),
                  pl.BlockSpec((1, d), lambda b, i: (0, 0))],
        out_specs=pl.BlockSpec((None, tm, d), lambda b, i: (b, i, 0)),
        out_shape=jax.ShapeDtypeStruct((bsz, t, d), F32),
        compiler_params=_params(("arbitrary", "arbitrary")),
        name="final",
    )(x1, acc_tiles, mods4, g_final)


def _layer(x, ctx, mods4, ctx_row, norm_mix_g, w_in, conv_w, conv_b, conv_ln_g, conv_ln_b, rpb, w_out,
           norm_ffn_g, w_router, w_gate, w_up, w_down):
    bsz, t, d = x.shape
    cw = conv_w.shape[1]
    n_heads = rpb.shape[0]
    aw = (w_in.shape[1] - 2 * cw) // 3
    head_dim = aw // n_heads
    rows = t // GRID_W
    ne = w_router.shape[1]
    cap = EC_FACTOR * t // ne
    row = lambda a: a.reshape(1, -1)

    w_in_bf = w_in.astype(BF16)
    u, q, k, v = _inproj(x, row(norm_mix_g), mods4, w_in_bf, cw, aw, head_dim ** -0.5, tm=512)
    kx, vx = _inproj_ctx(ctx, row(norm_mix_g), mods4, w_in_bf[:, 2 * cw + aw:], aw, ctx_row)
    y_conv = _conv(u, conv_w, conv_b, conv_ln_g, conv_ln_b, tt=256)
    assert rows >= (rpb.shape[1] + 1) // 2
    y_attn = _attn(q, k, v, kx, vx, _bias_table(rpb))
    x1, h2_tiles, aff_te = _outproj(y_conv, y_attn, x, w_out.astype(BF16), mods4, row(norm_ffn_g),
                                    w_router, tm=512)
    aff = jnp.swapaxes(aff_te, 1, 2)
    tok = _route(aff, cap).reshape(bsz, ne, 1, cap)
    y = _moe(tok, h2_tiles, w_gate, w_up, w_down, bsz, t, cap)
    acc_tiles = _combine(tok, aff.reshape(bsz, ne, 1, t), y, bsz, t, cap)
    return x1, acc_tiles


def kernel(x, c, ctx, c_ctx, w_mod, b_mod, norm_mix_g, w_in, conv_w, conv_b, conv_ln_g, conv_ln_b, rpb,
           w_out, norm_ffn_g, w_router, w_gate, w_up, w_down, norm_final_g):
    bsz, t, d = x.shape
    depth = w_mod.shape[0]
    assert depth == 1
    assert bsz + 1 <= SUBLANES
    c8 = jnp.zeros((SUBLANES, d), F32).at[:bsz].set(c).at[bsz].set(c_ctx)
    mods4 = _adaln(c8, w_mod[0], b_mod[0].reshape(1, -1)).reshape(SUBLANES, N_MOD, 1, d)
    x1, acc_tiles = _layer(x, ctx, mods4, bsz, norm_mix_g[0], w_in[0], conv_w[0], conv_b[0], conv_ln_g[0],
                           conv_ln_b[0], rpb[0], w_out[0], norm_ffn_g[0], w_router[0], w_gate[0], w_up[0],
                           w_down[0])
    return _final(x1, acc_tiles, mods4, norm_final_g.reshape(1, -1), tm=512)
```

```python
import functools

import jax
import jax.numpy as jnp
from jax import lax
from jax.experimental import pallas as pl
from jax.experimental.pallas import tpu as pltpu

F32 = jnp.float32
BF16 = jnp.bfloat16
I32 = jnp.int32
HIGHEST = lax.Precision.HIGHEST

GRID_W = 64
NA_COLS = 16
EC_FACTOR = 2
N_MOD = 6
EPS = 1e-6
NEG_INF = -1e30
LANES = 128
SUBLANES = 8
VMEM_LIMIT = 56 * 1024 * 1024

_NT = (((1,), (1,)), ((), ()))


def _params(sem, vmem=None):
    return pltpu.CompilerParams(dimension_semantics=sem, vmem_limit_bytes=vmem)


def _adaln_body(c_ref, w_ref, b_ref, o_ref):
    c = c_ref[...]
    s = c * jax.nn.sigmoid(c)
    o_ref[...] = jnp.dot(s, w_ref[...], preferred_element_type=F32, precision=HIGHEST) + b_ref[...]


def _adaln(c8, w_mod, b_mod):
    d, n = w_mod.shape
    bn = 512
    return pl.pallas_call(
        _adaln_body,
        grid=(n // bn,),
        in_specs=[pl.BlockSpec((SUBLANES, d), lambda j: (0, 0)),
                  pl.BlockSpec((d, bn), lambda j: (0, j)),
                  pl.BlockSpec((1, bn), lambda j: (0, j))],
        out_specs=pl.BlockSpec((SUBLANES, bn), lambda j: (0, j)),
        out_shape=jax.ShapeDtypeStruct((SUBLANES, n), F32),
        compiler_params=_params(("arbitrary",)),
        name="adaln",
    )(c8, w_mod, b_mod)


def _norm_mod(x, g, shift, scale):
    ms = jnp.mean(x * x, axis=-1, keepdims=True)
    h = x * lax.rsqrt(ms + EPS) * g
    return h * (1.0 + scale) + shift


def _inproj_body(x_ref, g_ref, sh_ref, sc_ref, w_ref, u_ref, q_ref, k_ref, v_ref, *, cw, aw, qscale):
    hb = _norm_mod(x_ref[...], g_ref[...], sh_ref[...], sc_ref[...]).astype(BF16)

    def proj(lo, width):
        return jnp.dot(hb, w_ref[:, lo:lo + width], preferred_element_type=F32)

    a = proj(0, cw)
    gate = proj(cw, cw)
    u_ref[...] = a * jax.nn.sigmoid(gate)
    q_ref[...] = (proj(2 * cw, aw) * qscale).astype(BF16)
    k_ref[...] = proj(2 * cw + aw, aw).astype(BF16)
    v_ref[...] = proj(2 * cw + 2 * aw, aw).astype(BF16)


def _inproj_kv_body(x_ref, g_ref, sh_ref, sc_ref, w_ref, k_ref, v_ref, *, aw):
    hb = _norm_mod(x_ref[...], g_ref[...], sh_ref[...], sc_ref[...]).astype(BF16)
    k_ref[...] = jnp.dot(hb, w_ref[:, 0:aw], preferred_element_type=F32).astype(BF16)
    v_ref[...] = jnp.dot(hb, w_ref[:, aw:2 * aw], preferred_element_type=F32).astype(BF16)


def _mod_spec(slot, row_of):
    return lambda d: pl.BlockSpec((None, None, 1, d), lambda b, i: (row_of(b), slot, 0, 0))


def _inproj(x, g, mods4, w_bf, cw, aw, qscale, tm):
    bsz, t, d = x.shape
    tok = lambda w, dt: (pl.BlockSpec((None, tm, w), lambda b, i: (b, i, 0)), jax.ShapeDtypeStruct((bsz, t, w), dt))
    outs = [tok(cw, F32), tok(aw, BF16), tok(aw, BF16), tok(aw, BF16)]
    return pl.pallas_call(
        functools.partial(_inproj_body, cw=cw, aw=aw, qscale=qscale),
        grid=(bsz, t // tm),
        in_specs=[pl.BlockSpec((None, tm, d), lambda b, i: (b, i, 0)),
                  pl.BlockSpec((1, d), lambda b, i: (0, 0)),
                  _mod_spec(0, lambda b: b)(d),
                  _mod_spec(1, lambda b: b)(d),
                  pl.BlockSpec(w_bf.shape, lambda b, i: (0, 0))],
        out_specs=[o[0] for o in outs],
        out_shape=[o[1] for o in outs],
        compiler_params=_params(("arbitrary", "arbitrary"), VMEM_LIMIT),
        name="inproj",
    )(x, g, mods4, mods4, w_bf)


def _inproj_ctx(ctx, g, mods4, w_kv_bf, aw, ctx_row):
    bsz, l, d = ctx.shape
    spec = pl.BlockSpec((None, l, aw), lambda b, i: (b, 0, 0))
    shape = jax.ShapeDtypeStruct((bsz, l, aw), BF16)
    return pl.pallas_call(
        functools.partial(_inproj_kv_body, aw=aw),
        grid=(bsz, 1),
        in_specs=[pl.BlockSpec((None, l, d), lambda b, i: (b, 0, 0)),
                  pl.BlockSpec((1, d), lambda b, i: (0, 0)),
                  _mod_spec(0, lambda b: ctx_row)(d),
                  _mod_spec(1, lambda b: ctx_row)(d),
                  pl.BlockSpec(w_kv_bf.shape, lambda b, i: (0, 0))],
        out_specs=[spec, spec],
        out_shape=[shape, shape],
        compiler_params=_params(("arbitrary", "arbitrary"), VMEM_LIMIT),
        name="inproj_ctx",
    )(ctx, g, mods4, mods4, w_kv_bf)


CONV_HALO = 16
CONV_RC = 64
CONV_CW = 256


def _conv_body(prev_ref, cur_ref, next_ref, w_ref, b_ref, lg_ref, lb_ref, o_ref, ext, ysc, *, width, tt):
    i = pl.program_id(1)
    n = pl.num_programs(1)
    ext[0:CONV_HALO, :] = jnp.where(i > 0, prev_ref[...], 0.0)
    ext[CONV_HALO:CONV_HALO + tt, :] = cur_ref[...]
    ext[CONV_HALO + tt:, :] = jnp.where(i < n - 1, next_ref[...], 0.0)
    lead = CONV_HALO - width // 2
    for rc in range(tt // CONV_RC):
        r0 = rc * CONV_RC
        for ch in range(cur_ref.shape[-1] // CONV_CW):
            cs = slice(ch * CONV_CW, (ch + 1) * CONV_CW)
            out = None
            for m in range(SUBLANES):
                nrow = CONV_RC + (SUBLANES if m else 0)
                part = None
                for j in range(width):
                    if (lead + j) % SUBLANES != m:
                        continue
                    a = r0 + lead + j - m
                    term = w_ref[j:j + 1, cs] * ext[a:a + nrow, cs]
                    part = term if part is None else part + term
                piece = part[m:m + CONV_RC]
                out = piece if out is None else out + piece
            ysc[r0:r0 + CONV_RC, cs] = out
    y = ysc[...] + b_ref[...]
    mu = jnp.mean(y, axis=-1, keepdims=True)
    yc = y - mu
    var = jnp.mean(yc * yc, axis=-1, keepdims=True)
    yn = yc * lax.rsqrt(var + EPS) * lg_ref[...] + lb_ref[...]
    o_ref[...] = (yn * jax.nn.sigmoid(yn)).astype(BF16)


def _conv(u, conv_w, conv_b, ln_g, ln_b, tt):
    bsz, t, cw = u.shape
    width = conv_w.shape[0]
    assert width // 2 < CONV_HALO and tt % CONV_HALO == 0 and tt % CONV_RC == 0 and cw % CONV_CW == 0
    hb = tt // CONV_HALO
    nh = t // CONV_HALO
    row = lambda a: a.reshape(1, cw)
    return pl.pallas_call(
        functools.partial(_conv_body, width=width, tt=tt),
        grid=(bsz, t // tt),
        in_specs=[pl.BlockSpec((None, CONV_HALO, cw), lambda b, i: (b, jnp.maximum(i * hb - 1, 0), 0)),
                  pl.BlockSpec((None, tt, cw), lambda b, i: (b, i, 0)),
                  pl.BlockSpec((None, CONV_HALO, cw), lambda b, i: (b, jnp.minimum((i + 1) * hb, nh - 1), 0)),
                  pl.BlockSpec((width, cw), lambda b, i: (0, 0)),
                  pl.BlockSpec((1, cw), lambda b, i: (0, 0)),
                  pl.BlockSpec((1, cw), lambda b, i: (0, 0)),
                  pl.BlockSpec((1, cw), lambda b, i: (0, 0))],
        out_specs=pl.BlockSpec((None, tt, cw), lambda b, i: (b, i, 0)),
        out_shape=jax.ShapeDtypeStruct((bsz, t, cw), BF16),
        scratch_shapes=[pltpu.VMEM((tt + 2 * CONV_HALO, cw), F32), pltpu.VMEM((tt, cw), F32)],
        compiler_params=_params(("arbitrary", "arbitrary")),
        name="conv",
    )(u, u, u, conv_w, row(conv_b), row(ln_g), row(ln_b))


ATT_RB = 8


def _bias_body(rpb_ref, onehot_ref, valid_ref, o_ref):
    t = jnp.dot(rpb_ref[...], onehot_ref[...], preferred_element_type=F32, precision=HIGHEST)
    o_ref[...] = jnp.where(valid_ref[...] > 0.5, t, NEG_INF)


def _bias_table(rpb):
    nh, ndr, ndc = rpb.shape
    qc = jnp.arange(GRID_W)[:, None]
    kc = jnp.arange(GRID_W)[None, :]
    c0 = jnp.clip(qc - NA_COLS // 2, 0, GRID_W - NA_COLS)
    valid = ((kc >= c0) & (kc < c0 + NA_COLS)).astype(F32).reshape(1, GRID_W * GRID_W)
    d_col = jnp.clip(kc - qc + (NA_COLS - 1), 0, 2 * NA_COLS - 2).reshape(1, GRID_W * GRID_W)
    ndc_pad = -(-ndc // SUBLANES) * SUBLANES
    onehot = (jnp.arange(ndc_pad)[:, None] == d_col).astype(F32)
    rpb2 = jnp.pad(rpb.reshape(nh * ndr, ndc), ((0, 0), (0, ndc_pad - ndc)))
    toeplitz = pl.pallas_call(
        _bias_body,
        out_shape=jax.ShapeDtypeStruct((nh * ndr, GRID_W * GRID_W), F32),
        name="bias",
    )(rpb2, onehot, valid).reshape(nh, ndr, GRID_W, GRID_W)
    return jnp.concatenate([toeplitz[:, :-1], toeplitz[:, 1:]], axis=-1)


def _attn_body(q_ref, kp_ref, kc_ref, kn_ref, vp_ref, vc_ref, vn_ref, kx_ref, vx_ref, bias_ref, o_ref,
               kwin, vwin, s_sc, p_sc, m_sc, l_sc, *, rows, kh, n_pairs):
    i = pl.program_id(1)
    blk = ATT_RB * GRID_W
    for w, (a, b) in enumerate(((kp_ref, vp_ref), (kc_ref, vc_ref), (kn_ref, vn_ref))):
        kwin[w * blk:(w + 1) * blk, :] = a[...]
        vwin[w * blk:(w + 1) * blk, :] = b[...]
    lo = lax.broadcasted_iota(I32, (GRID_W, LANES), 1) < LANES // 2
    band = kh * GRID_W

    def lane_groups(x, op):
        parts = [x[:, g * LANES:(g + 1) * LANES] for g in range(x.shape[-1] // LANES)]
        while len(parts) > 1:
            parts = [op(parts[a], parts[a + 1]) for a in range(0, len(parts) - 1, 2)] + parts[len(parts) & ~1:]
        return parts[0]

    def row_body(rr, carry):
        r = i * ATT_RB + rr
        r0 = jnp.clip(r - kh // 2, 0, rows - kh)
        delta = r - r0
        off = pl.multiple_of((r0 - (i - 1) * ATT_RB) * GRID_W, GRID_W)
        qoff = pl.multiple_of(rr * GRID_W, GRID_W)
        for p in range(n_pairs):
            cols = slice(p * LANES, (p + 1) * LANES)
            qp = q_ref[pl.ds(qoff, GRID_W), cols]
            kb = kwin[pl.ds(off, band), cols]
            kx = kx_ref[:, cols]
            for s in range(2):
                h = 2 * p + s
                qz = jnp.where(lo if s == 0 else jnp.logical_not(lo), qp, jnp.zeros_like(qp))
                bias = jnp.concatenate(
                    [bias_ref[h, 2 * g + (kh - 1) - delta] for g in range(kh // 2)], axis=-1)
                s_loc = lax.dot_general(qz, kb, _NT, preferred_element_type=F32) + bias
                s_ctx = lax.dot_general(qz, kx, _NT, preferred_element_type=F32)
                s_sc[h, :, 0:band] = s_loc
                s_sc[h, :, band:] = s_ctx
                m_sc[h] = jnp.maximum(lane_groups(s_loc, jnp.maximum), lane_groups(s_ctx, jnp.maximum))
        for h in range(2 * n_pairs):
            m = jnp.max(m_sc[h], axis=-1, keepdims=True)
            e = jnp.exp(s_sc[h] - m)
            p_sc[h] = e.astype(BF16)
            l_sc[h] = lane_groups(e, jnp.add)
        for p in range(n_pairs):
            cols = slice(p * LANES, (p + 1) * LANES)
            vb = vwin[pl.ds(off, band), cols]
            vx = vx_ref[:, cols]
            outs = []
            for s in range(2):
                h = 2 * p + s
                o = (jnp.dot(p_sc[h, :, 0:band], vb, preferred_element_type=F32)
                     + jnp.dot(p_sc[h, :, band:], vx, preferred_element_type=F32))
                outs.append(o / jnp.sum(l_sc[h], axis=-1, keepdims=True))
            o_ref[pl.ds(qoff, GRID_W), cols] = jnp.where(lo, outs[0], outs[1]).astype(BF16)
        return carry

    lax.fori_loop(0, ATT_RB, row_body, 0, unroll=2)


def _attn(q, k, v, kx, vx, bias):
    bsz, t, aw = q.shape
    l = kx.shape[1]
    rows = t // GRID_W
    kh = (bias.shape[1] + 2) // 2
    nb = rows // ATT_RB
    blk = ATT_RB * GRID_W
    assert kh == ATT_RB and rows % ATT_RB == 0 and aw % LANES == 0 and 2 * GRID_W == LANES
    cur = pl.BlockSpec((None, blk, aw), lambda b, i: (b, i, 0))
    prev = pl.BlockSpec((None, blk, aw), lambda b, i: (b, jnp.maximum(i - 1, 0), 0))
    nxt = pl.BlockSpec((None, blk, aw), lambda b, i: (b, jnp.minimum(i + 1, nb - 1), 0))
    cx = pl.BlockSpec((None, l, aw), lambda b, i: (b, 0, 0))
    nh = bias.shape[0]
    nkeys = kh * GRID_W + l
    assert l % LANES == 0 and nh == 2 * (aw // LANES)
    return pl.pallas_call(
        functools.partial(_attn_body, rows=rows, kh=kh, n_pairs=aw // LANES),
        grid=(bsz, nb),
        in_specs=[cur, prev, cur, nxt, prev, cur, nxt, cx, cx,
                  pl.BlockSpec(bias.shape, lambda b, i: (0, 0, 0, 0))],
        out_specs=cur,
        out_shape=jax.ShapeDtypeStruct((bsz, t, aw), BF16),
        scratch_shapes=[pltpu.VMEM((3 * blk, aw), BF16), pltpu.VMEM((3 * blk, aw), BF16),
                        pltpu.VMEM((nh, GRID_W, nkeys), F32), pltpu.VMEM((nh, GRID_W, nkeys), BF16),
                        pltpu.VMEM((nh, GRID_W, LANES), F32), pltpu.VMEM((nh, GRID_W, LANES), F32)],
        compiler_params=_params(("arbitrary", "arbitrary"), VMEM_LIMIT),
        name="attn",
    )(q, k, k, k, v, v, v, kx, vx, bias)


def _outproj_body(yc_ref, ya_ref, x_ref, wo_ref, g1_ref, gf_ref, sh_ref, sc_ref, wr_ref,
                  x1_ref, h2_ref, aff_ref, *, cw, tm, ne):
    acc = (jnp.dot(yc_ref[...], wo_ref[0:cw, :], preferred_element_type=F32)
           + jnp.dot(ya_ref[...], wo_ref[cw:, :], preferred_element_type=F32))
    x1 = x_ref[...] + g1_ref[...] * acc
    x1_ref[...] = x1
    h2 = _norm_mod(x1, gf_ref[...], sh_ref[...], sc_ref[...])
    for s in range(h2.shape[-1] // LANES):
        _tile_rows(h2_ref, 0, tm, s)[...] = h2[:, s * LANES:(s + 1) * LANES]
    hi = h2.astype(BF16)
    lo = (h2 - hi.astype(F32)).astype(BF16)
    part_hi = jnp.dot(hi, wr_ref[...], preferred_element_type=F32)
    part_lo = jnp.dot(lo, wr_ref[...], preferred_element_type=F32)
    logits = part_hi + pltpu.roll(part_hi, LANES - ne, axis=1) + part_lo
    logits = jnp.where(lax.broadcasted_iota(I32, logits.shape, 1) < ne, logits, NEG_INF)
    e = jnp.exp(logits - jnp.max(logits, axis=-1, keepdims=True))
    aff_ref[...] = (e / jnp.sum(e, axis=-1, keepdims=True))[:, 0:ne]


def _outproj(y_conv, y_attn, x, wo_bf, mods4, g_ffn, w_router, tm):
    bsz, t, d = x.shape
    cw = y_conv.shape[-1]
    ne = w_router.shape[1]
    assert d == SUBLANES * LANES and 2 * ne <= LANES
    nt = t // tm
    w_hi = w_router.astype(BF16)
    w_lo = (w_router - w_hi.astype(F32)).astype(BF16)
    wr_t = jnp.zeros((d, LANES), BF16).at[:, 0:ne].set(w_hi).at[:, ne:2 * ne].set(w_lo)
    return pl.pallas_call(
        functools.partial(_outproj_body, cw=cw, tm=tm, ne=ne),
        grid=(bsz, nt),
        in_specs=[pl.BlockSpec((None, tm, cw), lambda b, i: (b, i, 0)),
                  pl.BlockSpec((None, tm, y_attn.shape[-1]), lambda b, i: (b, i, 0)),
                  pl.BlockSpec((None, tm, d), lambda b, i: (b, i, 0)),
                  pl.BlockSpec(wo_bf.shape, lambda b, i: (0, 0)),
                  _mod_spec(2, lambda b: b)(d),
                  pl.BlockSpec((1, d), lambda b, i: (0, 0)),
                  _mod_spec(3, lambda b: b)(d),
                  _mod_spec(4, lambda b: b)(d),
                  pl.BlockSpec(wr_t.shape, lambda b, i: (0, 0))],
        out_specs=[pl.BlockSpec((None, tm, d), lambda b, i: (b, i, 0)),
                   pl.BlockSpec((tm * SUBLANES, LANES), lambda b, i: (b * nt + i, 0)),
                   pl.BlockSpec((None, tm, ne), lambda b, i: (b, i, 0))],
        out_shape=[jax.ShapeDtypeStruct((bsz, t, d), F32),
                   jax.ShapeDtypeStruct((bsz * t * SUBLANES, LANES), F32),
                   jax.ShapeDtypeStruct((bsz, t, ne), F32)],
        compiler_params=_params(("arbitrary", "arbitrary"), VMEM_LIMIT),
        name="outproj",
    )(y_conv, y_attn, x, wo_bf, mods4, g_ffn, mods4, mods4, wr_t)


def _route_body(affe_ref, affr_ref, u_ref, ones_ref, l_ref, tok_ref, *, cap, ne, nblk):
    aff_e = affe_ref[...]

    def search(it, bits):
        cand = bits | jnp.left_shift(jnp.int32(1), 30 - it)
        cnt = jnp.sum(jnp.where(aff_e >= pltpu.bitcast(cand, F32), 1.0, 0.0), axis=1, keepdims=True)
        return jnp.where(cnt >= cap, cand, bits)

    tau = pltpu.bitcast(lax.fori_loop(0, 31, search, jnp.zeros((ne, 1), I32)), F32)
    n_gt = jnp.sum(jnp.where(aff_e > tau, 1.0, 0.0), axis=1, keepdims=True)
    need = cap - n_gt

    def per_row(col):
        return jnp.concatenate([jnp.broadcast_to(col[e:e + 1, :], (nblk, 1)) for e in range(ne)], axis=0)

    tau_r = per_row(tau)
    need_r = per_row(need)
    kr = affr_ref[...]

    def scan(mf):
        mb = mf.astype(BF16)
        incl = jnp.dot(mb, u_ref[...], preferred_element_type=F32)
        tot = jnp.dot(mb, ones_ref[...], preferred_element_type=F32)
        off = jnp.dot(l_ref[...], tot.astype(BF16), preferred_element_type=F32)
        return incl, tot, off

    eq = jnp.where(kr == tau_r, 1.0, 0.0)
    incl_e, _, off_e = scan(eq)
    rank_eq = off_e + incl_e - eq
    sel = jnp.where((kr > tau_r) | ((eq > 0.5) & (rank_eq < need_r)), 1.0, 0.0)
    incl, _, off = scan(sel)

    nbits = LANES.bit_length() - 1
    lane = lax.broadcasted_iota(I32, kr.shape, 1)
    row = lax.broadcasted_iota(I32, kr.shape, 0)
    dist = lane - (incl.astype(I32) - 1)
    word = jnp.where(sel > 0.5, (1 << (2 * nbits)) | (dist << nbits) | lane, 0)

    def bit(x, pos):
        return (x >> pos) & 1

    for k in range(nbits):
        arriving = pltpu.roll(word, LANES - (1 << k), axis=1)
        take = bit(arriving, 2 * nbits) & bit(arriving, nbits + k)
        stay = bit(word, 2 * nbits) & (1 - bit(word, nbits + k))
        word = jnp.where(take == 1, arriving, jnp.where(stay == 1, word, 0))

    off_i = off.astype(I32)
    shift = off_i & (LANES - 1)
    for k in range(nbits):
        word = jnp.where(bit(shift, k) == 1, pltpu.roll(word, 1 << k, axis=1), word)
    valid = bit(word, 2 * nbits) == 1
    wrapped = lane < shift
    blk_bits = nblk.bit_length() - 1
    q0 = (row >> blk_bits) * (cap // LANES) + (off_i >> nbits)
    digits = (jnp.where(valid, row & (nblk - 1), 0), jnp.where(valid, word & (LANES - 1), 0))

    def place(sel_rows, q):
        onehot_t = jnp.where(q == lane, 1.0, 0.0).astype(BF16)
        outs = []
        for dgt in digits:
            x = jnp.where(sel_rows, dgt, 0).astype(F32).astype(BF16)
            outs.append(lax.dot_general(onehot_t, x, (((0,), (0,)), ((), ())), preferred_element_type=F32))
        return outs[0] * LANES + outs[1]

    tok_ref[...] = (place(jnp.logical_not(wrapped), q0) + place(wrapped, q0 + 1)).astype(I32)


def _route(aff, cap):
    bsz, ne, t = aff.shape
    nblk = t // LANES
    nr = ne * nblk
    nq = ne * cap // LANES
    assert nq == LANES and cap % LANES == 0 and nblk & (nblk - 1) == 0 and nblk <= 256
    aff_r = aff.reshape(bsz, nr, LANES)
    ar = jnp.arange(LANES)
    upper = (ar[:, None] <= ar[None, :]).astype(BF16)
    ones = jnp.ones((LANES, LANES), BF16)
    rr = jnp.arange(nr)
    lower = ((rr[:, None] // nblk == rr[None, :] // nblk) & (rr[None, :] < rr[:, None])).astype(BF16)
    const = lambda a: pl.BlockSpec(a.shape, lambda b: (0,) * a.ndim)
    return pl.pallas_call(
        functools.partial(_route_body, cap=cap, ne=ne, nblk=nblk),
        grid=(bsz,),
        in_specs=[pl.BlockSpec((None, ne, t), lambda b: (b, 0, 0)),
                  pl.BlockSpec((None, nr, LANES), lambda b: (b, 0, 0)),
                  const(upper), const(ones), const(lower)],
        out_specs=pl.BlockSpec((None, nq, LANES), lambda b: (b, 0, 0)),
        out_shape=jax.ShapeDtypeStruct((bsz, nq, LANES), I32),
        compiler_params=_params(("arbitrary",), VMEM_LIMIT),
        name="route",
    )(aff, aff_r, upper, ones, lower)


MOE_HC = 512
MOE_TM = 512
ROW_UNROLL = 8


def _tile_rows(ref, first, n, s):
    return ref.at[pl.ds(first * SUBLANES + s, n, stride=SUBLANES), :]


def _gather_copy(h2_hbm, xs3, sem, token, slot):
    src = h2_hbm.at[pl.ds(pl.multiple_of(token * SUBLANES, SUBLANES), SUBLANES), :]
    dst = xs3.at[pl.ds(pl.multiple_of(slot * SUBLANES, SUBLANES), SUBLANES), :]
    return pltpu.make_async_copy(src, dst, sem)


def _moe_body(tok_ref, nxt_ref, h2_hbm, wg_ref, wu_ref, wd_ref, y_ref, xs3, xsb, wgb, wub, wdb, sem,
              *, bsz, t, cap, ne, nch):
    e = pl.program_id(0)
    c = pl.program_id(1)
    n_rows = bsz * cap
    n_tiles = n_rows // MOE_TM
    per_tile = n_rows // (nch * n_tiles)
    n_lane_groups = xsb.shape[-1] // LANES

    def fetch(ref, b, slot):
        _gather_copy(h2_hbm, xs3, sem, b * t + ref[b, 0, slot], b * cap + slot).start()

    def wait_staging():
        pltpu.make_async_copy(h2_hbm.at[pl.ds(0, n_rows * SUBLANES), :], xs3, sem).wait()

    @pl.when((e == 0) & (c == 0))
    def _first_expert():
        for b in range(bsz):
            def rows(g, carry):
                for k in range(ROW_UNROLL):
                    fetch(tok_ref, b, g * ROW_UNROLL + k)
                return carry

            lax.fori_loop(0, cap // ROW_UNROLL, rows, 0)

    def tiles(first_chunk):
        wgb[...] = wg_ref[...].astype(BF16)
        wub[...] = wu_ref[...].astype(BF16)
        wdb[...] = wd_ref[...].astype(BF16)
        for m in range(n_tiles):
            r0 = m * MOE_TM
            if first_chunk:
                xm = jnp.concatenate([_tile_rows(xs3, r0, MOE_TM, s)[...].astype(BF16)
                                      for s in range(n_lane_groups)], axis=-1)
                xsb[r0:r0 + MOE_TM, :] = xm
            else:
                xm = xsb[r0:r0 + MOE_TM, :]
            gate = jnp.dot(xm, wgb[...], preferred_element_type=F32)
            first = (c * n_tiles + m) * per_tile
            b = first // cap
            for k in range(per_tile):
                fetch(nxt_ref, b, first - b * cap + k)
            up = jnp.dot(xm, wub[...], preferred_element_type=F32)
            hid = (gate * jax.nn.sigmoid(gate) * up).astype(BF16)
            contrib = jnp.dot(hid, wdb[...], preferred_element_type=F32)
            if first_chunk:
                y_ref[r0:r0 + MOE_TM, :] = contrib
            else:
                y_ref[r0:r0 + MOE_TM, :] += contrib

    @pl.when(c == 0)
    def _():
        wait_staging()
        tiles(True)

    @pl.when(c > 0)
    def _():
        tiles(False)

    @pl.when((e == ne - 1) & (c == nch - 1))
    def _drain():
        wait_staging()


def _moe(tok, h2_tiles, w_gate, w_up, w_down, bsz, t, cap):
    ne, d, dh = w_gate.shape
    nch = dh // MOE_HC
    n_rows = bsz * cap
    per_tile = n_rows // (nch * (n_rows // MOE_TM))
    assert cap % ROW_UNROLL == 0 and n_rows % MOE_TM == 0 and cap % per_tile == 0 and n_rows // nch <= MOE_TM
    assert d == SUBLANES * LANES
    tok_spec = lambda ahead: pl.BlockSpec((bsz, None, 1, cap),
                                          lambda e, c: (0, jnp.minimum(e + ahead, ne - 1), 0, 0),
                                          memory_space=pltpu.SMEM)
    return pl.pallas_call(
        functools.partial(_moe_body, bsz=bsz, t=t, cap=cap, ne=ne, nch=nch),
        grid=(ne, nch),
        in_specs=[tok_spec(0), tok_spec(1),
                  pl.BlockSpec(memory_space=pl.ANY),
                  pl.BlockSpec((None, d, MOE_HC), lambda e, c: (e, 0, c)),
                  pl.BlockSpec((None, d, MOE_HC), lambda e, c: (e, 0, c)),
                  pl.BlockSpec((None, MOE_HC, d), lambda e, c: (e, c, 0))],
        out_specs=pl.BlockSpec((None, n_rows, d), lambda e, c: (e, 0, 0)),
        out_shape=jax.ShapeDtypeStruct((ne, n_rows, d), F32),
        scratch_shapes=[pltpu.VMEM((n_rows * SUBLANES, LANES), F32),
                        pltpu.VMEM((n_rows, d), BF16),
                        pltpu.VMEM((d, MOE_HC), BF16),
                        pltpu.VMEM((d, MOE_HC), BF16),
                        pltpu.VMEM((MOE_HC, d), BF16),
                        pltpu.SemaphoreType.DMA],
        compiler_params=_params(("arbitrary", "arbitrary"), VMEM_LIMIT),
        name="moe",
    )(tok, tok, h2_tiles, w_gate, w_up, w_down)


def _combine_body(tok_ref, aff_ref, y_ref, x1_ref, g2_ref, gf_ref, o_ref, acc, ysc, *, cap, ne, tm):
    s = pl.program_id(1)

    @pl.when(s == 0)
    def _():
        acc[...] = jnp.zeros_like(acc)

    @pl.when(s < ne)
    def _scatter():
        for g in range(SUBLANES):
            _tile_rows(ysc, 0, cap, g)[...] = y_ref[:, g * LANES:(g + 1) * LANES]

        def tile_of(row):
            return pl.ds(pl.multiple_of(row * SUBLANES, SUBLANES), SUBLANES)

        def rows(g, carry):
            tokens = [tok_ref[0, g * ROW_UNROLL + k] for k in range(ROW_UNROLL)]
            new = [acc[tile_of(tk), :] + aff_ref[0, tk] * ysc[tile_of(g * ROW_UNROLL + k), :]
                   for k, tk in enumerate(tokens)]
            for tk, val in zip(tokens, new):
                acc[tile_of(tk), :] = val
            return carry

        lax.fori_loop(0, cap // ROW_UNROLL, rows, 0)

    @pl.when(s >= ne)
    def _finish():
        first = pl.multiple_of((s - ne) * tm, tm)
        moe = jnp.concatenate([_tile_rows(acc, first, tm, g)[...] for g in range(SUBLANES)], axis=-1)
        x = x1_ref[...] + g2_ref[...] * moe
        ms = jnp.mean(x * x, axis=-1, keepdims=True)
        o_ref[...] = x * lax.rsqrt(ms + EPS) * gf_ref[...]


def _combine(tok, aff, y, x1, mods4, g_final, cap, tm):
    bsz, t, d = x1.shape
    ne = aff.shape[1]
    nt = t // tm
    assert d == SUBLANES * LANES
    expert = lambda s: jnp.minimum(s, ne - 1)
    tile = lambda s: jnp.maximum(s - ne, 0)
    smem = lambda n: pl.BlockSpec((None, None, 1, n), lambda b, s: (b, expert(s), 0, 0), memory_space=pltpu.SMEM)
    token_tile = pl.BlockSpec((None, tm, d), lambda b, s: (b, tile(s), 0))
    return pl.pallas_call(
        functools.partial(_combine_body, cap=cap, ne=ne, tm=tm),
        grid=(bsz, ne + nt),
        in_specs=[smem(cap), smem(t),
                  pl.BlockSpec((None, cap, d), lambda b, s: (expert(s), b, 0)),
                  token_tile,
                  _mod_spec(5, lambda b: b)(d),
                  pl.BlockSpec((1, d), lambda b, s: (0, 0))],
        out_specs=token_tile,
        out_shape=jax.ShapeDtypeStruct((bsz, t, d), F32),
        scratch_shapes=[pltpu.VMEM((t * SUBLANES, LANES), F32),
                        pltpu.VMEM((cap * SUBLANES, LANES), F32)],
        compiler_params=_params(("arbitrary", "arbitrary"), VMEM_LIMIT),
        name="combine",
    )(tok, aff, y, x1, mods4, g_final)


def _layer(x, ctx, mods4, ctx_row, norm_mix_g, w_in, conv_w, conv_b, conv_ln_g, conv_ln_b, rpb, w_out,
           norm_ffn_g, w_router, w_gate, w_up, w_down, norm_final_g):
    bsz, t, d = x.shape
    cw = conv_w.shape[1]
    n_heads = rpb.shape[0]
    aw = (w_in.shape[1] - 2 * cw) // 3
    head_dim = aw // n_heads
    rows = t // GRID_W
    ne = w_router.shape[1]
    cap = EC_FACTOR * t // ne
    row = lambda a: a.reshape(1, -1)

    w_in_bf = w_in.astype(BF16)
    u, q, k, v = _inproj(x, row(norm_mix_g), mods4, w_in_bf, cw, aw, head_dim ** -0.5, tm=512)
    kx, vx = _inproj_ctx(ctx, row(norm_mix_g), mods4, w_in_bf[:, 2 * cw + aw:], aw, ctx_row)
    y_conv = _conv(u, conv_w, conv_b, conv_ln_g, conv_ln_b, tt=256)
    assert rows >= (rpb.shape[1] + 1) // 2
    y_attn = _attn(q, k, v, kx, vx, _bias_table(rpb))
    x1, h2_tiles, aff_te = _outproj(y_conv, y_attn, x, w_out.astype(BF16), mods4, row(norm_ffn_g),
                                    w_router, tm=512)
    aff = jnp.swapaxes(aff_te, 1, 2)
    tok = _route(aff, cap).reshape(bsz, ne, 1, cap)
    y = _moe(tok, h2_tiles, w_gate, w_up, w_down, bsz, t, cap)
    return _combine(tok, aff.reshape(bsz, ne, 1, t), y, x1, mods4, row(norm_final_g), cap, tm=512)


def kernel(x, c, ctx, c_ctx, w_mod, b_mod, norm_mix_g, w_in, conv_w, conv_b, conv_ln_g, conv_ln_b, rpb,
           w_out, norm_ffn_g, w_router, w_gate, w_up, w_down, norm_final_g):
    bsz, t, d = x.shape
    depth = w_mod.shape[0]
    assert depth == 1
    assert bsz + 1 <= SUBLANES
    c8 = jnp.zeros((SUBLANES, d), F32).at[:bsz].set(c).at[bsz].set(c_ctx)
    mods4 = _adaln(c8, w_mod[0], b_mod[0].reshape(1, -1)).reshape(SUBLANES, N_MOD, 1, d)
    return _layer(x, ctx, mods4, bsz, norm_mix_g[0], w_in[0], conv_w[0], conv_b[0], conv_ln_g[0],
                  conv_ln_b[0], rpb[0], w_out[0], norm_ffn_g[0], w_router[0], w_gate[0], w_up[0],
                  w_down[0], norm_final_g)
```

```python
import functools

import jax
import jax.numpy as jnp
from jax import lax
from jax.experimental import pallas as pl
from jax.experimental.pallas import tpu as pltpu

F32 = jnp.float32
BF16 = jnp.bfloat16
I32 = jnp.int32
HIGHEST = lax.Precision.HIGHEST

GRID_W = 64
NA_COLS = 16
EC_FACTOR = 2
N_MOD = 6
EPS = 1e-6
NEG_INF = -1e30
LANES = 128
SUBLANES = 8
VMEM_LIMIT = 56 * 1024 * 1024

_NT = (((1,), (1,)), ((), ()))


def _params(sem, vmem=None):
    return pltpu.CompilerParams(dimension_semantics=sem, vmem_limit_bytes=vmem)


def _adaln_body(c_ref, w_ref, b_ref, o_ref):
    c = c_ref[...]
    s = c * jax.nn.sigmoid(c)
    o_ref[...] = jnp.dot(s, w_ref[...], preferred_element_type=F32, precision=HIGHEST) + b_ref[...]


def _adaln(c8, w_mod, b_mod):
    d, n = w_mod.shape
    bn = 512
    return pl.pallas_call(
        _adaln_body,
        grid=(n // bn,),
        in_specs=[pl.BlockSpec((SUBLANES, d), lambda j: (0, 0)),
                  pl.BlockSpec((d, bn), lambda j: (0, j)),
                  pl.BlockSpec((1, bn), lambda j: (0, j))],
        out_specs=pl.BlockSpec((SUBLANES, bn), lambda j: (0, j)),
        out_shape=jax.ShapeDtypeStruct((SUBLANES, n), F32),
        compiler_params=_params(("arbitrary",)),
        name="adaln",
    )(c8, w_mod, b_mod)


def _norm_mod(x, g, shift, scale):
    ms = jnp.mean(x * x, axis=-1, keepdims=True)
    h = x * lax.rsqrt(ms + EPS) * g
    return h * (1.0 + scale) + shift


def _inproj_body(x_ref, g_ref, sh_ref, sc_ref, w_ref, u_ref, q_ref, k_ref, v_ref, *, cw, aw, qscale):
    hb = _norm_mod(x_ref[...], g_ref[...], sh_ref[...], sc_ref[...]).astype(BF16)

    def proj(lo, width):
        return jnp.dot(hb, w_ref[:, lo:lo + width], preferred_element_type=F32)

    a = proj(0, cw)
    gate = proj(cw, cw)
    u_ref[...] = a * jax.nn.sigmoid(gate)
    q_ref[...] = (proj(2 * cw, aw) * qscale).astype(BF16)
    k_ref[...] = proj(2 * cw + aw, aw).astype(BF16)
    v_ref[...] = proj(2 * cw + 2 * aw, aw).astype(BF16)


def _inproj_kv_body(x_ref, g_ref, sh_ref, sc_ref, w_ref, k_ref, v_ref, *, aw):
    hb = _norm_mod(x_ref[...], g_ref[...], sh_ref[...], sc_ref[...]).astype(BF16)
    k_ref[...] = jnp.dot(hb, w_ref[:, 0:aw], preferred_element_type=F32).astype(BF16)
    v_ref[...] = jnp.dot(hb, w_ref[:, aw:2 * aw], preferred_element_type=F32).astype(BF16)


def _mod_spec(slot, row_of):
    return lambda d: pl.BlockSpec((None, None, 1, d), lambda b, i: (row_of(b), slot, 0, 0))


def _inproj(x, g, mods4, w_bf, cw, aw, qscale, tm):
    bsz, t, d = x.shape
    tok = lambda w, dt: (pl.BlockSpec((None, tm, w), lambda b, i: (b, i, 0)), jax.ShapeDtypeStruct((bsz, t, w), dt))
    outs = [tok(cw, F32), tok(aw, BF16), tok(aw, BF16), tok(aw, BF16)]
    return pl.pallas_call(
        functools.partial(_inproj_body, cw=cw, aw=aw, qscale=qscale),
        grid=(bsz, t // tm),
        in_specs=[pl.BlockSpec((None, tm, d), lambda b, i: (b, i, 0)),
                  pl.BlockSpec((1, d), lambda b, i: (0, 0)),
                  _mod_spec(0, lambda b: b)(d),
                  _mod_spec(1, lambda b: b)(d),
                  pl.BlockSpec(w_bf.shape, lambda b, i: (0, 0))],
        out_specs=[o[0] for o in outs],
        out_shape=[o[1] for o in outs],
        compiler_params=_params(("arbitrary", "arbitrary"), VMEM_LIMIT),
        name="inproj",
    )(x, g, mods4, mods4, w_bf)


def _inproj_ctx(ctx, g, mods4, w_kv_bf, aw, ctx_row):
    bsz, l, d = ctx.shape
    spec = pl.BlockSpec((None, l, aw), lambda b, i: (b, 0, 0))
    shape = jax.ShapeDtypeStruct((bsz, l, aw), BF16)
    return pl.pallas_call(
        functools.partial(_inproj_kv_body, aw=aw),
        grid=(bsz, 1),
        in_specs=[pl.BlockSpec((None, l, d), lambda b, i: (b, 0, 0)),
                  pl.BlockSpec((1, d), lambda b, i: (0, 0)),
                  _mod_spec(0, lambda b: ctx_row)(d),
                  _mod_spec(1, lambda b: ctx_row)(d),
                  pl.BlockSpec(w_kv_bf.shape, lambda b, i: (0, 0))],
        out_specs=[spec, spec],
        out_shape=[shape, shape],
        compiler_params=_params(("arbitrary", "arbitrary"), VMEM_LIMIT),
        name="inproj_ctx",
    )(ctx, g, mods4, mods4, w_kv_bf)


CONV_HALO = 16
CONV_RC = 64
CONV_CW = 256


def _conv_body(prev_ref, cur_ref, next_ref, w_ref, b_ref, lg_ref, lb_ref, o_ref, ext, ysc, *, width, tt):
    i = pl.program_id(1)
    n = pl.num_programs(1)
    ext[0:CONV_HALO, :] = jnp.where(i > 0, prev_ref[...], 0.0)
    ext[CONV_HALO:CONV_HALO + tt, :] = cur_ref[...]
    ext[CONV_HALO + tt:, :] = jnp.where(i < n - 1, next_ref[...], 0.0)
    lead = CONV_HALO - width // 2
    for rc in range(tt // CONV_RC):
        r0 = rc * CONV_RC
        for ch in range(cur_ref.shape[-1] // CONV_CW):
            cs = slice(ch * CONV_CW, (ch + 1) * CONV_CW)
            out = None
            for m in range(SUBLANES):
                nrow = CONV_RC + (SUBLANES if m else 0)
                part = None
                for j in range(width):
                    if (lead + j) % SUBLANES != m:
                        continue
                    a = r0 + lead + j - m
                    term = w_ref[j:j + 1, cs] * ext[a:a + nrow, cs]
                    part = term if part is None else part + term
                piece = part[m:m + CONV_RC]
                out = piece if out is None else out + piece
            ysc[r0:r0 + CONV_RC, cs] = out
    y = ysc[...] + b_ref[...]
    mu = jnp.mean(y, axis=-1, keepdims=True)
    yc = y - mu
    var = jnp.mean(yc * yc, axis=-1, keepdims=True)
    yn = yc * lax.rsqrt(var + EPS) * lg_ref[...] + lb_ref[...]
    o_ref[...] = (yn * jax.nn.sigmoid(yn)).astype(BF16)


def _conv(u, conv_w, conv_b, ln_g, ln_b, tt):
    bsz, t, cw = u.shape
    width = conv_w.shape[0]
    assert width // 2 < CONV_HALO and tt % CONV_HALO == 0 and tt % CONV_RC == 0 and cw % CONV_CW == 0
    hb = tt // CONV_HALO
    nh = t // CONV_HALO
    row = lambda a: a.reshape(1, cw)
    return pl.pallas_call(
        functools.partial(_conv_body, width=width, tt=tt),
        grid=(bsz, t // tt),
        in_specs=[pl.BlockSpec((None, CONV_HALO, cw), lambda b, i: (b, jnp.maximum(i * hb - 1, 0), 0)),
                  pl.BlockSpec((None, tt, cw), lambda b, i: (b, i, 0)),
                  pl.BlockSpec((None, CONV_HALO, cw), lambda b, i: (b, jnp.minimum((i + 1) * hb, nh - 1), 0)),
                  pl.BlockSpec((width, cw), lambda b, i: (0, 0)),
                  pl.BlockSpec((1, cw), lambda b, i: (0, 0)),
                  pl.BlockSpec((1, cw), lambda b, i: (0, 0)),
                  pl.BlockSpec((1, cw), lambda b, i: (0, 0))],
        out_specs=pl.BlockSpec((None, tt, cw), lambda b, i: (b, i, 0)),
        out_shape=jax.ShapeDtypeStruct((bsz, t, cw), BF16),
        scratch_shapes=[pltpu.VMEM((tt + 2 * CONV_HALO, cw), F32), pltpu.VMEM((tt, cw), F32)],
        compiler_params=_params(("arbitrary", "arbitrary")),
        name="conv",
    )(u, u, u, conv_w, row(conv_b), row(ln_g), row(ln_b))


ATT_RB = 8


def _bias_body(rpb_ref, onehot_ref, valid_ref, o_ref):
    t = jnp.dot(rpb_ref[...], onehot_ref[...], preferred_element_type=F32, precision=HIGHEST)
    o_ref[...] = jnp.where(valid_ref[...] > 0.5, t, NEG_INF)


def _bias_table(rpb):
    nh, ndr, ndc = rpb.shape
    qc = jnp.arange(GRID_W)[:, None]
    kc = jnp.arange(GRID_W)[None, :]
    c0 = jnp.clip(qc - NA_COLS // 2, 0, GRID_W - NA_COLS)
    valid = ((kc >= c0) & (kc < c0 + NA_COLS)).astype(F32).reshape(1, GRID_W * GRID_W)
    d_col = jnp.clip(kc - qc + (NA_COLS - 1), 0, 2 * NA_COLS - 2).reshape(1, GRID_W * GRID_W)
    ndc_pad = -(-ndc // SUBLANES) * SUBLANES
    onehot = (jnp.arange(ndc_pad)[:, None] == d_col).astype(F32)
    rpb2 = jnp.pad(rpb.reshape(nh * ndr, ndc), ((0, 0), (0, ndc_pad - ndc)))
    toeplitz = pl.pallas_call(
        _bias_body,
        out_shape=jax.ShapeDtypeStruct((nh * ndr, GRID_W * GRID_W), F32),
        name="bias",
    )(rpb2, onehot, valid).reshape(nh, ndr, GRID_W, GRID_W)
    return jnp.concatenate([toeplitz[:, :-1], toeplitz[:, 1:]], axis=-1)


def _attn_body(q_ref, kp_ref, kc_ref, kn_ref, vp_ref, vc_ref, vn_ref, kx_ref, vx_ref, bias_ref, o_ref,
               kwin, vwin, s_sc, p_sc, m_sc, l_sc, *, rows, kh, n_pairs):
    i = pl.program_id(1)
    blk = ATT_RB * GRID_W
    for w, (a, b) in enumerate(((kp_ref, vp_ref), (kc_ref, vc_ref), (kn_ref, vn_ref))):
        kwin[w * blk:(w + 1) * blk, :] = a[...]
        vwin[w * blk:(w + 1) * blk, :] = b[...]
    lo = lax.broadcasted_iota(I32, (GRID_W, LANES), 1) < LANES // 2
    band = kh * GRID_W

    def lane_groups(x, op):
        parts = [x[:, g * LANES:(g + 1) * LANES] for g in range(x.shape[-1] // LANES)]
        while len(parts) > 1:
            parts = [op(parts[a], parts[a + 1]) for a in range(0, len(parts) - 1, 2)] + parts[len(parts) & ~1:]
        return parts[0]

    def row_body(rr, carry):
        r = i * ATT_RB + rr
        r0 = jnp.clip(r - kh // 2, 0, rows - kh)
        delta = r - r0
        off = pl.multiple_of((r0 - (i - 1) * ATT_RB) * GRID_W, GRID_W)
        qoff = pl.multiple_of(rr * GRID_W, GRID_W)
        for p in range(n_pairs):
            cols = slice(p * LANES, (p + 1) * LANES)
            qp = q_ref[pl.ds(qoff, GRID_W), cols]
            zeros = jnp.zeros_like(qp)
            q2 = jnp.concatenate([jnp.where(lo, qp, zeros), jnp.where(lo, zeros, qp)], axis=0)
            bias = jnp.concatenate(
                [jnp.concatenate([bias_ref[2 * p + s, 2 * g + (kh - 1) - delta] for g in range(kh // 2)], axis=-1)
                 for s in range(2)], axis=0)
            s_loc = lax.dot_general(q2, kwin[pl.ds(off, band), cols], _NT, preferred_element_type=F32) + bias
            s_ctx = lax.dot_general(q2, kx_ref[:, cols], _NT, preferred_element_type=F32)
            s_sc[p, :, 0:band] = s_loc
            s_sc[p, :, band:] = s_ctx
            m_sc[p] = jnp.maximum(lane_groups(s_loc, jnp.maximum), lane_groups(s_ctx, jnp.maximum))
        for p in range(n_pairs):
            m = jnp.max(m_sc[p], axis=-1, keepdims=True)
            e = jnp.exp(s_sc[p] - m)
            p_sc[p] = e.astype(BF16)
            l_sc[p] = lane_groups(e, jnp.add)
        for p in range(n_pairs):
            cols = slice(p * LANES, (p + 1) * LANES)
            o = (jnp.dot(p_sc[p, :, 0:band], vwin[pl.ds(off, band), cols], preferred_element_type=F32)
                 + jnp.dot(p_sc[p, :, band:], vx_ref[:, cols], preferred_element_type=F32))
            o = o / jnp.sum(l_sc[p], axis=-1, keepdims=True)
            o_ref[pl.ds(qoff, GRID_W), cols] = jnp.where(lo, o[0:GRID_W], o[GRID_W:]).astype(BF16)
        return carry

    lax.fori_loop(0, ATT_RB, row_body, 0, unroll=4)


def _attn(q, k, v, kx, vx, bias):
    bsz, t, aw = q.shape
    l = kx.shape[1]
    rows = t // GRID_W
    kh = (bias.shape[1] + 2) // 2
    nb = rows // ATT_RB
    blk = ATT_RB * GRID_W
    assert kh == ATT_RB and rows % ATT_RB == 0 and aw % LANES == 0 and 2 * GRID_W == LANES
    cur = pl.BlockSpec((None, blk, aw), lambda b, i: (b, i, 0))
    prev = pl.BlockSpec((None, blk, aw), lambda b, i: (b, jnp.maximum(i - 1, 0), 0))
    nxt = pl.BlockSpec((None, blk, aw), lambda b, i: (b, jnp.minimum(i + 1, nb - 1), 0))
    cx = pl.BlockSpec((None, l, aw), lambda b, i: (b, 0, 0))
    nh = bias.shape[0]
    nkeys = kh * GRID_W + l
    assert l % LANES == 0 and nh == 2 * (aw // LANES)
    return pl.pallas_call(
        functools.partial(_attn_body, rows=rows, kh=kh, n_pairs=aw // LANES),
        grid=(bsz, nb),
        in_specs=[cur, prev, cur, nxt, prev, cur, nxt, cx, cx,
                  pl.BlockSpec(bias.shape, lambda b, i: (0, 0, 0, 0))],
        out_specs=cur,
        out_shape=jax.ShapeDtypeStruct((bsz, t, aw), BF16),
        scratch_shapes=[pltpu.VMEM((3 * blk, aw), BF16), pltpu.VMEM((3 * blk, aw), BF16),
                        pltpu.VMEM((nh // 2, 2 * GRID_W, nkeys), F32), pltpu.VMEM((nh // 2, 2 * GRID_W, nkeys), BF16),
                        pltpu.VMEM((nh // 2, 2 * GRID_W, LANES), F32), pltpu.VMEM((nh // 2, 2 * GRID_W, LANES), F32)],
        compiler_params=_params(("arbitrary", "arbitrary"), VMEM_LIMIT),
        name="attn",
    )(q, k, k, k, v, v, v, kx, vx, bias)


def _outproj_body(yc_ref, ya_ref, x_ref, wo_ref, g1_ref, gf_ref, sh_ref, sc_ref, wr_ref,
                  x1_ref, h2_ref, aff_ref, *, cw, tm, ne):
    acc = (jnp.dot(yc_ref[...], wo_ref[0:cw, :], preferred_element_type=F32)
           + jnp.dot(ya_ref[...], wo_ref[cw:, :], preferred_element_type=F32))
    x1 = x_ref[...] + g1_ref[...] * acc
    x1_ref[...] = x1
    h2 = _norm_mod(x1, gf_ref[...], sh_ref[...], sc_ref[...])
    for s in range(h2.shape[-1] // LANES):
        _tile_rows(h2_ref, 0, tm, s)[...] = h2[:, s * LANES:(s + 1) * LANES]
    hi = h2.astype(BF16)
    lo = (h2 - hi.astype(F32)).astype(BF16)
    part_hi = jnp.dot(hi, wr_ref[...], preferred_element_type=F32)
    part_lo = jnp.dot(lo, wr_ref[...], preferred_element_type=F32)
    logits = part_hi + pltpu.roll(part_hi, LANES - ne, axis=1) + part_lo
    logits = jnp.where(lax.broadcasted_iota(I32, logits.shape, 1) < ne, logits, NEG_INF)
    e = jnp.exp(logits - jnp.max(logits, axis=-1, keepdims=True))
    aff_ref[...] = (e / jnp.sum(e, axis=-1, keepdims=True))[:, 0:ne]


def _outproj(y_conv, y_attn, x, wo_bf, mods4, g_ffn, w_router, tm):
    bsz, t, d = x.shape
    cw = y_conv.shape[-1]
    ne = w_router.shape[1]
    assert d == SUBLANES * LANES and 2 * ne <= LANES
    nt = t // tm
    w_hi = w_router.astype(BF16)
    w_lo = (w_router - w_hi.astype(F32)).astype(BF16)
    wr_t = jnp.zeros((d, LANES), BF16).at[:, 0:ne].set(w_hi).at[:, ne:2 * ne].set(w_lo)
    return pl.pallas_call(
        functools.partial(_outproj_body, cw=cw, tm=tm, ne=ne),
        grid=(bsz, nt),
        in_specs=[pl.BlockSpec((None, tm, cw), lambda b, i: (b, i, 0)),
                  pl.BlockSpec((None, tm, y_attn.shape[-1]), lambda b, i: (b, i, 0)),
                  pl.BlockSpec((None, tm, d), lambda b, i: (b, i, 0)),
                  pl.BlockSpec(wo_bf.shape, lambda b, i: (0, 0)),
                  _mod_spec(2, lambda b: b)(d),
                  pl.BlockSpec((1, d), lambda b, i: (0, 0)),
                  _mod_spec(3, lambda b: b)(d),
                  _mod_spec(4, lambda b: b)(d),
                  pl.BlockSpec(wr_t.shape, lambda b, i: (0, 0))],
        out_specs=[pl.BlockSpec((None, tm, d), lambda b, i: (b, i, 0)),
                   pl.BlockSpec((tm * SUBLANES, LANES), lambda b, i: (b * nt + i, 0)),
                   pl.BlockSpec((None, tm, ne), lambda b, i: (b, i, 0))],
        out_shape=[jax.ShapeDtypeStruct((bsz, t, d), F32),
                   jax.ShapeDtypeStruct((bsz * t * SUBLANES, LANES), F32),
                   jax.ShapeDtypeStruct((bsz, t, ne), F32)],
        compiler_params=_params(("arbitrary", "arbitrary"), VMEM_LIMIT),
        name="outproj",
    )(y_conv, y_attn, x, wo_bf, mods4, g_ffn, mods4, mods4, wr_t)


def _route_body(affe_ref, affr_ref, u_ref, ones_ref, l_ref, tok_ref, *, cap, ne, nblk):
    aff_e = affe_ref[...]

    def search(it, bits):
        cand = bits | jnp.left_shift(jnp.int32(1), 30 - it)
        cnt = jnp.sum(jnp.where(aff_e >= pltpu.bitcast(cand, F32), 1.0, 0.0), axis=1, keepdims=True)
        return jnp.where(cnt >= cap, cand, bits)

    tau = pltpu.bitcast(lax.fori_loop(0, 31, search, jnp.zeros((ne, 1), I32)), F32)
    n_gt = jnp.sum(jnp.where(aff_e > tau, 1.0, 0.0), axis=1, keepdims=True)
    need = cap - n_gt

    def per_row(col):
        return jnp.concatenate([jnp.broadcast_to(col[e:e + 1, :], (nblk, 1)) for e in range(ne)], axis=0)

    tau_r = per_row(tau)
    need_r = per_row(need)
    kr = affr_ref[...]

    def scan(mf):
        mb = mf.astype(BF16)
        incl = jnp.dot(mb, u_ref[...], preferred_element_type=F32)
        tot = jnp.dot(mb, ones_ref[...], preferred_element_type=F32)
        off = jnp.dot(l_ref[...], tot.astype(BF16), preferred_element_type=F32)
        return incl, tot, off

    eq = jnp.where(kr == tau_r, 1.0, 0.0)
    incl_e, _, off_e = scan(eq)
    rank_eq = off_e + incl_e - eq
    sel = jnp.where((kr > tau_r) | ((eq > 0.5) & (rank_eq < need_r)), 1.0, 0.0)
    incl, _, off = scan(sel)

    nbits = LANES.bit_length() - 1
    lane = lax.broadcasted_iota(I32, kr.shape, 1)
    row = lax.broadcasted_iota(I32, kr.shape, 0)
    dist = lane - (incl.astype(I32) - 1)
    word = jnp.where(sel > 0.5, (1 << (2 * nbits)) | (dist << nbits) | lane, 0)

    def bit(x, pos):
        return (x >> pos) & 1

    for k in range(nbits):
        arriving = pltpu.roll(word, LANES - (1 << k), axis=1)
        take = bit(arriving, 2 * nbits) & bit(arriving, nbits + k)
        stay = bit(word, 2 * nbits) & (1 - bit(word, nbits + k))
        word = jnp.where(take == 1, arriving, jnp.where(stay == 1, word, 0))

    off_i = off.astype(I32)
    shift = off_i & (LANES - 1)
    for k in range(nbits):
        word = jnp.where(bit(shift, k) == 1, pltpu.roll(word, 1 << k, axis=1), word)
    valid = bit(word, 2 * nbits) == 1
    wrapped = lane < shift
    blk_bits = nblk.bit_length() - 1
    q0 = (row >> blk_bits) * (cap // LANES) + (off_i >> nbits)
    digits = (jnp.where(valid, row & (nblk - 1), 0), jnp.where(valid, word & (LANES - 1), 0))

    def place(sel_rows, q):
        onehot_t = jnp.where(q == lane, 1.0, 0.0).astype(BF16)
        outs = []
        for dgt in digits:
            x = jnp.where(sel_rows, dgt, 0).astype(F32).astype(BF16)
            outs.append(lax.dot_general(onehot_t, x, (((0,), (0,)), ((), ())), preferred_element_type=F32))
        return outs[0] * LANES + outs[1]

    tok_ref[...] = (place(jnp.logical_not(wrapped), q0) + place(wrapped, q0 + 1)).astype(I32)


def _route(aff, cap):
    bsz, ne, t = aff.shape
    nblk = t // LANES
    nr = ne * nblk
    nq = ne * cap // LANES
    assert nq == LANES and cap % LANES == 0 and nblk & (nblk - 1) == 0 and nblk <= 256
    aff_r = aff.reshape(bsz, nr, LANES)
    ar = jnp.arange(LANES)
    upper = (ar[:, None] <= ar[None, :]).astype(BF16)
    ones = jnp.ones((LANES, LANES), BF16)
    rr = jnp.arange(nr)
    lower = ((rr[:, None] // nblk == rr[None, :] // nblk) & (rr[None, :] < rr[:, None])).astype(BF16)
    const = lambda a: pl.BlockSpec(a.shape, lambda b: (0,) * a.ndim)
    return pl.pallas_call(
        functools.partial(_route_body, cap=cap, ne=ne, nblk=nblk),
        grid=(bsz,),
        in_specs=[pl.BlockSpec((None, ne, t), lambda b: (b, 0, 0)),
                  pl.BlockSpec((None, nr, LANES), lambda b: (b, 0, 0)),
                  const(upper), const(ones), const(lower)],
        out_specs=pl.BlockSpec((None, nq, LANES), lambda b: (b, 0, 0)),
        out_shape=jax.ShapeDtypeStruct((bsz, nq, LANES), I32),
        compiler_params=_params(("arbitrary",), VMEM_LIMIT),
        name="route",
    )(aff, aff_r, upper, ones, lower)


MOE_HC = 512
MOE_TM = 512
ROW_UNROLL = 8


def _tile_rows(ref, first, n, s):
    return ref.at[pl.ds(first * SUBLANES + s, n, stride=SUBLANES), :]


def _gather_copy(h2_hbm, xs3, sem, token, slot):
    src = h2_hbm.at[pl.ds(pl.multiple_of(token * SUBLANES, SUBLANES), SUBLANES), :]
    dst = xs3.at[pl.ds(pl.multiple_of(slot * SUBLANES, SUBLANES), SUBLANES), :]
    return pltpu.make_async_copy(src, dst, sem)


def _moe_body(tok_ref, nxt_ref, h2_hbm, wg_ref, wu_ref, wd_ref, y_ref, xs3, xsb, wgb, wub, wdb, sem,
              *, bsz, t, cap, ne, nch):
    e = pl.program_id(0)
    c = pl.program_id(1)
    n_rows = bsz * cap
    n_tiles = n_rows // MOE_TM
    per_tile = n_rows // (nch * n_tiles)
    n_lane_groups = xsb.shape[-1] // LANES

    def fetch(ref, b, slot):
        _gather_copy(h2_hbm, xs3, sem, b * t + ref[b, 0, slot], b * cap + slot).start()

    def wait_staging():
        pltpu.make_async_copy(h2_hbm.at[pl.ds(0, n_rows * SUBLANES), :], xs3, sem).wait()

    @pl.when((e == 0) & (c == 0))
    def _first_expert():
        for b in range(bsz):
            def rows(g, carry):
                for k in range(ROW_UNROLL):
                    fetch(tok_ref, b, g * ROW_UNROLL + k)
                return carry

            lax.fori_loop(0, cap // ROW_UNROLL, rows, 0)

    def tiles(first_chunk):
        wgb[...] = wg_ref[...].astype(BF16)
        wub[...] = wu_ref[...].astype(BF16)
        wdb[...] = wd_ref[...].astype(BF16)
        for m in range(n_tiles):
            r0 = m * MOE_TM
            if first_chunk:
                xm = jnp.concatenate([_tile_rows(xs3, r0, MOE_TM, s)[...].astype(BF16)
                                      for s in range(n_lane_groups)], axis=-1)
                xsb[r0:r0 + MOE_TM, :] = xm
            else:
                xm = xsb[r0:r0 + MOE_TM, :]
            gate = jnp.dot(xm, wgb[...], preferred_element_type=F32)
            first = (c * n_tiles + m) * per_tile
            b = first // cap
            for k in range(per_tile):
                fetch(nxt_ref, b, first - b * cap + k)
            up = jnp.dot(xm, wub[...], preferred_element_type=F32)
            hid = (gate * jax.nn.sigmoid(gate) * up).astype(BF16)
            contrib = jnp.dot(hid, wdb[...], preferred_element_type=F32)
            if first_chunk:
                y_ref[r0:r0 + MOE_TM, :] = contrib
            else:
                y_ref[r0:r0 + MOE_TM, :] += contrib

    @pl.when(c == 0)
    def _():
        wait_staging()
        tiles(True)

    @pl.when(c > 0)
    def _():
        tiles(False)

    @pl.when((e == ne - 1) & (c == nch - 1))
    def _drain():
        wait_staging()


def _moe(tok, h2_tiles, w_gate, w_up, w_down, bsz, t, cap):
    ne, d, dh = w_gate.shape
    nch = dh // MOE_HC
    n_rows = bsz * cap
    per_tile = n_rows // (nch * (n_rows // MOE_TM))
    assert cap % ROW_UNROLL == 0 and n_rows % MOE_TM == 0 and cap % per_tile == 0 and n_rows // nch <= MOE_TM
    assert d == SUBLANES * LANES
    tok_spec = lambda ahead: pl.BlockSpec((bsz, None, 1, cap),
                                          lambda e, c: (0, jnp.minimum(e + ahead, ne - 1), 0, 0),
                                          memory_space=pltpu.SMEM)
    return pl.pallas_call(
        functools.partial(_moe_body, bsz=bsz, t=t, cap=cap, ne=ne, nch=nch),
        grid=(ne, nch),
        in_specs=[tok_spec(0), tok_spec(1),
                  pl.BlockSpec(memory_space=pl.ANY),
                  pl.BlockSpec((None, d, MOE_HC), lambda e, c: (e, 0, c)),
                  pl.BlockSpec((None, d, MOE_HC), lambda e, c: (e, 0, c)),
                  pl.BlockSpec((None, MOE_HC, d), lambda e, c: (e, c, 0))],
        out_specs=pl.BlockSpec((None, n_rows, d), lambda e, c: (e, 0, 0)),
        out_shape=jax.ShapeDtypeStruct((ne, n_rows, d), F32),
        scratch_shapes=[pltpu.VMEM((n_rows * SUBLANES, LANES), F32),
                        pltpu.VMEM((n_rows, d), BF16),
                        pltpu.VMEM((d, MOE_HC), BF16),
                        pltpu.VMEM((d, MOE_HC), BF16),
                        pltpu.VMEM((MOE_HC, d), BF16),
                        pltpu.SemaphoreType.DMA],
        compiler_params=_params(("arbitrary", "arbitrary"), VMEM_LIMIT),
        name="moe",
    )(tok, tok, h2_tiles, w_gate, w_up, w_down)


def _combine_body(tok_ref, aff_ref, y_ref, x1_ref, g2_ref, gf_ref, o_ref, acc, ysc, *, cap, ne, tm):
    s = pl.program_id(1)

    @pl.when(s == 0)
    def _():
        acc[...] = jnp.zeros_like(acc)

    @pl.when(s < ne)
    def _scatter():
        for g in range(SUBLANES):
            _tile_rows(ysc, 0, cap, g)[...] = y_ref[:, g * LANES:(g + 1) * LANES]

        def tile_of(row):
            return pl.ds(pl.multiple_of(row * SUBLANES, SUBLANES), SUBLANES)

        def rows(g, carry):
            tokens = [tok_ref[0, g * ROW_UNROLL + k] for k in range(ROW_UNROLL)]
            new = [acc[tile_of(tk), :] + aff_ref[0, tk] * ysc[tile_of(g * ROW_UNROLL + k), :]
                   for k, tk in enumerate(tokens)]
            for tk, val in zip(tokens, new):
                acc[tile_of(tk), :] = val
            return carry

        lax.fori_loop(0, cap // ROW_UNROLL, rows, 0)

    @pl.when(s >= ne)
    def _finish():
        first = pl.multiple_of((s - ne) * tm, tm)
        moe = jnp.concatenate([_tile_rows(acc, first, tm, g)[...] for g in range(SUBLANES)], axis=-1)
        x = x1_ref[...] + g2_ref[...] * moe
        ms = jnp.mean(x * x, axis=-1, keepdims=True)
        o_ref[...] = x * lax.rsqrt(ms + EPS) * gf_ref[...]


def _combine(tok, aff, y, x1, mods4, g_final, cap, tm):
    bsz, t, d = x1.shape
    ne = aff.shape[1]
    nt = t // tm
    assert d == SUBLANES * LANES
    expert = lambda s: jnp.minimum(s, ne - 1)
    tile = lambda s: jnp.maximum(s - ne, 0)
    smem = lambda n: pl.BlockSpec((None, None, 1, n), lambda b, s: (b, expert(s), 0, 0), memory_space=pltpu.SMEM)
    token_tile = pl.BlockSpec((None, tm, d), lambda b, s: (b, tile(s), 0))
    return pl.pallas_call(
        functools.partial(_combine_body, cap=cap, ne=ne, tm=tm),
        grid=(bsz, ne + nt),
        in_specs=[smem(cap), smem(t),
                  pl.BlockSpec((None, cap, d), lambda b, s: (expert(s), b, 0)),
                  token_tile,
                  _mod_spec(5, lambda b: b)(d),
                  pl.BlockSpec((1, d), lambda b, s: (0, 0))],
        out_specs=token_tile,
        out_shape=jax.ShapeDtypeStruct((bsz, t, d), F32),
        scratch_shapes=[pltpu.VMEM((t * SUBLANES, LANES), F32),
                        pltpu.VMEM((cap * SUBLANES, LANES), F32)],
        compiler_params=_params(("arbitrary", "arbitrary"), VMEM_LIMIT),
        name="combine",
    )(tok, aff, y, x1, mods4, g_final)


def _layer(x, ctx, mods4, ctx_row, norm_mix_g, w_in, conv_w, conv_b, conv_ln_g, conv_ln_b, rpb, w_out,
           norm_ffn_g, w_router, w_gate, w_up, w_down, norm_final_g):
    bsz, t, d = x.shape
    cw = conv_w.shape[1]
    n_heads = rpb.shape[0]
    aw = (w_in.shape[1] - 2 * cw) // 3
    head_dim = aw // n_heads
    rows = t // GRID_W
    ne = w_router.shape[1]
    cap = EC_FACTOR * t // ne
    row = lambda a: a.reshape(1, -1)

    w_in_bf = w_in.astype(BF16)
    u, q, k, v = _inproj(x, row(norm_mix_g), mods4, w_in_bf, cw, aw, head_dim ** -0.5, tm=512)
    kx, vx = _inproj_ctx(ctx, row(norm_mix_g), mods4, w_in_bf[:, 2 * cw + aw:], aw, ctx_row)
    y_conv = _conv(u, conv_w, conv_b, conv_ln_g, conv_ln_b, tt=256)
    assert rows >= (rpb.shape[1] + 1) // 2
    y_attn = _attn(q, k, v, kx, vx, _bias_table(rpb))
    x1, h2_tiles, aff_te = _outproj(y_conv, y_attn, x, w_out.astype(BF16), mods4, row(norm_ffn_g),
                                    w_router, tm=512)
    aff = jnp.swapaxes(aff_te, 1, 2)
    tok = _route(aff, cap).reshape(bsz, ne, 1, cap)
    y = _moe(tok, h2_tiles, w_gate, w_up, w_down, bsz, t, cap)
    return _combine(tok, aff.reshape(bsz, ne, 1, t), y, x1, mods4, row(norm_final_g), cap, tm=512)


def kernel(x, c, ctx, c_ctx, w_mod, b_mod, norm_mix_g, w_in, conv_w, conv_b, conv_ln_g, conv_ln_b, rpb,
           w_out, norm_ffn_g, w_router, w_gate, w_up, w_down, norm_final_g):
    bsz, t, d = x.shape
    depth = w_mod.shape[0]
    assert depth == 1
    assert bsz + 1 <= SUBLANES
    c8 = jnp.zeros((SUBLANES, d), F32).at[:bsz].set(c).at[bsz].set(c_ctx)
    mods4 = _adaln(c8, w_mod[0], b_mod[0].reshape(1, -1)).reshape(SUBLANES, N_MOD, 1, d)
    return _layer(x, ctx, mods4, bsz, norm_mix_g[0], w_in[0], conv_w[0], conv_b[0], conv_ln_g[0],
                  conv_ln_b[0], rpb[0], w_out[0], norm_ffn_g[0], w_router[0], w_gate[0], w_up[0],
                  w_down[0], norm_final_g)
```

```python
import functools

import jax
import jax.numpy as jnp
from jax import lax
from jax.experimental import pallas as pl
from jax.experimental.pallas import tpu as pltpu

F32 = jnp.float32
BF16 = jnp.bfloat16
I32 = jnp.int32
HIGHEST = lax.Precision.HIGHEST

GRID_W = 64
NA_COLS = 16
EC_FACTOR = 2
N_MOD = 6
EPS = 1e-6
NEG_INF = -1e30
LANES = 128
SUBLANES = 8
VMEM_LIMIT = 56 * 1024 * 1024

_NT = (((1,), (1,)), ((), ()))


def _params(sem, vmem=None):
    return pltpu.CompilerParams(dimension_semantics=sem, vmem_limit_bytes=vmem)


def _adaln_body(c_ref, w_ref, b_ref, o_ref):
    c = c_ref[...]
    s = c * jax.nn.sigmoid(c)
    o_ref[...] = jnp.dot(s, w_ref[...], preferred_element_type=F32, precision=HIGHEST) + b_ref[...]


def _adaln(c8, w_mod, b_mod):
    d, n = w_mod.shape
    bn = 512
    return pl.pallas_call(
        _adaln_body,
        grid=(n // bn,),
        in_specs=[pl.BlockSpec((SUBLANES, d), lambda j: (0, 0)),
                  pl.BlockSpec((d, bn), lambda j: (0, j)),
                  pl.BlockSpec((1, bn), lambda j: (0, j))],
        out_specs=pl.BlockSpec((SUBLANES, bn), lambda j: (0, j)),
        out_shape=jax.ShapeDtypeStruct((SUBLANES, n), F32),
        compiler_params=_params(("arbitrary",)),
        name="adaln",
    )(c8, w_mod, b_mod)


def _norm_mod(x, g, shift, scale):
    ms = jnp.mean(x * x, axis=-1, keepdims=True)
    h = x * lax.rsqrt(ms + EPS) * g
    return h * (1.0 + scale) + shift


def _inproj_body(x_ref, g_ref, sh_ref, sc_ref, w_ref, u_ref, q_ref, k_ref, v_ref, *, cw, aw, qscale):
    hb = _norm_mod(x_ref[...], g_ref[...], sh_ref[...], sc_ref[...]).astype(BF16)

    def proj(lo, width):
        return jnp.dot(hb, w_ref[:, lo:lo + width], preferred_element_type=F32)

    a = proj(0, cw)
    gate = proj(cw, cw)
    u_ref[...] = a * jax.nn.sigmoid(gate)
    q_ref[...] = (proj(2 * cw, aw) * qscale).astype(BF16)
    k_ref[...] = proj(2 * cw + aw, aw).astype(BF16)
    v_ref[...] = proj(2 * cw + 2 * aw, aw).astype(BF16)


def _inproj_kv_body(x_ref, g_ref, sh_ref, sc_ref, wk_ref, wv_ref, k_ref, v_ref):
    hb = _norm_mod(x_ref[...], g_ref[...], sh_ref[...], sc_ref[...]).astype(BF16)
    k_ref[...] = jnp.dot(hb, wk_ref[...], preferred_element_type=F32).astype(BF16)
    v_ref[...] = jnp.dot(hb, wv_ref[...], preferred_element_type=F32).astype(BF16)


def _mod_spec(slot, row_of):
    return lambda d: pl.BlockSpec((None, None, 1, d), lambda b, i: (row_of(b), slot, 0, 0))


def _inproj(x, g, mods4, w_bf, cw, aw, qscale, tm):
    bsz, t, d = x.shape
    tok = lambda w, dt: (pl.BlockSpec((None, tm, w), lambda b, i: (b, i, 0)), jax.ShapeDtypeStruct((bsz, t, w), dt))
    outs = [tok(cw, F32), tok(aw, BF16), tok(aw, BF16), tok(aw, BF16)]
    return pl.pallas_call(
        functools.partial(_inproj_body, cw=cw, aw=aw, qscale=qscale),
        grid=(bsz, t // tm),
        in_specs=[pl.BlockSpec((None, tm, d), lambda b, i: (b, i, 0)),
                  pl.BlockSpec((1, d), lambda b, i: (0, 0)),
                  _mod_spec(0, lambda b: b)(d),
                  _mod_spec(1, lambda b: b)(d),
                  pl.BlockSpec(w_bf.shape, lambda b, i: (0, 0))],
        out_specs=[o[0] for o in outs],
        out_shape=[o[1] for o in outs],
        compiler_params=_params(("arbitrary", "arbitrary"), VMEM_LIMIT),
        name="inproj",
    )(x, g, mods4, mods4, w_bf)


def _inproj_ctx(ctx, g, mods4, w_bf, k_lo, aw, ctx_row):
    bsz, l, d = ctx.shape
    assert k_lo % aw == 0
    spec = pl.BlockSpec((None, l, aw), lambda b, i: (b, 0, 0))
    shape = jax.ShapeDtypeStruct((bsz, l, aw), BF16)
    w_cols = lambda j: pl.BlockSpec((d, aw), lambda b, i: (0, k_lo // aw + j))
    return pl.pallas_call(
        _inproj_kv_body,
        grid=(bsz, 1),
        in_specs=[pl.BlockSpec((None, l, d), lambda b, i: (b, 0, 0)),
                  pl.BlockSpec((1, d), lambda b, i: (0, 0)),
                  _mod_spec(0, lambda b: ctx_row)(d),
                  _mod_spec(1, lambda b: ctx_row)(d),
                  w_cols(0), w_cols(1)],
        out_specs=[spec, spec],
        out_shape=[shape, shape],
        compiler_params=_params(("arbitrary", "arbitrary"), VMEM_LIMIT),
        name="inproj_ctx",
    )(ctx, g, mods4, mods4, w_bf, w_bf)


CONV_HALO = 16
CONV_RC = 64
CONV_CW = 256


def _conv_body(prev_ref, cur_ref, next_ref, w_ref, b_ref, lg_ref, lb_ref, o_ref, ext, ysc, *, width, tt):
    i = pl.program_id(1)
    n = pl.num_programs(1)
    ext[0:CONV_HALO, :] = jnp.where(i > 0, prev_ref[...], 0.0)
    ext[CONV_HALO:CONV_HALO + tt, :] = cur_ref[...]
    ext[CONV_HALO + tt:, :] = jnp.where(i < n - 1, next_ref[...], 0.0)
    lead = CONV_HALO - width // 2
    for rc in range(tt // CONV_RC):
        r0 = rc * CONV_RC
        for ch in range(cur_ref.shape[-1] // CONV_CW):
            cs = slice(ch * CONV_CW, (ch + 1) * CONV_CW)
            out = None
            for m in range(SUBLANES):
                nrow = CONV_RC + (SUBLANES if m else 0)
                part = None
                for j in range(width):
                    if (lead + j) % SUBLANES != m:
                        continue
                    a = r0 + lead + j - m
                    term = w_ref[j:j + 1, cs] * ext[a:a + nrow, cs]
                    part = term if part is None else part + term
                piece = part[m:m + CONV_RC]
                out = piece if out is None else out + piece
            ysc[r0:r0 + CONV_RC, cs] = out
    y = ysc[...] + b_ref[...]
    mu = jnp.mean(y, axis=-1, keepdims=True)
    yc = y - mu
    var = jnp.mean(yc * yc, axis=-1, keepdims=True)
    yn = yc * lax.rsqrt(var + EPS) * lg_ref[...] + lb_ref[...]
    o_ref[...] = (yn * jax.nn.sigmoid(yn)).astype(BF16)


def _conv(u, conv_w, conv_b, ln_g, ln_b, tt):
    bsz, t, cw = u.shape
    width = conv_w.shape[0]
    assert width // 2 < CONV_HALO and tt % CONV_HALO == 0 and tt % CONV_RC == 0 and cw % CONV_CW == 0
    hb = tt // CONV_HALO
    nh = t // CONV_HALO
    row = lambda a: a.reshape(1, cw)
    return pl.pallas_call(
        functools.partial(_conv_body, width=width, tt=tt),
        grid=(bsz, t // tt),
        in_specs=[pl.BlockSpec((None, CONV_HALO, cw), lambda b, i: (b, jnp.maximum(i * hb - 1, 0), 0)),
                  pl.BlockSpec((None, tt, cw), lambda b, i: (b, i, 0)),
                  pl.BlockSpec((None, CONV_HALO, cw), lambda b, i: (b, jnp.minimum((i + 1) * hb, nh - 1), 0)),
                  pl.BlockSpec((width, cw), lambda b, i: (0, 0)),
                  pl.BlockSpec((1, cw), lambda b, i: (0, 0)),
                  pl.BlockSpec((1, cw), lambda b, i: (0, 0)),
                  pl.BlockSpec((1, cw), lambda b, i: (0, 0))],
        out_specs=pl.BlockSpec((None, tt, cw), lambda b, i: (b, i, 0)),
        out_shape=jax.ShapeDtypeStruct((bsz, t, cw), BF16),
        scratch_shapes=[pltpu.VMEM((tt + 2 * CONV_HALO, cw), F32), pltpu.VMEM((tt, cw), F32)],
        compiler_params=_params(("arbitrary", "arbitrary")),
        name="conv",
    )(u, u, u, conv_w, row(conv_b), row(ln_g), row(ln_b))


ATT_RB = 8


def _bias_body(rpb_ref, onehot_ref, valid_ref, o_ref):
    t = jnp.dot(rpb_ref[...], onehot_ref[...], preferred_element_type=F32, precision=HIGHEST)
    o_ref[...] = jnp.where(valid_ref[...] > 0.5, t, NEG_INF)


def _bias_table(rpb):
    nh, ndr, ndc = rpb.shape
    qc = jnp.arange(GRID_W)[:, None]
    kc = jnp.arange(GRID_W)[None, :]
    c0 = jnp.clip(qc - NA_COLS // 2, 0, GRID_W - NA_COLS)
    valid = ((kc >= c0) & (kc < c0 + NA_COLS)).astype(F32).reshape(1, GRID_W * GRID_W)
    d_col = jnp.clip(kc - qc + (NA_COLS - 1), 0, 2 * NA_COLS - 2).reshape(1, GRID_W * GRID_W)
    ndc_pad = -(-ndc // SUBLANES) * SUBLANES
    onehot = (jnp.arange(ndc_pad)[:, None] == d_col).astype(F32)
    rpb2 = jnp.pad(rpb.reshape(nh * ndr, ndc), ((0, 0), (0, ndc_pad - ndc)))
    toeplitz = pl.pallas_call(
        _bias_body,
        out_shape=jax.ShapeDtypeStruct((nh * ndr, GRID_W * GRID_W), F32),
        name="bias",
    )(rpb2, onehot, valid).reshape(nh, ndr, GRID_W, GRID_W)
    return jnp.concatenate([toeplitz[:, :-1], toeplitz[:, 1:]], axis=-1)


def _attn_body(q_ref, kp_ref, kc_ref, kn_ref, vp_ref, vc_ref, vn_ref, kx_ref, vx_ref, bias_ref, o_ref,
               kwin, vwin, s_sc, p_sc, m_sc, l_sc, *, rows, kh, n_pairs):
    i = pl.program_id(1)
    blk = ATT_RB * GRID_W
    for w, (a, b) in enumerate(((kp_ref, vp_ref), (kc_ref, vc_ref), (kn_ref, vn_ref))):
        kwin[w * blk:(w + 1) * blk, :] = a[...]
        vwin[w * blk:(w + 1) * blk, :] = b[...]
    lo = lax.broadcasted_iota(I32, (GRID_W, LANES), 1) < LANES // 2
    band = kh * GRID_W

    def lane_groups(x, op):
        parts = [x[:, g * LANES:(g + 1) * LANES] for g in range(x.shape[-1] // LANES)]
        while len(parts) > 1:
            parts = [op(parts[a], parts[a + 1]) for a in range(0, len(parts) - 1, 2)] + parts[len(parts) & ~1:]
        return parts[0]

    def row_body(rr, carry):
        r = i * ATT_RB + rr
        r0 = jnp.clip(r - kh // 2, 0, rows - kh)
        delta = r - r0
        off = pl.multiple_of((r0 - (i - 1) * ATT_RB) * GRID_W, GRID_W)
        qoff = pl.multiple_of(rr * GRID_W, GRID_W)
        def scores(p):
            cols = slice(p * LANES, (p + 1) * LANES)
            qp = q_ref[pl.ds(qoff, GRID_W), cols]
            zeros = jnp.zeros_like(qp)
            q2 = jnp.concatenate([jnp.where(lo, qp, zeros), jnp.where(lo, zeros, qp)], axis=0)
            bias = jnp.concatenate(
                [jnp.concatenate([bias_ref[2 * p + s, 2 * g + (kh - 1) - delta] for g in range(kh // 2)], axis=-1)
                 for s in range(2)], axis=0)
            s_loc = lax.dot_general(q2, kwin[pl.ds(off, band), cols], _NT, preferred_element_type=F32) + bias
            s_ctx = lax.dot_general(q2, kx_ref[:, cols], _NT, preferred_element_type=F32)
            s_sc[p, :, 0:band] = s_loc
            s_sc[p, :, band:] = s_ctx
            m_sc[p] = jnp.maximum(lane_groups(s_loc, jnp.maximum), lane_groups(s_ctx, jnp.maximum))

        def numerators(p):
            m = jnp.max(m_sc[p], axis=-1, keepdims=True)
            e = jnp.exp(s_sc[p] - m)
            p_sc[p] = e.astype(BF16)
            l_sc[p] = lane_groups(e, jnp.add)

        def values(p):
            cols = slice(p * LANES, (p + 1) * LANES)
            o = (jnp.dot(p_sc[p, :, 0:band], vwin[pl.ds(off, band), cols], preferred_element_type=F32)
                 + jnp.dot(p_sc[p, :, band:], vx_ref[:, cols], preferred_element_type=F32))
            o = o / jnp.sum(l_sc[p], axis=-1, keepdims=True)
            o_ref[pl.ds(qoff, GRID_W), cols] = jnp.where(lo, o[0:GRID_W], o[GRID_W:]).astype(BF16)

        for stage in (scores, numerators, values):
            for p in range(n_pairs):
                stage(p)
        return carry

    lax.fori_loop(0, ATT_RB, row_body, 0, unroll=4)


def _attn(q, k, v, kx, vx, bias):
    bsz, t, aw = q.shape
    l = kx.shape[1]
    rows = t // GRID_W
    kh = (bias.shape[1] + 2) // 2
    nb = rows // ATT_RB
    blk = ATT_RB * GRID_W
    assert kh == ATT_RB and rows % ATT_RB == 0 and aw % LANES == 0 and 2 * GRID_W == LANES
    cur = pl.BlockSpec((None, blk, aw), lambda b, i: (b, i, 0))
    prev = pl.BlockSpec((None, blk, aw), lambda b, i: (b, jnp.maximum(i - 1, 0), 0))
    nxt = pl.BlockSpec((None, blk, aw), lambda b, i: (b, jnp.minimum(i + 1, nb - 1), 0))
    cx = pl.BlockSpec((None, l, aw), lambda b, i: (b, 0, 0))
    nh = bias.shape[0]
    nkeys = kh * GRID_W + l
    assert l % LANES == 0 and nh == 2 * (aw // LANES)
    return pl.pallas_call(
        functools.partial(_attn_body, rows=rows, kh=kh, n_pairs=aw // LANES),
        grid=(bsz, nb),
        in_specs=[cur, prev, cur, nxt, prev, cur, nxt, cx, cx,
                  pl.BlockSpec(bias.shape, lambda b, i: (0, 0, 0, 0))],
        out_specs=cur,
        out_shape=jax.ShapeDtypeStruct((bsz, t, aw), BF16),
        scratch_shapes=[pltpu.VMEM((3 * blk, aw), BF16), pltpu.VMEM((3 * blk, aw), BF16),
                        pltpu.VMEM((nh // 2, 2 * GRID_W, nkeys), F32), pltpu.VMEM((nh // 2, 2 * GRID_W, nkeys), BF16),
                        pltpu.VMEM((nh // 2, 2 * GRID_W, LANES), F32), pltpu.VMEM((nh // 2, 2 * GRID_W, LANES), F32)],
        compiler_params=_params(("arbitrary", "arbitrary"), VMEM_LIMIT),
        name="attn",
    )(q, k, k, k, v, v, v, kx, vx, bias)


def _outproj_body(yc_ref, ya_ref, x_ref, wo_ref, g1_ref, gf_ref, sh_ref, sc_ref, wr_ref,
                  x1_ref, h2_ref, aff_ref, *, cw, tm, ne):
    acc = (jnp.dot(yc_ref[...], wo_ref[0:cw, :], preferred_element_type=F32)
           + jnp.dot(ya_ref[...], wo_ref[cw:, :], preferred_element_type=F32))
    x1 = x_ref[...] + g1_ref[...] * acc
    x1_ref[...] = x1
    h2 = _norm_mod(x1, gf_ref[...], sh_ref[...], sc_ref[...])
    for s in range(h2.shape[-1] // LANES):
        _tile_rows(h2_ref, 0, tm, s)[...] = h2[:, s * LANES:(s + 1) * LANES]
    hi = h2.astype(BF16)
    lo = (h2 - hi.astype(F32)).astype(BF16)
    part_hi = jnp.dot(hi, wr_ref[...], preferred_element_type=F32)
    part_lo = jnp.dot(lo, wr_ref[...], preferred_element_type=F32)
    logits = part_hi + pltpu.roll(part_hi, LANES - ne, axis=1) + part_lo
    logits = jnp.where(lax.broadcasted_iota(I32, logits.shape, 1) < ne, logits, NEG_INF)
    e = jnp.exp(logits - jnp.max(logits, axis=-1, keepdims=True))
    aff_ref[...] = (e / jnp.sum(e, axis=-1, keepdims=True)).T[0:ne, :]


def _outproj(y_conv, y_attn, x, wo_bf, mods4, g_ffn, w_router, tm):
    bsz, t, d = x.shape
    cw = y_conv.shape[-1]
    ne = w_router.shape[1]
    assert d == SUBLANES * LANES and 2 * ne <= LANES
    nt = t // tm
    w_hi = w_router.astype(BF16)
    w_lo = (w_router - w_hi.astype(F32)).astype(BF16)
    wr_t = jnp.zeros((d, LANES), BF16).at[:, 0:ne].set(w_hi).at[:, ne:2 * ne].set(w_lo)
    return pl.pallas_call(
        functools.partial(_outproj_body, cw=cw, tm=tm, ne=ne),
        grid=(bsz, nt),
        in_specs=[pl.BlockSpec((None, tm, cw), lambda b, i: (b, i, 0)),
                  pl.BlockSpec((None, tm, y_attn.shape[-1]), lambda b, i: (b, i, 0)),
                  pl.BlockSpec((None, tm, d), lambda b, i: (b, i, 0)),
                  pl.BlockSpec(wo_bf.shape, lambda b, i: (0, 0)),
                  _mod_spec(2, lambda b: b)(d),
                  pl.BlockSpec((1, d), lambda b, i: (0, 0)),
                  _mod_spec(3, lambda b: b)(d),
                  _mod_spec(4, lambda b: b)(d),
                  pl.BlockSpec(wr_t.shape, lambda b, i: (0, 0))],
        out_specs=[pl.BlockSpec((None, tm, d), lambda b, i: (b, i, 0)),
                   pl.BlockSpec((tm * SUBLANES, LANES), lambda b, i: (b * nt + i, 0)),
                   pl.BlockSpec((None, ne, tm), lambda b, i: (b, 0, i))],
        out_shape=[jax.ShapeDtypeStruct((bsz, t, d), F32),
                   jax.ShapeDtypeStruct((bsz * t * SUBLANES, LANES), F32),
                   jax.ShapeDtypeStruct((bsz, ne, t), F32)],
        compiler_params=_params(("arbitrary", "arbitrary"), VMEM_LIMIT),
        name="outproj",
    )(y_conv, y_attn, x, wo_bf, mods4, g_ffn, mods4, mods4, wr_t)


def _route_body(affe_ref, affr_ref, u_ref, ones_ref, l_ref, tok_ref, *, cap, ne, nblk):
    aff_e = affe_ref[...]

    def search(it, bits):
        cand = bits | jnp.left_shift(jnp.int32(1), 30 - it)
        cnt = jnp.sum(jnp.where(aff_e >= pltpu.bitcast(cand, F32), 1.0, 0.0), axis=1, keepdims=True)
        return jnp.where(cnt >= cap, cand, bits)

    tau = pltpu.bitcast(lax.fori_loop(0, 31, search, jnp.zeros((ne, 1), I32)), F32)
    n_gt = jnp.sum(jnp.where(aff_e > tau, 1.0, 0.0), axis=1, keepdims=True)
    need = cap - n_gt

    def per_row(col):
        return jnp.concatenate([jnp.broadcast_to(col[e:e + 1, :], (nblk, 1)) for e in range(ne)], axis=0)

    tau_r = per_row(tau)
    need_r = per_row(need)
    kr = affr_ref[...]

    def scan(mf):
        mb = mf.astype(BF16)
        incl = jnp.dot(mb, u_ref[...], preferred_element_type=F32)
        tot = jnp.dot(mb, ones_ref[...], preferred_element_type=F32)
        off = jnp.dot(l_ref[...], tot.astype(BF16), preferred_element_type=F32)
        return incl, tot, off

    eq = jnp.where(kr == tau_r, 1.0, 0.0)
    incl_e, _, off_e = scan(eq)
    rank_eq = off_e + incl_e - eq
    sel = jnp.where((kr > tau_r) | ((eq > 0.5) & (rank_eq < need_r)), 1.0, 0.0)
    incl, _, off = scan(sel)

    nbits = LANES.bit_length() - 1
    lane = lax.broadcasted_iota(I32, kr.shape, 1)
    row = lax.broadcasted_iota(I32, kr.shape, 0)
    dist = lane - (incl.astype(I32) - 1)
    word = jnp.where(sel > 0.5, (1 << (2 * nbits)) | (dist << nbits) | lane, 0)

    def bit(x, pos):
        return (x >> pos) & 1

    for k in range(nbits):
        arriving = pltpu.roll(word, LANES - (1 << k), axis=1)
        take = bit(arriving, 2 * nbits) & bit(arriving, nbits + k)
        stay = bit(word, 2 * nbits) & (1 - bit(word, nbits + k))
        word = jnp.where(take == 1, arriving, jnp.where(stay == 1, word, 0))

    off_i = off.astype(I32)
    shift = off_i & (LANES - 1)
    for k in range(nbits):
        word = jnp.where(bit(shift, k) == 1, pltpu.roll(word, 1 << k, axis=1), word)
    valid = bit(word, 2 * nbits) == 1
    wrapped = lane < shift
    blk_bits = nblk.bit_length() - 1
    q0 = (row >> blk_bits) * (cap // LANES) + (off_i >> nbits)
    digits = (jnp.where(valid, row & (nblk - 1), 0), jnp.where(valid, word & (LANES - 1), 0))

    def place(sel_rows, q):
        onehot_t = jnp.where(q == lane, 1.0, 0.0).astype(BF16)
        outs = []
        for dgt in digits:
            x = jnp.where(sel_rows, dgt, 0).astype(F32).astype(BF16)
            outs.append(lax.dot_general(onehot_t, x, (((0,), (0,)), ((), ())), preferred_element_type=F32))
        return outs[0] * LANES + outs[1]

    tok_ref[...] = (place(jnp.logical_not(wrapped), q0) + place(wrapped, q0 + 1)).astype(I32)


def _route(aff, cap):
    bsz, ne, t = aff.shape
    nblk = t // LANES
    nr = ne * nblk
    nq = ne * cap // LANES
    assert nq == LANES and cap % LANES == 0 and nblk & (nblk - 1) == 0 and nblk <= 256
    aff_r = aff.reshape(bsz, nr, LANES)
    ar = jnp.arange(LANES)
    upper = (ar[:, None] <= ar[None, :]).astype(BF16)
    ones = jnp.ones((LANES, LANES), BF16)
    rr = jnp.arange(nr)
    lower = ((rr[:, None] // nblk == rr[None, :] // nblk) & (rr[None, :] < rr[:, None])).astype(BF16)
    const = lambda a: pl.BlockSpec(a.shape, lambda b: (0,) * a.ndim)
    return pl.pallas_call(
        functools.partial(_route_body, cap=cap, ne=ne, nblk=nblk),
        grid=(bsz,),
        in_specs=[pl.BlockSpec((None, ne, t), lambda b: (b, 0, 0)),
                  pl.BlockSpec((None, nr, LANES), lambda b: (b, 0, 0)),
                  const(upper), const(ones), const(lower)],
        out_specs=pl.BlockSpec((None, nq, LANES), lambda b: (b, 0, 0)),
        out_shape=jax.ShapeDtypeStruct((bsz, nq, LANES), I32),
        compiler_params=_params(("arbitrary",), VMEM_LIMIT),
        name="route",
    )(aff, aff_r, upper, ones, lower)


MOE_HC = 512
MOE_TM = 512
ROW_UNROLL = 8


def _tile_rows(ref, first, n, s):
    return ref.at[pl.ds(first * SUBLANES + s, n, stride=SUBLANES), :]


def _gather_copy(h2_hbm, xs3, sem, token, slot):
    src = h2_hbm.at[pl.ds(pl.multiple_of(token * SUBLANES, SUBLANES), SUBLANES), :]
    dst = xs3.at[pl.ds(pl.multiple_of(slot * SUBLANES, SUBLANES), SUBLANES), :]
    return pltpu.make_async_copy(src, dst, sem)


def _moe_body(tok_ref, nxt_ref, h2_hbm, wg_ref, wu_ref, wd_ref, y_ref, xs3, xsb, wgb, wub, wdb, sem,
              *, bsz, t, cap, ne, nch):
    e = pl.program_id(0)
    c = pl.program_id(1)
    n_rows = bsz * cap
    n_tiles = n_rows // MOE_TM
    per_tile = n_rows // (nch * n_tiles)
    n_lane_groups = xsb.shape[-1] // LANES

    def fetch(ref, b, slot):
        _gather_copy(h2_hbm, xs3, sem, b * t + ref[b, 0, slot], b * cap + slot).start()

    def wait_staging():
        pltpu.make_async_copy(h2_hbm.at[pl.ds(0, n_rows * SUBLANES), :], xs3, sem).wait()

    @pl.when((e == 0) & (c == 0))
    def _first_expert():
        for b in range(bsz):
            def rows(g, carry):
                for k in range(ROW_UNROLL):
                    fetch(tok_ref, b, g * ROW_UNROLL + k)
                return carry

            lax.fori_loop(0, cap // ROW_UNROLL, rows, 0)

    def tiles(first_chunk):
        wgb[...] = wg_ref[...].astype(BF16)
        wub[...] = wu_ref[...].astype(BF16)
        wdb[...] = wd_ref[...].astype(BF16)
        for m in range(n_tiles):
            r0 = m * MOE_TM
            if first_chunk:
                xm = jnp.concatenate([_tile_rows(xs3, r0, MOE_TM, s)[...].astype(BF16)
                                      for s in range(n_lane_groups)], axis=-1)
                xsb[r0:r0 + MOE_TM, :] = xm
            else:
                xm = xsb[r0:r0 + MOE_TM, :]
            gate = jnp.dot(xm, wgb[...], preferred_element_type=F32)
            first = (c * n_tiles + m) * per_tile
            b = first // cap
            for k in range(per_tile):
                fetch(nxt_ref, b, first - b * cap + k)
            up = jnp.dot(xm, wub[...], preferred_element_type=F32)
            hid = (gate * jax.nn.sigmoid(gate) * up).astype(BF16)
            contrib = jnp.dot(hid, wdb[...], preferred_element_type=F32)
            if first_chunk:
                y_ref[r0:r0 + MOE_TM, :] = contrib
            else:
                y_ref[r0:r0 + MOE_TM, :] += contrib

    @pl.when(c == 0)
    def _():
        wait_staging()
        tiles(True)

    @pl.when(c > 0)
    def _():
        tiles(False)

    @pl.when((e == ne - 1) & (c == nch - 1))
    def _drain():
        wait_staging()


def _moe(tok, h2_tiles, w_gate, w_up, w_down, bsz, t, cap):
    ne, d, dh = w_gate.shape
    nch = dh // MOE_HC
    n_rows = bsz * cap
    per_tile = n_rows // (nch * (n_rows // MOE_TM))
    assert cap % ROW_UNROLL == 0 and n_rows % MOE_TM == 0 and cap % per_tile == 0 and n_rows // nch <= MOE_TM
    assert d == SUBLANES * LANES
    tok_spec = lambda ahead: pl.BlockSpec((bsz, None, 1, cap),
                                          lambda e, c: (0, jnp.minimum(e + ahead, ne - 1), 0, 0),
                                          memory_space=pltpu.SMEM)
    return pl.pallas_call(
        functools.partial(_moe_body, bsz=bsz, t=t, cap=cap, ne=ne, nch=nch),
        grid=(ne, nch),
        in_specs=[tok_spec(0), tok_spec(1),
                  pl.BlockSpec(memory_space=pl.ANY),
                  pl.BlockSpec((None, d, MOE_HC), lambda e, c: (e, 0, c)),
                  pl.BlockSpec((None, d, MOE_HC), lambda e, c: (e, 0, c)),
                  pl.BlockSpec((None, MOE_HC, d), lambda e, c: (e, c, 0))],
        out_specs=pl.BlockSpec((None, n_rows, d), lambda e, c: (e, 0, 0)),
        out_shape=jax.ShapeDtypeStruct((ne, n_rows, d), F32),
        scratch_shapes=[pltpu.VMEM((n_rows * SUBLANES, LANES), F32),
                        pltpu.VMEM((n_rows, d), BF16),
                        pltpu.VMEM((d, MOE_HC), BF16),
                        pltpu.VMEM((d, MOE_HC), BF16),
                        pltpu.VMEM((MOE_HC, d), BF16),
                        pltpu.SemaphoreType.DMA],
        compiler_params=_params(("arbitrary", "arbitrary"), VMEM_LIMIT),
        name="moe",
    )(tok, tok, h2_tiles, w_gate, w_up, w_down)


def _combine_body(tok_ref, aff_ref, y_ref, x1_ref, g2_ref, gf_ref, o_ref, acc, ysc, *, cap, ne, tm):
    s = pl.program_id(1)

    @pl.when(s == 0)
    def _():
        acc[...] = jnp.zeros_like(acc)

    @pl.when(s < ne)
    def _scatter():
        for g in range(SUBLANES):
            _tile_rows(ysc, 0, cap, g)[...] = y_ref[:, g * LANES:(g + 1) * LANES]

        def tile_of(row):
            return pl.ds(pl.multiple_of(row * SUBLANES, SUBLANES), SUBLANES)

        def rows(g, carry):
            tokens = [tok_ref[0, g * ROW_UNROLL + k] for k in range(ROW_UNROLL)]
            new = [acc[tile_of(tk), :] + aff_ref[0, tk] * ysc[tile_of(g * ROW_UNROLL + k), :]
                   for k, tk in enumerate(tokens)]
            for tk, val in zip(tokens, new):
                acc[tile_of(tk), :] = val
            return carry

        lax.fori_loop(0, cap // ROW_UNROLL, rows, 0)

    @pl.when(s >= ne)
    def _finish():
        first = pl.multiple_of((s - ne) * tm, tm)
        moe = jnp.concatenate([_tile_rows(acc, first, tm, g)[...] for g in range(SUBLANES)], axis=-1)
        x = x1_ref[...] + g2_ref[...] * moe
        ms = jnp.mean(x * x, axis=-1, keepdims=True)
        o_ref[...] = x * lax.rsqrt(ms + EPS) * gf_ref[...]


def _combine(tok, aff, y, x1, mods4, g_final, cap, tm):
    bsz, t, d = x1.shape
    ne = aff.shape[1]
    nt = t // tm
    assert d == SUBLANES * LANES
    expert = lambda s: jnp.minimum(s, ne - 1)
    tile = lambda s: jnp.maximum(s - ne, 0)
    smem = lambda n: pl.BlockSpec((None, None, 1, n), lambda b, s: (b, expert(s), 0, 0), memory_space=pltpu.SMEM)
    token_tile = pl.BlockSpec((None, tm, d), lambda b, s: (b, tile(s), 0))
    return pl.pallas_call(
        functools.partial(_combine_body, cap=cap, ne=ne, tm=tm),
        grid=(bsz, ne + nt),
        in_specs=[smem(cap), smem(t),
                  pl.BlockSpec((None, cap, d), lambda b, s: (expert(s), b, 0)),
                  token_tile,
                  _mod_spec(5, lambda b: b)(d),
                  pl.BlockSpec((1, d), lambda b, s: (0, 0))],
        out_specs=token_tile,
        out_shape=jax.ShapeDtypeStruct((bsz, t, d), F32),
        scratch_shapes=[pltpu.VMEM((t * SUBLANES, LANES), F32),
                        pltpu.VMEM((cap * SUBLANES, LANES), F32)],
        compiler_params=_params(("arbitrary", "arbitrary"), VMEM_LIMIT),
        name="combine",
    )(tok, aff, y, x1, mods4, g_final)


def _layer(x, ctx, mods4, ctx_row, norm_mix_g, w_in, conv_w, conv_b, conv_ln_g, conv_ln_b, rpb, w_out,
           norm_ffn_g, w_router, w_gate, w_up, w_down, norm_final_g):
    bsz, t, d = x.shape
    cw = conv_w.shape[1]
    n_heads = rpb.shape[0]
    aw = (w_in.shape[1] - 2 * cw) // 3
    head_dim = aw // n_heads
    rows = t // GRID_W
    ne = w_router.shape[1]
    cap = EC_FACTOR * t // ne
    row = lambda a: a.reshape(1, -1)

    w_in_bf = w_in.astype(BF16)
    u, q, k, v = _inproj(x, row(norm_mix_g), mods4, w_in_bf, cw, aw, head_dim ** -0.5, tm=1024)
    kx, vx = _inproj_ctx(ctx, row(norm_mix_g), mods4, w_in_bf, 2 * cw + aw, aw, ctx_row)
    y_conv = _conv(u, conv_w, conv_b, conv_ln_g, conv_ln_b, tt=512)
    assert rows >= (rpb.shape[1] + 1) // 2
    y_attn = _attn(q, k, v, kx, vx, _bias_table(rpb))
    x1, h2_tiles, aff = _outproj(y_conv, y_attn, x, w_out.astype(BF16), mods4, row(norm_ffn_g),
                                 w_router, tm=1024)
    tok = _route(aff, cap).reshape(bsz, ne, 1, cap)
    y = _moe(tok, h2_tiles, w_gate, w_up, w_down, bsz, t, cap)
    return _combine(tok, aff.reshape(bsz, ne, 1, t), y, x1, mods4, row(norm_final_g), cap, tm=512)


def kernel(x, c, ctx, c_ctx, w_mod, b_mod, norm_mix_g, w_in, conv_w, conv_b, conv_ln_g, conv_ln_b, rpb,
           w_out, norm_ffn_g, w_router, w_gate, w_up, w_down, norm_final_g):
    bsz, t, d = x.shape
    depth = w_mod.shape[0]
    assert depth == 1
    assert bsz + 1 <= SUBLANES
    c8 = jnp.zeros((SUBLANES, d), F32).at[:bsz].set(c).at[bsz].set(c_ctx)
    mods4 = _adaln(c8, w_mod[0], b_mod[0].reshape(1, -1)).reshape(SUBLANES, N_MOD, 1, d)
    return _layer(x, ctx, mods4, bsz, norm_mix_g[0], w_in[0], conv_w[0], conv_b[0], conv_ln_g[0],
                  conv_ln_b[0], rpb[0], w_out[0], norm_ffn_g[0], w_router[0], w_gate[0], w_up[0],
                  w_down[0], norm_final_g)
```

```python
import functools

import jax
import jax.numpy as jnp
from jax import lax
from jax.experimental import pallas as pl
from jax.experimental.pallas import tpu as pltpu

F32 = jnp.float32
BF16 = jnp.bfloat16
I32 = jnp.int32
HIGHEST = lax.Precision.HIGHEST

GRID_W = 64
NA_COLS = 16
EC_FACTOR = 2
N_MOD = 6
EPS = 1e-6
NEG_INF = -1e30
LANES = 128
SUBLANES = 8
VMEM_LIMIT = 56 * 1024 * 1024

_NT = (((1,), (1,)), ((), ()))


def _params(sem, vmem=None):
    return pltpu.CompilerParams(dimension_semantics=sem, vmem_limit_bytes=vmem)


def _adaln_body(c_ref, w_ref, b_ref, o_ref):
    c = c_ref[...]
    s = c * jax.nn.sigmoid(c)
    o_ref[...] = jnp.dot(s, w_ref[...], preferred_element_type=F32, precision=HIGHEST) + b_ref[...]


def _adaln(c8, w_mod, b_mod):
    d, n = w_mod.shape
    bn = 512
    return pl.pallas_call(
        _adaln_body,
        grid=(n // bn,),
        in_specs=[pl.BlockSpec((SUBLANES, d), lambda j: (0, 0)),
                  pl.BlockSpec((d, bn), lambda j: (0, j)),
                  pl.BlockSpec((1, bn), lambda j: (0, j))],
        out_specs=pl.BlockSpec((SUBLANES, bn), lambda j: (0, j)),
        out_shape=jax.ShapeDtypeStruct((SUBLANES, n), F32),
        compiler_params=_params(("arbitrary",)),
        name="adaln",
    )(c8, w_mod, b_mod)


def _norm_mod(x, g, shift, scale):
    ms = jnp.mean(x * x, axis=-1, keepdims=True)
    h = x * lax.rsqrt(ms + EPS) * g
    return h * (1.0 + scale) + shift


def _inproj_body(x_ref, g_ref, sh_ref, sc_ref, w_ref, u_ref, q_ref, k_ref, v_ref, *, cw, aw, qscale):
    hb = _norm_mod(x_ref[...], g_ref[...], sh_ref[...], sc_ref[...]).astype(BF16)

    def proj(lo, width):
        return jnp.dot(hb, w_ref[:, lo:lo + width], preferred_element_type=F32)

    a = proj(0, cw)
    gate = proj(cw, cw)
    u_ref[...] = a * jax.nn.sigmoid(gate)
    q_ref[...] = (proj(2 * cw, aw) * qscale).astype(BF16)
    k_ref[...] = proj(2 * cw + aw, aw).astype(BF16)
    v_ref[...] = proj(2 * cw + 2 * aw, aw).astype(BF16)


def _inproj_kv_body(x_ref, g_ref, sh_ref, sc_ref, wk_ref, wv_ref, k_ref, v_ref):
    hb = _norm_mod(x_ref[...], g_ref[...], sh_ref[...], sc_ref[...]).astype(BF16)
    k_ref[...] = jnp.dot(hb, wk_ref[...], preferred_element_type=F32).astype(BF16)
    v_ref[...] = jnp.dot(hb, wv_ref[...], preferred_element_type=F32).astype(BF16)


def _mod_spec(slot, row_of):
    return lambda d: pl.BlockSpec((None, None, 1, d), lambda b, i: (row_of(b), slot, 0, 0))


def _inproj(x, g, mods4, w_bf, cw, aw, qscale, tm):
    bsz, t, d = x.shape
    tok = lambda w, dt: (pl.BlockSpec((None, tm, w), lambda b, i: (b, i, 0)), jax.ShapeDtypeStruct((bsz, t, w), dt))
    outs = [tok(cw, F32), tok(aw, BF16), tok(aw, BF16), tok(aw, BF16)]
    return pl.pallas_call(
        functools.partial(_inproj_body, cw=cw, aw=aw, qscale=qscale),
        grid=(bsz, t // tm),
        in_specs=[pl.BlockSpec((None, tm, d), lambda b, i: (b, i, 0)),
                  pl.BlockSpec((1, d), lambda b, i: (0, 0)),
                  _mod_spec(0, lambda b: b)(d),
                  _mod_spec(1, lambda b: b)(d),
                  pl.BlockSpec(w_bf.shape, lambda b, i: (0, 0))],
        out_specs=[o[0] for o in outs],
        out_shape=[o[1] for o in outs],
        compiler_params=_params(("arbitrary", "arbitrary"), VMEM_LIMIT),
        name="inproj",
    )(x, g, mods4, mods4, w_bf)


def _inproj_ctx(ctx, g, mods4, w_bf, k_lo, aw, ctx_row):
    bsz, l, d = ctx.shape
    assert k_lo % aw == 0
    spec = pl.BlockSpec((None, l, aw), lambda b, i: (b, 0, 0))
    shape = jax.ShapeDtypeStruct((bsz, l, aw), BF16)
    w_cols = lambda j: pl.BlockSpec((d, aw), lambda b, i: (0, k_lo // aw + j))
    return pl.pallas_call(
        _inproj_kv_body,
        grid=(bsz, 1),
        in_specs=[pl.BlockSpec((None, l, d), lambda b, i: (b, 0, 0)),
                  pl.BlockSpec((1, d), lambda b, i: (0, 0)),
                  _mod_spec(0, lambda b: ctx_row)(d),
                  _mod_spec(1, lambda b: ctx_row)(d),
                  w_cols(0), w_cols(1)],
        out_specs=[spec, spec],
        out_shape=[shape, shape],
        compiler_params=_params(("arbitrary", "arbitrary"), VMEM_LIMIT),
        name="inproj_ctx",
    )(ctx, g, mods4, mods4, w_bf, w_bf)


CONV_HALO = 16
CONV_RC = 64
CONV_CW = 256


def _conv_body(prev_ref, cur_ref, next_ref, w_ref, b_ref, lg_ref, lb_ref, o_ref, ext, ysc, *, width, tt):
    i = pl.program_id(1)
    n = pl.num_programs(1)
    ext[0:CONV_HALO, :] = jnp.where(i > 0, prev_ref[...], 0.0)
    ext[CONV_HALO:CONV_HALO + tt, :] = cur_ref[...]
    ext[CONV_HALO + tt:, :] = jnp.where(i < n - 1, next_ref[...], 0.0)
    lead = CONV_HALO - width // 2
    for rc in range(tt // CONV_RC):
        r0 = rc * CONV_RC
        for ch in range(cur_ref.shape[-1] // CONV_CW):
            cs = slice(ch * CONV_CW, (ch + 1) * CONV_CW)
            out = None
            for m in range(SUBLANES):
                nrow = CONV_RC + (SUBLANES if m else 0)
                part = None
                for j in range(width):
                    if (lead + j) % SUBLANES != m:
                        continue
                    a = r0 + lead + j - m
                    term = w_ref[j:j + 1, cs] * ext[a:a + nrow, cs]
                    part = term if part is None else part + term
                piece = part[m:m + CONV_RC]
                out = piece if out is None else out + piece
            ysc[r0:r0 + CONV_RC, cs] = out
    y = ysc[...] + b_ref[...]
    mu = jnp.mean(y, axis=-1, keepdims=True)
    yc = y - mu
    var = jnp.mean(yc * yc, axis=-1, keepdims=True)
    yn = yc * lax.rsqrt(var + EPS) * lg_ref[...] + lb_ref[...]
    o_ref[...] = (yn * jax.nn.sigmoid(yn)).astype(BF16)


def _conv(u, conv_w, conv_b, ln_g, ln_b, tt):
    bsz, t, cw = u.shape
    width = conv_w.shape[0]
    assert width // 2 < CONV_HALO and tt % CONV_HALO == 0 and tt % CONV_RC == 0 and cw % CONV_CW == 0
    hb = tt // CONV_HALO
    nh = t // CONV_HALO
    row = lambda a: a.reshape(1, cw)
    return pl.pallas_call(
        functools.partial(_conv_body, width=width, tt=tt),
        grid=(bsz, t // tt),
        in_specs=[pl.BlockSpec((None, CONV_HALO, cw), lambda b, i: (b, jnp.maximum(i * hb - 1, 0), 0)),
                  pl.BlockSpec((None, tt, cw), lambda b, i: (b, i, 0)),
                  pl.BlockSpec((None, CONV_HALO, cw), lambda b, i: (b, jnp.minimum((i + 1) * hb, nh - 1), 0)),
                  pl.BlockSpec((width, cw), lambda b, i: (0, 0)),
                  pl.BlockSpec((1, cw), lambda b, i: (0, 0)),
                  pl.BlockSpec((1, cw), lambda b, i: (0, 0)),
                  pl.BlockSpec((1, cw), lambda b, i: (0, 0))],
        out_specs=pl.BlockSpec((None, tt, cw), lambda b, i: (b, i, 0)),
        out_shape=jax.ShapeDtypeStruct((bsz, t, cw), BF16),
        scratch_shapes=[pltpu.VMEM((tt + 2 * CONV_HALO, cw), F32), pltpu.VMEM((tt, cw), F32)],
        compiler_params=_params(("arbitrary", "arbitrary")),
        name="conv",
    )(u, u, u, conv_w, row(conv_b), row(ln_g), row(ln_b))


ATT_RB = 8


def _bias_body(rpb_ref, onehot_ref, valid_ref, o_ref):
    t = jnp.dot(rpb_ref[...], onehot_ref[...], preferred_element_type=F32, precision=HIGHEST)
    o_ref[...] = jnp.where(valid_ref[...] > 0.5, t, NEG_INF)


def _bias_table(rpb):
    nh, ndr, ndc = rpb.shape
    qc = jnp.arange(GRID_W)[:, None]
    kc = jnp.arange(GRID_W)[None, :]
    c0 = jnp.clip(qc - NA_COLS // 2, 0, GRID_W - NA_COLS)
    valid = ((kc >= c0) & (kc < c0 + NA_COLS)).astype(F32).reshape(1, GRID_W * GRID_W)
    d_col = jnp.clip(kc - qc + (NA_COLS - 1), 0, 2 * NA_COLS - 2).reshape(1, GRID_W * GRID_W)
    ndc_pad = -(-ndc // SUBLANES) * SUBLANES
    onehot = (jnp.arange(ndc_pad)[:, None] == d_col).astype(F32)
    rpb2 = jnp.pad(rpb.reshape(nh * ndr, ndc), ((0, 0), (0, ndc_pad - ndc)))
    toeplitz = pl.pallas_call(
        _bias_body,
        out_shape=jax.ShapeDtypeStruct((nh * ndr, GRID_W * GRID_W), F32),
        name="bias",
    )(rpb2, onehot, valid).reshape(nh, ndr, GRID_W, GRID_W)
    return jnp.concatenate([toeplitz[:, :-1], toeplitz[:, 1:]], axis=-1)


def _attn_body(q_ref, kp_ref, kc_ref, kn_ref, vp_ref, vc_ref, vn_ref, kx_ref, vx_ref, bias_ref, o_ref,
               kwin, vwin, s_sc, p_sc, m_sc, l_sc, *, rows, kh, n_pairs):
    i = pl.program_id(1)
    blk = ATT_RB * GRID_W
    for w, (a, b) in enumerate(((kp_ref, vp_ref), (kc_ref, vc_ref), (kn_ref, vn_ref))):
        kwin[w * blk:(w + 1) * blk, :] = a[...]
        vwin[w * blk:(w + 1) * blk, :] = b[...]
    lo = lax.broadcasted_iota(I32, (GRID_W, LANES), 1) < LANES // 2
    band = kh * GRID_W

    def lane_groups(x, op):
        parts = [x[:, g * LANES:(g + 1) * LANES] for g in range(x.shape[-1] // LANES)]
        while len(parts) > 1:
            parts = [op(parts[a], parts[a + 1]) for a in range(0, len(parts) - 1, 2)] + parts[len(parts) & ~1:]
        return parts[0]

    def row_body(rr, carry):
        r = i * ATT_RB + rr
        r0 = jnp.clip(r - kh // 2, 0, rows - kh)
        delta = r - r0
        off = pl.multiple_of((r0 - (i - 1) * ATT_RB) * GRID_W, GRID_W)
        qoff = pl.multiple_of(rr * GRID_W, GRID_W)
        def scores(p):
            cols = slice(p * LANES, (p + 1) * LANES)
            qp = q_ref[pl.ds(qoff, GRID_W), cols]
            zeros = jnp.zeros_like(qp)
            q2 = jnp.concatenate([jnp.where(lo, qp, zeros), jnp.where(lo, zeros, qp)], axis=0)
            bias = jnp.concatenate(
                [jnp.concatenate([bias_ref[2 * p + s, 2 * g + (kh - 1) - delta] for g in range(kh // 2)], axis=-1)
                 for s in range(2)], axis=0)
            s_loc = lax.dot_general(q2, kwin[pl.ds(off, band), cols], _NT, preferred_element_type=F32) + bias
            s_ctx = lax.dot_general(q2, kx_ref[:, cols], _NT, preferred_element_type=F32)
            s_sc[p, :, 0:band] = s_loc
            s_sc[p, :, band:] = s_ctx
            m_sc[p] = jnp.maximum(lane_groups(s_loc, jnp.maximum), lane_groups(s_ctx, jnp.maximum))

        def numerators(p):
            m = jnp.max(m_sc[p], axis=-1, keepdims=True)
            e = jnp.exp(s_sc[p] - m)
            p_sc[p] = e.astype(BF16)
            l_sc[p] = lane_groups(e, jnp.add)

        def values(p):
            cols = slice(p * LANES, (p + 1) * LANES)
            o = (jnp.dot(p_sc[p, :, 0:band], vwin[pl.ds(off, band), cols], preferred_element_type=F32)
                 + jnp.dot(p_sc[p, :, band:], vx_ref[:, cols], preferred_element_type=F32))
            o = o / jnp.sum(l_sc[p], axis=-1, keepdims=True)
            o_ref[pl.ds(qoff, GRID_W), cols] = jnp.where(lo, o[0:GRID_W], o[GRID_W:]).astype(BF16)

        for stage in (scores, numerators, values):
            for p in range(n_pairs):
                stage(p)
        return carry

    lax.fori_loop(0, ATT_RB, row_body, 0, unroll=4)


def _attn(q, k, v, kx, vx, bias):
    bsz, t, aw = q.shape
    l = kx.shape[1]
    rows = t // GRID_W
    kh = (bias.shape[1] + 2) // 2
    nb = rows // ATT_RB
    blk = ATT_RB * GRID_W
    assert kh == ATT_RB and rows % ATT_RB == 0 and aw % LANES == 0 and 2 * GRID_W == LANES
    cur = pl.BlockSpec((None, blk, aw), lambda b, i: (b, i, 0))
    prev = pl.BlockSpec((None, blk, aw), lambda b, i: (b, jnp.maximum(i - 1, 0), 0))
    nxt = pl.BlockSpec((None, blk, aw), lambda b, i: (b, jnp.minimum(i + 1, nb - 1), 0))
    cx = pl.BlockSpec((None, l, aw), lambda b, i: (b, 0, 0))
    nh = bias.shape[0]
    nkeys = kh * GRID_W + l
    assert l % LANES == 0 and nh == 2 * (aw // LANES)
    return pl.pallas_call(
        functools.partial(_attn_body, rows=rows, kh=kh, n_pairs=aw // LANES),
        grid=(bsz, nb),
        in_specs=[cur, prev, cur, nxt, prev, cur, nxt, cx, cx,
                  pl.BlockSpec(bias.shape, lambda b, i: (0, 0, 0, 0))],
        out_specs=cur,
        out_shape=jax.ShapeDtypeStruct((bsz, t, aw), BF16),
        scratch_shapes=[pltpu.VMEM((3 * blk, aw), BF16), pltpu.VMEM((3 * blk, aw), BF16),
                        pltpu.VMEM((nh // 2, 2 * GRID_W, nkeys), F32), pltpu.VMEM((nh // 2, 2 * GRID_W, nkeys), BF16),
                        pltpu.VMEM((nh // 2, 2 * GRID_W, LANES), F32), pltpu.VMEM((nh // 2, 2 * GRID_W, LANES), F32)],
        compiler_params=_params(("arbitrary", "arbitrary"), VMEM_LIMIT),
        name="attn",
    )(q, k, k, k, v, v, v, kx, vx, bias)


def _outproj_body(yc_ref, ya_ref, x_ref, wo_ref, g1_ref, gf_ref, sh_ref, sc_ref, wr_ref,
                  x1_ref, h2_ref, aff_ref, *, cw, tm, ne):
    acc = (jnp.dot(yc_ref[...], wo_ref[0:cw, :], preferred_element_type=F32)
           + jnp.dot(ya_ref[...], wo_ref[cw:, :], preferred_element_type=F32))
    x1 = x_ref[...] + g1_ref[...] * acc
    x1_ref[...] = x1
    h2 = _norm_mod(x1, gf_ref[...], sh_ref[...], sc_ref[...])
    for s in range(h2.shape[-1] // LANES):
        _tile_rows(h2_ref, 0, tm, s)[...] = h2[:, s * LANES:(s + 1) * LANES]
    hi = h2.astype(BF16)
    lo = (h2 - hi.astype(F32)).astype(BF16)
    part_hi = jnp.dot(hi, wr_ref[...], preferred_element_type=F32)
    part_lo = jnp.dot(lo, wr_ref[...], preferred_element_type=F32)
    logits = part_hi + pltpu.roll(part_hi, LANES - ne, axis=1) + part_lo
    logits = jnp.where(lax.broadcasted_iota(I32, logits.shape, 1) < ne, logits, NEG_INF)
    e = jnp.exp(logits - jnp.max(logits, axis=-1, keepdims=True))
    aff_ref[...] = (e / jnp.sum(e, axis=-1, keepdims=True)).T[0:ne, :]


def _outproj(y_conv, y_attn, x, wo_bf, mods4, g_ffn, w_router, tm):
    bsz, t, d = x.shape
    cw = y_conv.shape[-1]
    ne = w_router.shape[1]
    assert d == SUBLANES * LANES and 2 * ne <= LANES
    nt = t // tm
    w_hi = w_router.astype(BF16)
    w_lo = (w_router - w_hi.astype(F32)).astype(BF16)
    wr_t = jnp.zeros((d, LANES), BF16).at[:, 0:ne].set(w_hi).at[:, ne:2 * ne].set(w_lo)
    return pl.pallas_call(
        functools.partial(_outproj_body, cw=cw, tm=tm, ne=ne),
        grid=(bsz, nt),
        in_specs=[pl.BlockSpec((None, tm, cw), lambda b, i: (b, i, 0)),
                  pl.BlockSpec((None, tm, y_attn.shape[-1]), lambda b, i: (b, i, 0)),
                  pl.BlockSpec((None, tm, d), lambda b, i: (b, i, 0)),
                  pl.BlockSpec(wo_bf.shape, lambda b, i: (0, 0)),
                  _mod_spec(2, lambda b: b)(d),
                  pl.BlockSpec((1, d), lambda b, i: (0, 0)),
                  _mod_spec(3, lambda b: b)(d),
                  _mod_spec(4, lambda b: b)(d),
                  pl.BlockSpec(wr_t.shape, lambda b, i: (0, 0))],
        out_specs=[pl.BlockSpec((None, tm, d), lambda b, i: (b, i, 0)),
                   pl.BlockSpec((tm * SUBLANES, LANES), lambda b, i: (b * nt + i, 0)),
                   pl.BlockSpec((None, ne, tm), lambda b, i: (b, 0, i))],
        out_shape=[jax.ShapeDtypeStruct((bsz, t, d), F32),
                   jax.ShapeDtypeStruct((bsz * t * SUBLANES, LANES), F32),
                   jax.ShapeDtypeStruct((bsz, ne, t), F32)],
        compiler_params=_params(("arbitrary", "arbitrary"), VMEM_LIMIT),
        name="outproj",
    )(y_conv, y_attn, x, wo_bf, mods4, g_ffn, mods4, mods4, wr_t)


def _route_body(affe_ref, affr_ref, u_ref, ones_ref, l_ref, tok_ref, *, cap, ne, nblk):
    aff_e = affe_ref[...]

    def search(it, bits):
        cand = bits | jnp.left_shift(jnp.int32(1), 30 - it)
        cnt = jnp.sum(jnp.where(aff_e >= pltpu.bitcast(cand, F32), 1.0, 0.0), axis=1, keepdims=True)
        return jnp.where(cnt >= cap, cand, bits)

    tau = pltpu.bitcast(lax.fori_loop(0, 31, search, jnp.zeros((ne, 1), I32)), F32)
    n_gt = jnp.sum(jnp.where(aff_e > tau, 1.0, 0.0), axis=1, keepdims=True)
    need = cap - n_gt

    def per_row(col):
        return jnp.concatenate([jnp.broadcast_to(col[e:e + 1, :], (nblk, 1)) for e in range(ne)], axis=0)

    tau_r = per_row(tau)
    need_r = per_row(need)
    kr = affr_ref[...]

    def scan(mf):
        mb = mf.astype(BF16)
        incl = jnp.dot(mb, u_ref[...], preferred_element_type=F32)
        tot = jnp.dot(mb, ones_ref[...], preferred_element_type=F32)
        off = jnp.dot(l_ref[...], tot.astype(BF16), preferred_element_type=F32)
        return incl, tot, off

    eq = jnp.where(kr == tau_r, 1.0, 0.0)
    incl_e, _, off_e = scan(eq)
    rank_eq = off_e + incl_e - eq
    sel = jnp.where((kr > tau_r) | ((eq > 0.5) & (rank_eq < need_r)), 1.0, 0.0)
    incl, _, off = scan(sel)

    nbits = LANES.bit_length() - 1
    lane = lax.broadcasted_iota(I32, kr.shape, 1)
    row = lax.broadcasted_iota(I32, kr.shape, 0)
    dist = lane - (incl.astype(I32) - 1)
    word = jnp.where(sel > 0.5, (1 << (2 * nbits)) | (dist << nbits) | lane, 0)

    def bit(x, pos):
        return (x >> pos) & 1

    for k in range(nbits):
        arriving = pltpu.roll(word, LANES - (1 << k), axis=1)
        take = bit(arriving, 2 * nbits) & bit(arriving, nbits + k)
        stay = bit(word, 2 * nbits) & (1 - bit(word, nbits + k))
        word = jnp.where(take == 1, arriving, jnp.where(stay == 1, word, 0))

    off_i = off.astype(I32)
    shift = off_i & (LANES - 1)
    for k in range(nbits):
        word = jnp.where(bit(shift, k) == 1, pltpu.roll(word, 1 << k, axis=1), word)
    valid = bit(word, 2 * nbits) == 1
    wrapped = lane < shift
    blk_bits = nblk.bit_length() - 1
    q0 = (row >> blk_bits) * (cap // LANES) + (off_i >> nbits)
    digits = (jnp.where(valid, row & (nblk - 1), 0), jnp.where(valid, word & (LANES - 1), 0))

    def place(sel_rows, q):
        onehot_t = jnp.where(q == lane, 1.0, 0.0).astype(BF16)
        outs = []
        for dgt in digits:
            x = jnp.where(sel_rows, dgt, 0).astype(F32).astype(BF16)
            outs.append(lax.dot_general(onehot_t, x, (((0,), (0,)), ((), ())), preferred_element_type=F32))
        return outs[0] * LANES + outs[1]

    tok_ref[...] = (place(jnp.logical_not(wrapped), q0) + place(wrapped, q0 + 1)).astype(I32)


def _route(aff, cap):
    bsz, ne, t = aff.shape
    nblk = t // LANES
    nr = ne * nblk
    nq = ne * cap // LANES
    assert nq == LANES and cap % LANES == 0 and nblk & (nblk - 1) == 0 and nblk <= 256
    aff_r = aff.reshape(bsz, nr, LANES)
    ar = jnp.arange(LANES)
    upper = (ar[:, None] <= ar[None, :]).astype(BF16)
    ones = jnp.ones((LANES, LANES), BF16)
    rr = jnp.arange(nr)
    lower = ((rr[:, None] // nblk == rr[None, :] // nblk) & (rr[None, :] < rr[:, None])).astype(BF16)
    const = lambda a: pl.BlockSpec(a.shape, lambda b: (0,) * a.ndim)
    return pl.pallas_call(
        functools.partial(_route_body, cap=cap, ne=ne, nblk=nblk),
        grid=(bsz,),
        in_specs=[pl.BlockSpec((None, ne, t), lambda b: (b, 0, 0)),
                  pl.BlockSpec((None, nr, LANES), lambda b: (b, 0, 0)),
                  const(upper), const(ones), const(lower)],
        out_specs=pl.BlockSpec((None, nq, LANES), lambda b: (b, 0, 0)),
        out_shape=jax.ShapeDtypeStruct((bsz, nq, LANES), I32),
        compiler_params=_params(("arbitrary",), VMEM_LIMIT),
        name="route",
    )(aff, aff_r, upper, ones, lower)


MOE_HC = 512
MOE_TM = 512
ROW_UNROLL = 8


def _tile_rows(ref, first, n, s):
    return ref.at[pl.ds(first * SUBLANES + s, n, stride=SUBLANES), :]


def _gather_copy(h2_hbm, xs3, sem, token, slot):
    src = h2_hbm.at[pl.ds(pl.multiple_of(token * SUBLANES, SUBLANES), SUBLANES), :]
    dst = xs3.at[pl.ds(pl.multiple_of(slot * SUBLANES, SUBLANES), SUBLANES), :]
    return pltpu.make_async_copy(src, dst, sem)


def _moe_body(tok_ref, nxt_ref, h2_hbm, wg_ref, wu_ref, wd_ref, y_ref, xs3, xsb, wgb, wub, wdb, sem,
              *, bsz, t, cap, ne, nch):
    e = pl.program_id(0)
    c = pl.program_id(1)
    n_rows = bsz * cap
    n_tiles = n_rows // MOE_TM
    per_tile = n_rows // (nch * n_tiles)
    n_lane_groups = xsb.shape[-1] // LANES

    def fetch(ref, b, slot):
        _gather_copy(h2_hbm, xs3, sem, b * t + ref[b, 0, slot], b * cap + slot).start()

    def wait_staging():
        pltpu.make_async_copy(h2_hbm.at[pl.ds(0, n_rows * SUBLANES), :], xs3, sem).wait()

    @pl.when((e == 0) & (c == 0))
    def _first_expert():
        for b in range(bsz):
            def rows(g, carry):
                for k in range(ROW_UNROLL):
                    fetch(tok_ref, b, g * ROW_UNROLL + k)
                return carry

            lax.fori_loop(0, cap // ROW_UNROLL, rows, 0)

    def tiles(first_chunk):
        wgb[...] = wg_ref[...].astype(BF16)
        wub[...] = wu_ref[...].astype(BF16)
        wdb[...] = wd_ref[...].astype(BF16)
        for m in range(n_tiles):
            r0 = m * MOE_TM
            if first_chunk:
                xm = jnp.concatenate([_tile_rows(xs3, r0, MOE_TM, s)[...].astype(BF16)
                                      for s in range(n_lane_groups)], axis=-1)
                xsb[r0:r0 + MOE_TM, :] = xm
            else:
                xm = xsb[r0:r0 + MOE_TM, :]
            gate = jnp.dot(xm, wgb[...], preferred_element_type=F32)
            first = (c * n_tiles + m) * per_tile
            b = first // cap
            for k in range(per_tile):
                fetch(nxt_ref, b, first - b * cap + k)
            up = jnp.dot(xm, wub[...], preferred_element_type=F32)
            hid = (gate * jax.nn.sigmoid(gate) * up).astype(BF16)
            contrib = jnp.dot(hid, wdb[...], preferred_element_type=F32)
            if first_chunk:
                y_ref[r0:r0 + MOE_TM, :] = contrib
            else:
                y_ref[r0:r0 + MOE_TM, :] += contrib

    @pl.when(c == 0)
    def _():
        wait_staging()
        tiles(True)

    @pl.when(c > 0)
    def _():
        tiles(False)

    @pl.when((e == ne - 1) & (c == nch - 1))
    def _drain():
        wait_staging()


def _moe(tok, h2_tiles, w_gate, w_up, w_down, bsz, t, cap):
    ne, d, dh = w_gate.shape
    nch = dh // MOE_HC
    n_rows = bsz * cap
    per_tile = n_rows // (nch * (n_rows // MOE_TM))
    assert cap % ROW_UNROLL == 0 and n_rows % MOE_TM == 0 and cap % per_tile == 0 and n_rows // nch <= MOE_TM
    assert d == SUBLANES * LANES
    tok_spec = lambda ahead: pl.BlockSpec((bsz, None, 1, cap),
                                          lambda e, c: (0, jnp.minimum(e + ahead, ne - 1), 0, 0),
                                          memory_space=pltpu.SMEM)
    return pl.pallas_call(
        functools.partial(_moe_body, bsz=bsz, t=t, cap=cap, ne=ne, nch=nch),
        grid=(ne, nch),
        in_specs=[tok_spec(0), tok_spec(1),
                  pl.BlockSpec(memory_space=pl.ANY),
                  pl.BlockSpec((None, d, MOE_HC), lambda e, c: (e, 0, c)),
                  pl.BlockSpec((None, d, MOE_HC), lambda e, c: (e, 0, c)),
                  pl.BlockSpec((None, MOE_HC, d), lambda e, c: (e, c, 0))],
        out_specs=pl.BlockSpec((None, n_rows, d), lambda e, c: (e, 0, 0)),
        out_shape=jax.ShapeDtypeStruct((ne, n_rows, d), F32),
        scratch_shapes=[pltpu.VMEM((n_rows * SUBLANES, LANES), F32),
                        pltpu.VMEM((n_rows, d), BF16),
                        pltpu.VMEM((d, MOE_HC), BF16),
                        pltpu.VMEM((d, MOE_HC), BF16),
                        pltpu.VMEM((MOE_HC, d), BF16),
                        pltpu.SemaphoreType.DMA],
        compiler_params=_params(("arbitrary", "arbitrary"), VMEM_LIMIT),
        name="moe",
    )(tok, tok, h2_tiles, w_gate, w_up, w_down)


COMBINE_CHUNK = 128

def _combine_body(tok_ref, aff_ref, y_ref, x1_ref, g2_ref, gf_ref, o_ref, acc, ysc, *, cap, ne, tm):
    s = pl.program_id(1)

    @pl.when(s == 0)
    def _():
        acc[...] = jnp.zeros_like(acc)

    @pl.when(s < ne)
    def _scatter():
        def tile_of(row):
            start = row * SUBLANES
            return pl.ds(start if isinstance(start, int) else pl.multiple_of(start, SUBLANES), SUBLANES)

        for r0 in range(0, cap, COMBINE_CHUNK):
            for g in range(SUBLANES):
                _tile_rows(ysc, r0, COMBINE_CHUNK, g)[...] = y_ref[r0:r0 + COMBINE_CHUNK, g * LANES:(g + 1) * LANES]
            for base in range(r0, r0 + COMBINE_CHUNK, ROW_UNROLL):
                tokens = [tok_ref[0, base + k] for k in range(ROW_UNROLL)]
                new = [acc[tile_of(tk), :] + aff_ref[0, tk] * ysc[tile_of(base + k), :]
                       for k, tk in enumerate(tokens)]
                for tk, val in zip(tokens, new):
                    acc[tile_of(tk), :] = val

    @pl.when(s >= ne)
    def _finish():
        first = pl.multiple_of((s - ne) * tm, tm)
        moe = jnp.concatenate([_tile_rows(acc, first, tm, g)[...] for g in range(SUBLANES)], axis=-1)
        x = x1_ref[...] + g2_ref[...] * moe
        ms = jnp.mean(x * x, axis=-1, keepdims=True)
        o_ref[...] = x * lax.rsqrt(ms + EPS) * gf_ref[...]


def _combine(tok, aff, y, x1, mods4, g_final, cap, tm):
    bsz, t, d = x1.shape
    ne = aff.shape[1]
    nt = t // tm
    assert d == SUBLANES * LANES
    expert = lambda s: jnp.minimum(s, ne - 1)
    tile = lambda s: jnp.maximum(s - ne, 0)
    smem = lambda n: pl.BlockSpec((None, None, 1, n), lambda b, s: (b, expert(s), 0, 0), memory_space=pltpu.SMEM)
    token_tile = pl.BlockSpec((None, tm, d), lambda b, s: (b, tile(s), 0))
    return pl.pallas_call(
        functools.partial(_combine_body, cap=cap, ne=ne, tm=tm),
        grid=(bsz, ne + nt),
        in_specs=[smem(cap), smem(t),
                  pl.BlockSpec((None, cap, d), lambda b, s: (expert(s), b, 0)),
                  token_tile,
                  _mod_spec(5, lambda b: b)(d),
                  pl.BlockSpec((1, d), lambda b, s: (0, 0))],
        out_specs=token_tile,
        out_shape=jax.ShapeDtypeStruct((bsz, t, d), F32),
        scratch_shapes=[pltpu.VMEM((t * SUBLANES, LANES), F32),
                        pltpu.VMEM((cap * SUBLANES, LANES), F32)],
        compiler_params=_params(("arbitrary", "arbitrary"), VMEM_LIMIT),
        name="combine",
    )(tok, aff, y, x1, mods4, g_final)


def _layer(x, ctx, mods4, ctx_row, norm_mix_g, w_in, conv_w, conv_b, conv_ln_g, conv_ln_b, rpb, w_out,
           norm_ffn_g, w_router, w_gate, w_up, w_down, norm_final_g):
    bsz, t, d = x.shape
    cw = conv_w.shape[1]
    n_heads = rpb.shape[0]
    aw = (w_in.shape[1] - 2 * cw) // 3
    head_dim = aw // n_heads
    rows = t // GRID_W
    ne = w_router.shape[1]
    cap = EC_FACTOR * t // ne
    row = lambda a: a.reshape(1, -1)

    w_in_bf = w_in.astype(BF16)
    u, q, k, v = _inproj(x, row(norm_mix_g), mods4, w_in_bf, cw, aw, head_dim ** -0.5, tm=1024)
    kx, vx = _inproj_ctx(ctx, row(norm_mix_g), mods4, w_in_bf, 2 * cw + aw, aw, ctx_row)
    y_conv = _conv(u, conv_w, conv_b, conv_ln_g, conv_ln_b, tt=512)
    assert rows >= (rpb.shape[1] + 1) // 2
    y_attn = _attn(q, k, v, kx, vx, _bias_table(rpb))
    x1, h2_tiles, aff = _outproj(y_conv, y_attn, x, w_out.astype(BF16), mods4, row(norm_ffn_g),
                                 w_router, tm=1024)
    tok = _route(aff, cap).reshape(bsz, ne, 1, cap)
    y = _moe(tok, h2_tiles, w_gate, w_up, w_down, bsz, t, cap)
    return _combine(tok, aff.reshape(bsz, ne, 1, t), y, x1, mods4, row(norm_final_g), cap, tm=512)


def kernel(x, c, ctx, c_ctx, w_mod, b_mod, norm_mix_g, w_in, conv_w, conv_b, conv_ln_g, conv_ln_b, rpb,
           w_out, norm_ffn_g, w_router, w_gate, w_up, w_down, norm_final_g):
    bsz, t, d = x.shape
    depth = w_mod.shape[0]
    assert depth == 1
    assert bsz + 1 <= SUBLANES
    c8 = jnp.zeros((SUBLANES, d), F32).at[:bsz].set(c).at[bsz].set(c_ctx)
    mods4 = _adaln(c8, w_mod[0], b_mod[0].reshape(1, -1)).reshape(SUBLANES, N_MOD, 1, d)
    return _layer(x, ctx, mods4, bsz, norm_mix_g[0], w_in[0], conv_w[0], conv_b[0], conv_ln_g[0],
                  conv_ln_b[0], rpb[0], w_out[0], norm_ffn_g[0], w_router[0], w_gate[0], w_up[0],
                  w_down[0], norm_final_g)
```

```python
import functools

import jax
import jax.numpy as jnp
from jax import lax
from jax.experimental import pallas as pl
from jax.experimental.pallas import tpu as pltpu

F32 = jnp.float32
BF16 = jnp.bfloat16
I32 = jnp.int32
HIGHEST = lax.Precision.HIGHEST

GRID_W = 64
NA_COLS = 16
EC_FACTOR = 2
N_MOD = 6
EPS = 1e-6
NEG_INF = -1e30
LANES = 128
SUBLANES = 8
VMEM_LIMIT = 56 * 1024 * 1024

_NT = (((1,), (1,)), ((), ()))


def _params(sem, vmem=None):
    return pltpu.CompilerParams(dimension_semantics=sem, vmem_limit_bytes=vmem)


def _adaln_body(c_ref, w_ref, b_ref, o_ref):
    c = c_ref[...]
    s = c * jax.nn.sigmoid(c)
    o_ref[...] = jnp.dot(s, w_ref[...], preferred_element_type=F32, precision=HIGHEST) + b_ref[...]


def _adaln(c8, w_mod, b_mod):
    d, n = w_mod.shape
    bn = 512
    return pl.pallas_call(
        _adaln_body,
        grid=(n // bn,),
        in_specs=[pl.BlockSpec((SUBLANES, d), lambda j: (0, 0)),
                  pl.BlockSpec((d, bn), lambda j: (0, j)),
                  pl.BlockSpec((1, bn), lambda j: (0, j))],
        out_specs=pl.BlockSpec((SUBLANES, bn), lambda j: (0, j)),
        out_shape=jax.ShapeDtypeStruct((SUBLANES, n), F32),
        compiler_params=_params(("arbitrary",)),
        name="adaln",
    )(c8, w_mod, b_mod)


def _norm_mod(x, g, shift, scale):
    ms = jnp.mean(x * x, axis=-1, keepdims=True)
    h = x * lax.rsqrt(ms + EPS) * g
    return h * (1.0 + scale) + shift


def _inproj_body(x_ref, g_ref, sh_ref, sc_ref, w_ref, u_ref, q_ref, k_ref, v_ref, *, cw, aw, qscale):
    hb = _norm_mod(x_ref[...], g_ref[...], sh_ref[...], sc_ref[...]).astype(BF16)

    def proj(lo, width):
        return jnp.dot(hb, w_ref[:, lo:lo + width], preferred_element_type=F32)

    a = proj(0, cw)
    gate = proj(cw, cw)
    u_ref[...] = a * jax.nn.sigmoid(gate)
    q_ref[...] = (proj(2 * cw, aw) * qscale).astype(BF16)
    k_ref[...] = proj(2 * cw + aw, aw).astype(BF16)
    v_ref[...] = proj(2 * cw + 2 * aw, aw).astype(BF16)


def _inproj_kv_body(x_ref, g_ref, sh_ref, sc_ref, wk_ref, wv_ref, k_ref, v_ref):
    hb = _norm_mod(x_ref[...], g_ref[...], sh_ref[...], sc_ref[...]).astype(BF16)
    k_ref[...] = jnp.dot(hb, wk_ref[...], preferred_element_type=F32).astype(BF16)
    v_ref[...] = jnp.dot(hb, wv_ref[...], preferred_element_type=F32).astype(BF16)


def _mod_spec(slot, row_of):
    return lambda d: pl.BlockSpec((None, None, 1, d), lambda b, i: (row_of(b), slot, 0, 0))


def _inproj(x, g, mods4, w_bf, cw, aw, qscale, tm):
    bsz, t, d = x.shape
    tok = lambda w, dt: (pl.BlockSpec((None, tm, w), lambda b, i: (b, i, 0)), jax.ShapeDtypeStruct((bsz, t, w), dt))
    outs = [tok(cw, F32), tok(aw, BF16), tok(aw, BF16), tok(aw, BF16)]
    return pl.pallas_call(
        functools.partial(_inproj_body, cw=cw, aw=aw, qscale=qscale),
        grid=(bsz, t // tm),
        in_specs=[pl.BlockSpec((None, tm, d), lambda b, i: (b, i, 0)),
                  pl.BlockSpec((1, d), lambda b, i: (0, 0)),
                  _mod_spec(0, lambda b: b)(d),
                  _mod_spec(1, lambda b: b)(d),
                  pl.BlockSpec(w_bf.shape, lambda b, i: (0, 0))],
        out_specs=[o[0] for o in outs],
        out_shape=[o[1] for o in outs],
        compiler_params=_params(("arbitrary", "arbitrary"), VMEM_LIMIT),
        name="inproj",
    )(x, g, mods4, mods4, w_bf)


def _inproj_ctx(ctx, g, mods4, w_bf, k_lo, aw, ctx_row):
    bsz, l, d = ctx.shape
    assert k_lo % aw == 0
    spec = pl.BlockSpec((None, l, aw), lambda b, i: (b, 0, 0))
    shape = jax.ShapeDtypeStruct((bsz, l, aw), BF16)
    w_cols = lambda j: pl.BlockSpec((d, aw), lambda b, i: (0, k_lo // aw + j))
    return pl.pallas_call(
        _inproj_kv_body,
        grid=(bsz, 1),
        in_specs=[pl.BlockSpec((None, l, d), lambda b, i: (b, 0, 0)),
                  pl.BlockSpec((1, d), lambda b, i: (0, 0)),
                  _mod_spec(0, lambda b: ctx_row)(d),
                  _mod_spec(1, lambda b: ctx_row)(d),
                  w_cols(0), w_cols(1)],
        out_specs=[spec, spec],
        out_shape=[shape, shape],
        compiler_params=_params(("arbitrary", "arbitrary"), VMEM_LIMIT),
        name="inproj_ctx",
    )(ctx, g, mods4, mods4, w_bf, w_bf)


CONV_HALO = 16
CONV_RC = 64
CONV_CW = 256


def _conv_body(prev_ref, cur_ref, next_ref, w_ref, b_ref, lg_ref, lb_ref, o_ref, ext, ysc, *, width, tt):
    i = pl.program_id(1)
    n = pl.num_programs(1)
    ext[0:CONV_HALO, :] = jnp.where(i > 0, prev_ref[...], 0.0)
    ext[CONV_HALO:CONV_HALO + tt, :] = cur_ref[...]
    ext[CONV_HALO + tt:, :] = jnp.where(i < n - 1, next_ref[...], 0.0)
    lead = CONV_HALO - width // 2
    for rc in range(tt // CONV_RC):
        r0 = rc * CONV_RC
        for ch in range(cur_ref.shape[-1] // CONV_CW):
            cs = slice(ch * CONV_CW, (ch + 1) * CONV_CW)
            out = None
            for m in range(SUBLANES):
                nrow = CONV_RC + (SUBLANES if m else 0)
                part = None
                for j in range(width):
                    if (lead + j) % SUBLANES != m:
                        continue
                    a = r0 + lead + j - m
                    term = w_ref[j:j + 1, cs] * ext[a:a + nrow, cs]
                    part = term if part is None else part + term
                piece = part[m:m + CONV_RC]
                out = piece if out is None else out + piece
            ysc[r0:r0 + CONV_RC, cs] = out
    y = ysc[...] + b_ref[...]
    mu = jnp.mean(y, axis=-1, keepdims=True)
    yc = y - mu
    var = jnp.mean(yc * yc, axis=-1, keepdims=True)
    yn = yc * lax.rsqrt(var + EPS) * lg_ref[...] + lb_ref[...]
    o_ref[...] = (yn * jax.nn.sigmoid(yn)).astype(BF16)


def _conv(u, conv_w, conv_b, ln_g, ln_b, tt):
    bsz, t, cw = u.shape
    width = conv_w.shape[0]
    assert width // 2 < CONV_HALO and tt % CONV_HALO == 0 and tt % CONV_RC == 0 and cw % CONV_CW == 0
    hb = tt // CONV_HALO
    nh = t // CONV_HALO
    row = lambda a: a.reshape(1, cw)
    return pl.pallas_call(
        functools.partial(_conv_body, width=width, tt=tt),
        grid=(bsz, t // tt),
        in_specs=[pl.BlockSpec((None, CONV_HALO, cw), lambda b, i: (b, jnp.maximum(i * hb - 1, 0), 0)),
                  pl.BlockSpec((None, tt, cw), lambda b, i: (b, i, 0)),
                  pl.BlockSpec((None, CONV_HALO, cw), lambda b, i: (b, jnp.minimum((i + 1) * hb, nh - 1), 0)),
                  pl.BlockSpec((width, cw), lambda b, i: (0, 0)),
                  pl.BlockSpec((1, cw), lambda b, i: (0, 0)),
                  pl.BlockSpec((1, cw), lambda b, i: (0, 0)),
                  pl.BlockSpec((1, cw), lambda b, i: (0, 0))],
        out_specs=pl.BlockSpec((None, tt, cw), lambda b, i: (b, i, 0)),
        out_shape=jax.ShapeDtypeStruct((bsz, t, cw), BF16),
        scratch_shapes=[pltpu.VMEM((tt + 2 * CONV_HALO, cw), F32), pltpu.VMEM((tt, cw), F32)],
        compiler_params=_params(("arbitrary", "arbitrary")),
        name="conv",
    )(u, u, u, conv_w, row(conv_b), row(ln_g), row(ln_b))


ATT_RB = 8


def _bias_body(rpb_ref, onehot_ref, valid_ref, o_ref):
    t = jnp.dot(rpb_ref[...], onehot_ref[...], preferred_element_type=F32, precision=HIGHEST)
    o_ref[...] = jnp.where(valid_ref[...] > 0.5, t, NEG_INF)


def _bias_table(rpb):
    nh, ndr, ndc = rpb.shape
    qc = jnp.arange(GRID_W)[:, None]
    kc = jnp.arange(GRID_W)[None, :]
    c0 = jnp.clip(qc - NA_COLS // 2, 0, GRID_W - NA_COLS)
    valid = ((kc >= c0) & (kc < c0 + NA_COLS)).astype(F32).reshape(1, GRID_W * GRID_W)
    d_col = jnp.clip(kc - qc + (NA_COLS - 1), 0, 2 * NA_COLS - 2).reshape(1, GRID_W * GRID_W)
    ndc_pad = -(-ndc // SUBLANES) * SUBLANES
    onehot = (jnp.arange(ndc_pad)[:, None] == d_col).astype(F32)
    rpb2 = jnp.pad(rpb.reshape(nh * ndr, ndc), ((0, 0), (0, ndc_pad - ndc)))
    toeplitz = pl.pallas_call(
        _bias_body,
        out_shape=jax.ShapeDtypeStruct((nh * ndr, GRID_W * GRID_W), F32),
        name="bias",
    )(rpb2, onehot, valid).reshape(nh, ndr, GRID_W, GRID_W)
    return jnp.concatenate([toeplitz[:, :-1], toeplitz[:, 1:]], axis=-1)


def _attn_body(q_ref, kp_ref, kc_ref, kn_ref, vp_ref, vc_ref, vn_ref, kx_ref, vx_ref, bias_ref, o_ref,
               kwin, vwin, s_sc, p_sc, m_sc, l_sc, *, rows, kh, n_pairs):
    i = pl.program_id(1)
    blk = ATT_RB * GRID_W
    for w, (a, b) in enumerate(((kp_ref, vp_ref), (kc_ref, vc_ref), (kn_ref, vn_ref))):
        kwin[w * blk:(w + 1) * blk, :] = a[...]
        vwin[w * blk:(w + 1) * blk, :] = b[...]
    lo = lax.broadcasted_iota(I32, (GRID_W, LANES), 1) < LANES // 2
    band = kh * GRID_W

    def lane_groups(x, op):
        parts = [x[:, g * LANES:(g + 1) * LANES] for g in range(x.shape[-1] // LANES)]
        while len(parts) > 1:
            parts = [op(parts[a], parts[a + 1]) for a in range(0, len(parts) - 1, 2)] + parts[len(parts) & ~1:]
        return parts[0]

    def row_body(rr, carry):
        r = i * ATT_RB + rr
        r0 = jnp.clip(r - kh // 2, 0, rows - kh)
        delta = r - r0
        off = pl.multiple_of((r0 - (i - 1) * ATT_RB) * GRID_W, GRID_W)
        qoff = pl.multiple_of(rr * GRID_W, GRID_W)
        def scores(p):
            cols = slice(p * LANES, (p + 1) * LANES)
            qp = q_ref[pl.ds(qoff, GRID_W), cols]
            zeros = jnp.zeros_like(qp)
            q2 = jnp.concatenate([jnp.where(lo, qp, zeros), jnp.where(lo, zeros, qp)], axis=0)
            bias = jnp.concatenate(
                [jnp.concatenate([bias_ref[2 * p + s, 2 * g + (kh - 1) - delta] for g in range(kh // 2)], axis=-1)
                 for s in range(2)], axis=0)
            s_loc = lax.dot_general(q2, kwin[pl.ds(off, band), cols], _NT, preferred_element_type=F32) + bias
            s_ctx = lax.dot_general(q2, kx_ref[:, cols], _NT, preferred_element_type=F32)
            s_sc[p, :, 0:band] = s_loc
            s_sc[p, :, band:] = s_ctx
            m_sc[p] = jnp.maximum(lane_groups(s_loc, jnp.maximum), lane_groups(s_ctx, jnp.maximum))

        def numerators(p):
            m = jnp.max(m_sc[p], axis=-1, keepdims=True)
            e = jnp.exp(s_sc[p] - m)
            p_sc[p] = e.astype(BF16)
            l_sc[p] = lane_groups(e, jnp.add)

        def values(p):
            cols = slice(p * LANES, (p + 1) * LANES)
            o = (jnp.dot(p_sc[p, :, 0:band], vwin[pl.ds(off, band), cols], preferred_element_type=F32)
                 + jnp.dot(p_sc[p, :, band:], vx_ref[:, cols], preferred_element_type=F32))
            o = o / jnp.sum(l_sc[p], axis=-1, keepdims=True)
            o_ref[pl.ds(qoff, GRID_W), cols] = jnp.where(lo, o[0:GRID_W], o[GRID_W:]).astype(BF16)

        for stage in (scores, numerators, values):
            for p in range(n_pairs):
                stage(p)
        return carry

    lax.fori_loop(0, ATT_RB, row_body, 0, unroll=4)


def _attn(q, k, v, kx, vx, bias):
    bsz, t, aw = q.shape
    l = kx.shape[1]
    rows = t // GRID_W
    kh = (bias.shape[1] + 2) // 2
    nb = rows // ATT_RB
    blk = ATT_RB * GRID_W
    assert kh == ATT_RB and rows % ATT_RB == 0 and aw % LANES == 0 and 2 * GRID_W == LANES
    cur = pl.BlockSpec((None, blk, aw), lambda b, i: (b, i, 0))
    prev = pl.BlockSpec((None, blk, aw), lambda b, i: (b, jnp.maximum(i - 1, 0), 0))
    nxt = pl.BlockSpec((None, blk, aw), lambda b, i: (b, jnp.minimum(i + 1, nb - 1), 0))
    cx = pl.BlockSpec((None, l, aw), lambda b, i: (b, 0, 0))
    nh = bias.shape[0]
    nkeys = kh * GRID_W + l
    assert l % LANES == 0 and nh == 2 * (aw // LANES)
    return pl.pallas_call(
        functools.partial(_attn_body, rows=rows, kh=kh, n_pairs=aw // LANES),
        grid=(bsz, nb),
        in_specs=[cur, prev, cur, nxt, prev, cur, nxt, cx, cx,
                  pl.BlockSpec(bias.shape, lambda b, i: (0, 0, 0, 0))],
        out_specs=cur,
        out_shape=jax.ShapeDtypeStruct((bsz, t, aw), BF16),
        scratch_shapes=[pltpu.VMEM((3 * blk, aw), BF16), pltpu.VMEM((3 * blk, aw), BF16),
                        pltpu.VMEM((nh // 2, 2 * GRID_W, nkeys), F32), pltpu.VMEM((nh // 2, 2 * GRID_W, nkeys), BF16),
                        pltpu.VMEM((nh // 2, 2 * GRID_W, LANES), F32), pltpu.VMEM((nh // 2, 2 * GRID_W, LANES), F32)],
        compiler_params=_params(("arbitrary", "arbitrary"), VMEM_LIMIT),
        name="attn",
    )(q, k, k, k, v, v, v, kx, vx, bias)


def _outproj_body(yc_ref, ya_ref, x_ref, wo_ref, g1_ref, gf_ref, sh_ref, sc_ref, wr_ref,
                  x1_ref, aff_ref, *, cw, tm, ne):
    acc = (jnp.dot(yc_ref[...], wo_ref[0:cw, :], preferred_element_type=F32)
           + jnp.dot(ya_ref[...], wo_ref[cw:, :], preferred_element_type=F32))
    x1 = x_ref[...] + g1_ref[...] * acc
    for s in range(x1.shape[-1] // LANES):
        _tile_rows(x1_ref, 0, tm, s)[...] = x1[:, s * LANES:(s + 1) * LANES]
    h2 = _norm_mod(x1, gf_ref[...], sh_ref[...], sc_ref[...])
    hi = h2.astype(BF16)
    lo = (h2 - hi.astype(F32)).astype(BF16)
    part_hi = jnp.dot(hi, wr_ref[...], preferred_element_type=F32)
    part_lo = jnp.dot(lo, wr_ref[...], preferred_element_type=F32)
    logits = part_hi + pltpu.roll(part_hi, LANES - ne, axis=1) + part_lo
    logits = jnp.where(lax.broadcasted_iota(I32, logits.shape, 1) < ne, logits, NEG_INF)
    e = jnp.exp(logits - jnp.max(logits, axis=-1, keepdims=True))
    aff_ref[...] = (e / jnp.sum(e, axis=-1, keepdims=True)).T[0:ne, :]


def _outproj(y_conv, y_attn, x, wo_bf, mods4, g_ffn, w_router, tm):
    bsz, t, d = x.shape
    cw = y_conv.shape[-1]
    ne = w_router.shape[1]
    assert d == SUBLANES * LANES and 2 * ne <= LANES
    nt = t // tm
    w_hi = w_router.astype(BF16)
    w_lo = (w_router - w_hi.astype(F32)).astype(BF16)
    wr_t = jnp.zeros((d, LANES), BF16).at[:, 0:ne].set(w_hi).at[:, ne:2 * ne].set(w_lo)
    return pl.pallas_call(
        functools.partial(_outproj_body, cw=cw, tm=tm, ne=ne),
        grid=(bsz, nt),
        in_specs=[pl.BlockSpec((None, tm, cw), lambda b, i: (b, i, 0)),
                  pl.BlockSpec((None, tm, y_attn.shape[-1]), lambda b, i: (b, i, 0)),
                  pl.BlockSpec((None, tm, d), lambda b, i: (b, i, 0)),
                  pl.BlockSpec(wo_bf.shape, lambda b, i: (0, 0)),
                  _mod_spec(2, lambda b: b)(d),
                  pl.BlockSpec((1, d), lambda b, i: (0, 0)),
                  _mod_spec(3, lambda b: b)(d),
                  _mod_spec(4, lambda b: b)(d),
                  pl.BlockSpec(wr_t.shape, lambda b, i: (0, 0))],
        out_specs=[pl.BlockSpec((tm * SUBLANES, LANES), lambda b, i: (b * nt + i, 0)),
                   pl.BlockSpec((None, ne, tm), lambda b, i: (b, 0, i))],
        out_shape=[jax.ShapeDtypeStruct((bsz * t * SUBLANES, LANES), F32),
                   jax.ShapeDtypeStruct((bsz, ne, t), F32)],
        compiler_params=_params(("arbitrary", "arbitrary"), VMEM_LIMIT),
        name="outproj",
    )(y_conv, y_attn, x, wo_bf, mods4, g_ffn, mods4, mods4, wr_t)


def _route_body(affe_ref, affr_ref, u_ref, ones_ref, l_ref, tok_ref, *, cap, ne, nblk):
    aff_e = affe_ref[...]

    def search(it, bits):
        cand = bits | jnp.left_shift(jnp.int32(1), 30 - it)
        cnt = jnp.sum(jnp.where(aff_e >= pltpu.bitcast(cand, F32), 1.0, 0.0), axis=1, keepdims=True)
        return jnp.where(cnt >= cap, cand, bits)

    tau = pltpu.bitcast(lax.fori_loop(0, 31, search, jnp.zeros((ne, 1), I32)), F32)
    n_gt = jnp.sum(jnp.where(aff_e > tau, 1.0, 0.0), axis=1, keepdims=True)
    need = cap - n_gt

    def per_row(col):
        return jnp.concatenate([jnp.broadcast_to(col[e:e + 1, :], (nblk, 1)) for e in range(ne)], axis=0)

    tau_r = per_row(tau)
    need_r = per_row(need)
    kr = affr_ref[...]

    def scan(mf):
        mb = mf.astype(BF16)
        incl = jnp.dot(mb, u_ref[...], preferred_element_type=F32)
        tot = jnp.dot(mb, ones_ref[...], preferred_element_type=F32)
        off = jnp.dot(l_ref[...], tot.astype(BF16), preferred_element_type=F32)
        return incl, tot, off

    eq = jnp.where(kr == tau_r, 1.0, 0.0)
    incl_e, _, off_e = scan(eq)
    rank_eq = off_e + incl_e - eq
    sel = jnp.where((kr > tau_r) | ((eq > 0.5) & (rank_eq < need_r)), 1.0, 0.0)
    incl, _, off = scan(sel)

    nbits = LANES.bit_length() - 1
    lane = lax.broadcasted_iota(I32, kr.shape, 1)
    row = lax.broadcasted_iota(I32, kr.shape, 0)
    dist = lane - (incl.astype(I32) - 1)
    word = jnp.where(sel > 0.5, (1 << (2 * nbits)) | (dist << nbits) | lane, 0)

    def bit(x, pos):
        return (x >> pos) & 1

    for k in range(nbits):
        arriving = pltpu.roll(word, LANES - (1 << k), axis=1)
        take = bit(arriving, 2 * nbits) & bit(arriving, nbits + k)
        stay = bit(word, 2 * nbits) & (1 - bit(word, nbits + k))
        word = jnp.where(take == 1, arriving, jnp.where(stay == 1, word, 0))

    off_i = off.astype(I32)
    shift = off_i & (LANES - 1)
    for k in range(nbits):
        word = jnp.where(bit(shift, k) == 1, pltpu.roll(word, 1 << k, axis=1), word)
    valid = bit(word, 2 * nbits) == 1
    wrapped = lane < shift
    blk_bits = nblk.bit_length() - 1
    q0 = (row >> blk_bits) * (cap // LANES) + (off_i >> nbits)
    digits = (jnp.where(valid, row & (nblk - 1), 0), jnp.where(valid, word & (LANES - 1), 0))

    def place(sel_rows, q):
        onehot_t = jnp.where(q == lane, 1.0, 0.0).astype(BF16)
        outs = []
        for dgt in digits:
            x = jnp.where(sel_rows, dgt, 0).astype(F32).astype(BF16)
            outs.append(lax.dot_general(onehot_t, x, (((0,), (0,)), ((), ())), preferred_element_type=F32))
        return outs[0] * LANES + outs[1]

    tok_ref[...] = (place(jnp.logical_not(wrapped), q0) + place(wrapped, q0 + 1)).astype(I32)


def _route(aff, cap):
    bsz, ne, t = aff.shape
    nblk = t // LANES
    nr = ne * nblk
    nq = ne * cap // LANES
    assert nq == LANES and cap % LANES == 0 and nblk & (nblk - 1) == 0 and nblk <= 256
    aff_r = aff.reshape(bsz, nr, LANES)
    ar = jnp.arange(LANES)
    upper = (ar[:, None] <= ar[None, :]).astype(BF16)
    ones = jnp.ones((LANES, LANES), BF16)
    rr = jnp.arange(nr)
    lower = ((rr[:, None] // nblk == rr[None, :] // nblk) & (rr[None, :] < rr[:, None])).astype(BF16)
    const = lambda a: pl.BlockSpec(a.shape, lambda b: (0,) * a.ndim)
    return pl.pallas_call(
        functools.partial(_route_body, cap=cap, ne=ne, nblk=nblk),
        grid=(bsz,),
        in_specs=[pl.BlockSpec((None, ne, t), lambda b: (b, 0, 0)),
                  pl.BlockSpec((None, nr, LANES), lambda b: (b, 0, 0)),
                  const(upper), const(ones), const(lower)],
        out_specs=pl.BlockSpec((None, nq, LANES), lambda b: (b, 0, 0)),
        out_shape=jax.ShapeDtypeStruct((bsz, nq, LANES), I32),
        compiler_params=_params(("arbitrary",), VMEM_LIMIT),
        name="route",
    )(aff, aff_r, upper, ones, lower)


MOE_HC = 512
MOE_TM = 512
ROW_UNROLL = 8


def _tile_rows(ref, first, n, s):
    return ref.at[pl.ds(first * SUBLANES + s, n, stride=SUBLANES), :]


def _gather_copy(x1_hbm, xs3, sem, token, slot):
    src = x1_hbm.at[pl.ds(pl.multiple_of(token * SUBLANES, SUBLANES), SUBLANES), :]
    dst = xs3.at[pl.ds(pl.multiple_of(slot * SUBLANES, SUBLANES), SUBLANES), :]
    return pltpu.make_async_copy(src, dst, sem)


def _moe_body(tok_ref, nxt_ref, x1_hbm, gf_ref, *refs, bsz, t, cap, ne, nch):
    mod_refs = refs[:2 * bsz]
    wg_ref, wu_ref, wd_ref, y_ref, xs3, xsb, wgb, wub, wdb, sem = refs[2 * bsz:]
    e = pl.program_id(0)
    c = pl.program_id(1)
    n_rows = bsz * cap
    n_tiles = n_rows // MOE_TM
    per_tile = n_rows // (nch * n_tiles)
    n_lane_groups = xsb.shape[-1] // LANES

    def fetch(ref, b, slot):
        _gather_copy(x1_hbm, xs3, sem, b * t + ref[b, 0, slot], b * cap + slot).start()

    def wait_staging():
        pltpu.make_async_copy(x1_hbm.at[pl.ds(0, n_rows * SUBLANES), :], xs3, sem).wait()

    @pl.when((e == 0) & (c == 0))
    def _first_expert():
        for b in range(bsz):
            def rows(g, carry):
                for k in range(ROW_UNROLL):
                    fetch(tok_ref, b, g * ROW_UNROLL + k)
                return carry

            lax.fori_loop(0, cap // ROW_UNROLL, rows, 0)

    def tiles(first_chunk):
        wgb[...] = wg_ref[...].astype(BF16)
        wub[...] = wu_ref[...].astype(BF16)
        wdb[...] = wd_ref[...].astype(BF16)
        for m in range(n_tiles):
            r0 = m * MOE_TM
            if first_chunk:
                x1 = jnp.concatenate([_tile_rows(xs3, r0, MOE_TM, s)[...] for s in range(n_lane_groups)], axis=-1)
                shift, scale = mod_refs[2 * (r0 // cap)], mod_refs[2 * (r0 // cap) + 1]
                xm = _norm_mod(x1, gf_ref[...], shift[...], scale[...]).astype(BF16)
                xsb[r0:r0 + MOE_TM, :] = xm
            else:
                xm = xsb[r0:r0 + MOE_TM, :]
            gate = jnp.dot(xm, wgb[...], preferred_element_type=F32)
            first = (c * n_tiles + m) * per_tile
            b = first // cap
            for k in range(per_tile):
                fetch(nxt_ref, b, first - b * cap + k)
            up = jnp.dot(xm, wub[...], preferred_element_type=F32)
            hid = (gate * jax.nn.sigmoid(gate) * up).astype(BF16)
            contrib = jnp.dot(hid, wdb[...], preferred_element_type=F32)
            if first_chunk:
                y_ref[r0:r0 + MOE_TM, :] = contrib
            else:
                y_ref[r0:r0 + MOE_TM, :] += contrib

    @pl.when(c == 0)
    def _():
        wait_staging()
        tiles(True)

    @pl.when(c > 0)
    def _():
        tiles(False)

    @pl.when((e == ne - 1) & (c == nch - 1))
    def _drain():
        wait_staging()


def _moe(tok, x1_tiles, g_ffn, mods4, w_gate, w_up, w_down, bsz, t, cap):
    ne, d, dh = w_gate.shape
    nch = dh // MOE_HC
    n_rows = bsz * cap
    per_tile = n_rows // (nch * (n_rows // MOE_TM))
    assert cap % ROW_UNROLL == 0 and n_rows % MOE_TM == 0 and cap % per_tile == 0 and n_rows // nch <= MOE_TM
    assert d == SUBLANES * LANES and cap % MOE_TM == 0
    tok_spec = lambda ahead: pl.BlockSpec((bsz, None, 1, cap),
                                          lambda e, c: (0, jnp.minimum(e + ahead, ne - 1), 0, 0),
                                          memory_space=pltpu.SMEM)
    return pl.pallas_call(
        functools.partial(_moe_body, bsz=bsz, t=t, cap=cap, ne=ne, nch=nch),
        grid=(ne, nch),
        in_specs=[tok_spec(0), tok_spec(1),
                  pl.BlockSpec(memory_space=pl.ANY),
                  pl.BlockSpec((1, d), lambda e, c: (0, 0)),
                  *[pl.BlockSpec((None, None, 1, d), lambda e, c, b=b, slot=slot: (b, slot, 0, 0))
                    for b in range(bsz) for slot in (3, 4)],
                  pl.BlockSpec((None, d, MOE_HC), lambda e, c: (e, 0, c)),
                  pl.BlockSpec((None, d, MOE_HC), lambda e, c: (e, 0, c)),
                  pl.BlockSpec((None, MOE_HC, d), lambda e, c: (e, c, 0))],
        out_specs=pl.BlockSpec((None, n_rows, d), lambda e, c: (e, 0, 0)),
        out_shape=jax.ShapeDtypeStruct((ne, n_rows, d), F32),
        scratch_shapes=[pltpu.VMEM((n_rows * SUBLANES, LANES), F32),
                        pltpu.VMEM((n_rows, d), BF16),
                        pltpu.VMEM((d, MOE_HC), BF16),
                        pltpu.VMEM((d, MOE_HC), BF16),
                        pltpu.VMEM((MOE_HC, d), BF16),
                        pltpu.SemaphoreType.DMA],
        compiler_params=_params(("arbitrary", "arbitrary"), VMEM_LIMIT),
        name="moe",
    )(tok, tok, x1_tiles, g_ffn, *[mods4] * (2 * bsz), w_gate, w_up, w_down)


COMBINE_CHUNK = 128

def _combine_body(tok_ref, aff_ref, y_ref, x1_ref, g2_ref, gf_ref, o_ref, acc, ysc, *, cap, ne, tm):
    s = pl.program_id(1)

    @pl.when(s == 0)
    def _():
        acc[...] = jnp.zeros_like(acc)

    @pl.when(s < ne)
    def _scatter():
        def tile_of(row):
            start = row * SUBLANES
            return pl.ds(start if isinstance(start, int) else pl.multiple_of(start, SUBLANES), SUBLANES)

        for r0 in range(0, cap, COMBINE_CHUNK):
            for g in range(SUBLANES):
                _tile_rows(ysc, r0, COMBINE_CHUNK, g)[...] = y_ref[r0:r0 + COMBINE_CHUNK, g * LANES:(g + 1) * LANES]
            for base in range(r0, r0 + COMBINE_CHUNK, ROW_UNROLL):
                tokens = [tok_ref[0, base + k] for k in range(ROW_UNROLL)]
                new = [acc[tile_of(tk), :] + aff_ref[0, tk] * ysc[tile_of(base + k), :]
                       for k, tk in enumerate(tokens)]
                for tk, val in zip(tokens, new):
                    acc[tile_of(tk), :] = val

    @pl.when(s >= ne)
    def _finish():
        first = pl.multiple_of((s - ne) * tm, tm)
        moe = jnp.concatenate([_tile_rows(acc, first, tm, g)[...] for g in range(SUBLANES)], axis=-1)
        x1 = jnp.concatenate([_tile_rows(x1_ref, 0, tm, g)[...] for g in range(SUBLANES)], axis=-1)
        x = x1 + g2_ref[...] * moe
        ms = jnp.mean(x * x, axis=-1, keepdims=True)
        o_ref[...] = x * lax.rsqrt(ms + EPS) * gf_ref[...]


def _combine(tok, aff, y, x1_tiles, mods4, g_final, cap, tm):
    bsz, ne, _, t = aff.shape
    d = y.shape[-1]
    nt = t // tm
    assert d == SUBLANES * LANES and x1_tiles.shape == (bsz * t * SUBLANES, LANES)
    expert = lambda s: jnp.minimum(s, ne - 1)
    tile = lambda s: jnp.maximum(s - ne, 0)
    smem = lambda n: pl.BlockSpec((None, None, 1, n), lambda b, s: (b, expert(s), 0, 0), memory_space=pltpu.SMEM)
    token_tile = pl.BlockSpec((None, tm, d), lambda b, s: (b, tile(s), 0))
    return pl.pallas_call(
        functools.partial(_combine_body, cap=cap, ne=ne, tm=tm),
        grid=(bsz, ne + nt),
        in_specs=[smem(cap), smem(t),
                  pl.BlockSpec((None, cap, d), lambda b, s: (expert(s), b, 0)),
                  pl.BlockSpec((tm * SUBLANES, LANES), lambda b, s: (b * nt + tile(s), 0)),
                  _mod_spec(5, lambda b: b)(d),
                  pl.BlockSpec((1, d), lambda b, s: (0, 0))],
        out_specs=token_tile,
        out_shape=jax.ShapeDtypeStruct((bsz, t, d), F32),
        scratch_shapes=[pltpu.VMEM((t * SUBLANES, LANES), F32),
                        pltpu.VMEM((cap * SUBLANES, LANES), F32)],
        compiler_params=_params(("arbitrary", "arbitrary"), VMEM_LIMIT),
        name="combine",
    )(tok, aff, y, x1_tiles, mods4, g_final)


def _layer(x, ctx, mods4, ctx_row, norm_mix_g, w_in, conv_w, conv_b, conv_ln_g, conv_ln_b, rpb, w_out,
           norm_ffn_g, w_router, w_gate, w_up, w_down, norm_final_g):
    bsz, t, d = x.shape
    cw = conv_w.shape[1]
    n_heads = rpb.shape[0]
    aw = (w_in.shape[1] - 2 * cw) // 3
    head_dim = aw // n_heads
    rows = t // GRID_W
    ne = w_router.shape[1]
    cap = EC_FACTOR * t // ne
    row = lambda a: a.reshape(1, -1)

    w_in_bf = w_in.astype(BF16)
    u, q, k, v = _inproj(x, row(norm_mix_g), mods4, w_in_bf, cw, aw, head_dim ** -0.5, tm=1024)
    kx, vx = _inproj_ctx(ctx, row(norm_mix_g), mods4, w_in_bf, 2 * cw + aw, aw, ctx_row)
    y_conv = _conv(u, conv_w, conv_b, conv_ln_g, conv_ln_b, tt=512)
    assert rows >= (rpb.shape[1] + 1) // 2
    y_attn = _attn(q, k, v, kx, vx, _bias_table(rpb))
    x1_tiles, aff = _outproj(y_conv, y_attn, x, w_out.astype(BF16), mods4, row(norm_ffn_g),
                             w_router, tm=1024)
    tok = _route(aff, cap).reshape(bsz, ne, 1, cap)
    y = _moe(tok, x1_tiles, row(norm_ffn_g), mods4, w_gate, w_up, w_down, bsz, t, cap)
    return _combine(tok, aff.reshape(bsz, ne, 1, t), y, x1_tiles, mods4, row(norm_final_g), cap, tm=512)


def kernel(x, c, ctx, c_ctx, w_mod, b_mod, norm_mix_g, w_in, conv_w, conv_b, conv_ln_g, conv_ln_b, rpb,
           w_out, norm_ffn_g, w_router, w_gate, w_up, w_down, norm_final_g):
    bsz, t, d = x.shape
    depth = w_mod.shape[0]
    assert depth == 1
    assert bsz + 1 <= SUBLANES
    c8 = jnp.zeros((SUBLANES, d), F32).at[:bsz].set(c).at[bsz].set(c_ctx)
    mods4 = _adaln(c8, w_mod[0], b_mod[0].reshape(1, -1)).reshape(SUBLANES, N_MOD, 1, d)
    return _layer(x, ctx, mods4, bsz, norm_mix_g[0], w_in[0], conv_w[0], conv_b[0], conv_ln_g[0],
                  conv_ln_b[0], rpb[0], w_out[0], norm_ffn_g[0], w_router[0], w_gate[0], w_up[0],
                  w_down[0], norm_final_g)
```

```python
import functools

import jax
import jax.numpy as jnp
import numpy as np
from jax import lax
from jax.experimental import pallas as pl
from jax.experimental.pallas import tpu as pltpu

F32 = jnp.float32
BF16 = jnp.bfloat16
I32 = jnp.int32
HIGHEST = lax.Precision.HIGHEST

GRID_W = 64
NA_COLS = 16
EC_FACTOR = 2
N_MOD = 6
EPS = 1e-6
NEG_INF = -1e30
LANES = 128
SUBLANES = 8
VMEM_LIMIT = 56 * 1024 * 1024

_NT = (((1,), (1,)), ((), ()))


def _params(sem, vmem=None):
    return pltpu.CompilerParams(dimension_semantics=sem, vmem_limit_bytes=vmem)


def _tile_rows(ref, first, n, s):
    return ref.at[pl.ds(first * SUBLANES + s, n, stride=SUBLANES), :]


def _adaln_body(c_ref, w_ref, b_ref, o_ref):
    c = c_ref[...]
    s = c * jax.nn.sigmoid(c)
    o_ref[...] = jnp.dot(s, w_ref[...], preferred_element_type=F32, precision=HIGHEST) + b_ref[...]


def _adaln(c8, w_mod, b_mod):
    d, n = w_mod.shape
    bn = 512
    return pl.pallas_call(
        _adaln_body,
        grid=(n // bn,),
        in_specs=[pl.BlockSpec((SUBLANES, d), lambda j: (0, 0)),
                  pl.BlockSpec((d, bn), lambda j: (0, j)),
                  pl.BlockSpec((1, bn), lambda j: (0, j))],
        out_specs=pl.BlockSpec((SUBLANES, bn), lambda j: (0, j)),
        out_shape=jax.ShapeDtypeStruct((SUBLANES, n), F32),
        compiler_params=_params(("arbitrary",)),
        name="adaln",
    )(c8, w_mod, b_mod)


def _norm_mod(x, g, shift, scale):
    ms = jnp.mean(x * x, axis=-1, keepdims=True)
    h = x * lax.rsqrt(ms + EPS) * g
    return h * (1.0 + scale) + shift


def _inproj_body(x_ref, g_ref, sh_ref, sc_ref, w_ref, u_ref, q_ref, k_ref, v_ref, *, cw, aw, qscale):
    hb = _norm_mod(x_ref[...], g_ref[...], sh_ref[...], sc_ref[...]).astype(BF16)

    def proj(lo, width):
        return jnp.dot(hb, w_ref[:, lo:lo + width], preferred_element_type=F32)

    a = proj(0, cw)
    gate = proj(cw, cw)
    u_ref[...] = a * jax.nn.sigmoid(gate)
    q_ref[...] = (proj(2 * cw, aw) * qscale).astype(BF16)
    k_ref[...] = proj(2 * cw + aw, aw).astype(BF16)
    v_ref[...] = proj(2 * cw + 2 * aw, aw).astype(BF16)


def _inproj_kv_body(x_ref, g_ref, sh_ref, sc_ref, wk_ref, wv_ref, k_ref, v_ref):
    hb = _norm_mod(x_ref[...], g_ref[...], sh_ref[...], sc_ref[...]).astype(BF16)
    k_ref[...] = jnp.dot(hb, wk_ref[...], preferred_element_type=F32).astype(BF16)
    v_ref[...] = jnp.dot(hb, wv_ref[...], preferred_element_type=F32).astype(BF16)


def _mod_spec(slot, row_of):
    return lambda d: pl.BlockSpec((None, None, 1, d), lambda b, i: (row_of(b), slot, 0, 0))


def _inproj(x, g, mods4, w_bf, cw, aw, qscale, tm):
    bsz, t, d = x.shape
    tok = lambda w, dt: (pl.BlockSpec((None, tm, w), lambda b, i: (b, i, 0)), jax.ShapeDtypeStruct((bsz, t, w), dt))
    outs = [tok(cw, F32), tok(aw, BF16), tok(aw, BF16), tok(aw, BF16)]
    return pl.pallas_call(
        functools.partial(_inproj_body, cw=cw, aw=aw, qscale=qscale),
        grid=(bsz, t // tm),
        in_specs=[pl.BlockSpec((None, tm, d), lambda b, i: (b, i, 0)),
                  pl.BlockSpec((1, d), lambda b, i: (0, 0)),
                  _mod_spec(0, lambda b: b)(d),
                  _mod_spec(1, lambda b: b)(d),
                  pl.BlockSpec(w_bf.shape, lambda b, i: (0, 0))],
        out_specs=[o[0] for o in outs],
        out_shape=[o[1] for o in outs],
        compiler_params=_params(("arbitrary", "arbitrary"), VMEM_LIMIT),
        name="inproj",
    )(x, g, mods4, mods4, w_bf)


def _inproj_ctx(ctx, g, mods4, w_bf, k_lo, aw, ctx_row):
    bsz, l, d = ctx.shape
    assert k_lo % aw == 0
    spec = pl.BlockSpec((None, l, aw), lambda b, i: (b, 0, 0))
    shape = jax.ShapeDtypeStruct((bsz, l, aw), BF16)
    w_cols = lambda j: pl.BlockSpec((d, aw), lambda b, i: (0, k_lo // aw + j))
    return pl.pallas_call(
        _inproj_kv_body,
        grid=(bsz, 1),
        in_specs=[pl.BlockSpec((None, l, d), lambda b, i: (b, 0, 0)),
                  pl.BlockSpec((1, d), lambda b, i: (0, 0)),
                  _mod_spec(0, lambda b: ctx_row)(d),
                  _mod_spec(1, lambda b: ctx_row)(d),
                  w_cols(0), w_cols(1)],
        out_specs=[spec, spec],
        out_shape=[shape, shape],
        compiler_params=_params(("arbitrary", "arbitrary"), VMEM_LIMIT),
        name="inproj_ctx",
    )(ctx, g, mods4, mods4, w_bf, w_bf)


CONV_HALO = 16
CONV_RC = 64
CONV_CW = 256


def _conv_body(prev_ref, cur_ref, next_ref, w_ref, b_ref, lg_ref, lb_ref, o_ref, ext, ysc, *, width, tt):
    i = pl.program_id(1)
    n = pl.num_programs(1)
    ext[0:CONV_HALO, :] = jnp.where(i > 0, prev_ref[...], 0.0)
    ext[CONV_HALO:CONV_HALO + tt, :] = cur_ref[...]
    ext[CONV_HALO + tt:, :] = jnp.where(i < n - 1, next_ref[...], 0.0)
    lead = CONV_HALO - width // 2
    for rc in range(tt // CONV_RC):
        r0 = rc * CONV_RC
        for ch in range(cur_ref.shape[-1] // CONV_CW):
            cs = slice(ch * CONV_CW, (ch + 1) * CONV_CW)
            out = None
            for m in range(SUBLANES):
                nrow = CONV_RC + (SUBLANES if m else 0)
                part = None
                for j in range(width):
                    if (lead + j) % SUBLANES != m:
                        continue
                    a = r0 + lead + j - m
                    term = w_ref[j:j + 1, cs] * ext[a:a + nrow, cs]
                    part = term if part is None else part + term
                piece = part[m:m + CONV_RC]
                out = piece if out is None else out + piece
            ysc[r0:r0 + CONV_RC, cs] = out
    y = ysc[...] + b_ref[...]
    mu = jnp.mean(y, axis=-1, keepdims=True)
    yc = y - mu
    var = jnp.mean(yc * yc, axis=-1, keepdims=True)
    yn = yc * lax.rsqrt(var + EPS) * lg_ref[...] + lb_ref[...]
    o_ref[...] = (yn * jax.nn.sigmoid(yn)).astype(BF16)


def _conv(u, conv_w, conv_b, ln_g, ln_b, tt):
    bsz, t, cw = u.shape
    width = conv_w.shape[0]
    assert width // 2 < CONV_HALO and tt % CONV_HALO == 0 and tt % CONV_RC == 0 and cw % CONV_CW == 0
    hb = tt // CONV_HALO
    nh = t // CONV_HALO
    row = lambda a: a.reshape(1, cw)
    return pl.pallas_call(
        functools.partial(_conv_body, width=width, tt=tt),
        grid=(bsz, t // tt),
        in_specs=[pl.BlockSpec((None, CONV_HALO, cw), lambda b, i: (b, jnp.maximum(i * hb - 1, 0), 0)),
                  pl.BlockSpec((None, tt, cw), lambda b, i: (b, i, 0)),
                  pl.BlockSpec((None, CONV_HALO, cw), lambda b, i: (b, jnp.minimum((i + 1) * hb, nh - 1), 0)),
                  pl.BlockSpec((width, cw), lambda b, i: (0, 0)),
                  pl.BlockSpec((1, cw), lambda b, i: (0, 0)),
                  pl.BlockSpec((1, cw), lambda b, i: (0, 0)),
                  pl.BlockSpec((1, cw), lambda b, i: (0, 0))],
        out_specs=pl.BlockSpec((None, tt, cw), lambda b, i: (b, i, 0)),
        out_shape=jax.ShapeDtypeStruct((bsz, t, cw), BF16),
        scratch_shapes=[pltpu.VMEM((tt + 2 * CONV_HALO, cw), F32), pltpu.VMEM((tt, cw), F32)],
        compiler_params=_params(("arbitrary", "arbitrary")),
        name="conv",
    )(u, u, u, conv_w, row(conv_b), row(ln_g), row(ln_b))


ATT_RB = 8


def _bias_body(rpb_ref, onehot_ref, valid_ref, o_ref):
    t = jnp.dot(rpb_ref[...], onehot_ref[...], preferred_element_type=F32, precision=HIGHEST)
    o_ref[...] = jnp.where(valid_ref[...] > 0.5, t, NEG_INF)


def _bias_table(rpb):
    nh, ndr, ndc = rpb.shape
    qc = np.arange(GRID_W)[:, None]
    kc = np.arange(GRID_W)[None, :]
    c0 = np.clip(qc - NA_COLS // 2, 0, GRID_W - NA_COLS)
    valid = ((kc >= c0) & (kc < c0 + NA_COLS)).astype(np.float32).reshape(1, GRID_W * GRID_W)
    d_col = np.clip(kc - qc + (NA_COLS - 1), 0, 2 * NA_COLS - 2).reshape(1, GRID_W * GRID_W)
    ndc_pad = -(-ndc // SUBLANES) * SUBLANES
    onehot = (np.arange(ndc_pad)[:, None] == d_col).astype(np.float32)
    rpb2 = jnp.pad(rpb.reshape(nh * ndr, ndc), ((0, 0), (0, ndc_pad - ndc)))
    toeplitz = pl.pallas_call(
        _bias_body,
        out_shape=jax.ShapeDtypeStruct((nh * ndr, GRID_W * GRID_W), F32),
        name="bias",
    )(rpb2, onehot, valid).reshape(nh, ndr, GRID_W, GRID_W)
    return jnp.concatenate([toeplitz[:, :-1], toeplitz[:, 1:]], axis=-1)


def _attn_body(q_ref, kp_ref, kc_ref, kn_ref, vp_ref, vc_ref, vn_ref, kx_ref, vx_ref, bias_ref, o_ref,
               kwin, vwin, s_sc, p_sc, m_sc, l_sc, *, rows, kh, n_pairs):
    i = pl.program_id(1)
    blk = ATT_RB * GRID_W
    for w, (a, b) in enumerate(((kp_ref, vp_ref), (kc_ref, vc_ref), (kn_ref, vn_ref))):
        kwin[w * blk:(w + 1) * blk, :] = a[...]
        vwin[w * blk:(w + 1) * blk, :] = b[...]
    lo = lax.broadcasted_iota(I32, (GRID_W, LANES), 1) < LANES // 2
    band = kh * GRID_W

    def lane_groups(x, op):
        parts = [x[:, g * LANES:(g + 1) * LANES] for g in range(x.shape[-1] // LANES)]
        while len(parts) > 1:
            parts = [op(parts[a], parts[a + 1]) for a in range(0, len(parts) - 1, 2)] + parts[len(parts) & ~1:]
        return parts[0]

    def row_body(rr, carry):
        r = i * ATT_RB + rr
        r0 = jnp.clip(r - kh // 2, 0, rows - kh)
        delta = r - r0
        off = pl.multiple_of((r0 - (i - 1) * ATT_RB) * GRID_W, GRID_W)
        qoff = pl.multiple_of(rr * GRID_W, GRID_W)
        def scores(p):
            cols = slice(p * LANES, (p + 1) * LANES)
            qp = q_ref[pl.ds(qoff, GRID_W), cols]
            zeros = jnp.zeros_like(qp)
            q2 = jnp.concatenate([jnp.where(lo, qp, zeros), jnp.where(lo, zeros, qp)], axis=0)
            bias = jnp.concatenate(
                [jnp.concatenate([bias_ref[2 * p + s, 2 * g + (kh - 1) - delta] for g in range(kh // 2)], axis=-1)
                 for s in range(2)], axis=0)
            s_loc = lax.dot_general(q2, kwin[pl.ds(off, band), cols], _NT, preferred_element_type=F32) + bias
            s_ctx = lax.dot_general(q2, kx_ref[:, cols], _NT, preferred_element_type=F32)
            s_sc[p, :, 0:band] = s_loc
            s_sc[p, :, band:] = s_ctx
            m_sc[p] = jnp.maximum(lane_groups(s_loc, jnp.maximum), lane_groups(s_ctx, jnp.maximum))

        def numerators(p):
            m = jnp.max(m_sc[p], axis=-1, keepdims=True)
            e = jnp.exp(s_sc[p] - m)
            p_sc[p] = e.astype(BF16)
            l_sc[p] = lane_groups(e, jnp.add)

        def values(p):
            cols = slice(p * LANES, (p + 1) * LANES)
            o = (jnp.dot(p_sc[p, :, 0:band], vwin[pl.ds(off, band), cols], preferred_element_type=F32)
                 + jnp.dot(p_sc[p, :, band:], vx_ref[:, cols], preferred_element_type=F32))
            o = o / jnp.sum(l_sc[p], axis=-1, keepdims=True)
            o_ref[pl.ds(qoff, GRID_W), cols] = jnp.where(lo, o[0:GRID_W], o[GRID_W:]).astype(BF16)

        for stage in (scores, numerators, values):
            for p in range(n_pairs):
                stage(p)
        return carry

    lax.fori_loop(0, ATT_RB, row_body, 0, unroll=True)


def _attn(q, k, v, kx, vx, bias):
    bsz, t, aw = q.shape
    l = kx.shape[1]
    rows = t // GRID_W
    kh = (bias.shape[1] + 2) // 2
    nb = rows // ATT_RB
    blk = ATT_RB * GRID_W
    assert kh == ATT_RB and rows % ATT_RB == 0 and aw % LANES == 0 and 2 * GRID_W == LANES
    cur = pl.BlockSpec((None, blk, aw), lambda b, i: (b, i, 0))
    prev = pl.BlockSpec((None, blk, aw), lambda b, i: (b, jnp.maximum(i - 1, 0), 0))
    nxt = pl.BlockSpec((None, blk, aw), lambda b, i: (b, jnp.minimum(i + 1, nb - 1), 0))
    cx = pl.BlockSpec((None, l, aw), lambda b, i: (b, 0, 0))
    nh = bias.shape[0]
    nkeys = kh * GRID_W + l
    assert l % LANES == 0 and nh == 2 * (aw // LANES)
    return pl.pallas_call(
        functools.partial(_attn_body, rows=rows, kh=kh, n_pairs=aw // LANES),
        grid=(bsz, nb),
        in_specs=[cur, prev, cur, nxt, prev, cur, nxt, cx, cx,
                  pl.BlockSpec(bias.shape, lambda b, i: (0, 0, 0, 0))],
        out_specs=cur,
        out_shape=jax.ShapeDtypeStruct((bsz, t, aw), BF16),
        scratch_shapes=[pltpu.VMEM((3 * blk, aw), BF16), pltpu.VMEM((3 * blk, aw), BF16),
                        pltpu.VMEM((nh // 2, 2 * GRID_W, nkeys), F32), pltpu.VMEM((nh // 2, 2 * GRID_W, nkeys), BF16),
                        pltpu.VMEM((nh // 2, 2 * GRID_W, LANES), F32), pltpu.VMEM((nh // 2, 2 * GRID_W, LANES), F32)],
        compiler_params=_params(("arbitrary", "arbitrary"), VMEM_LIMIT),
        name="attn",
    )(q, k, k, k, v, v, v, kx, vx, bias)


def _outproj_body(yc_ref, ya_ref, x_ref, wo_ref, g1_ref, gf_ref, sh_ref, sc_ref, wr_ref,
                  x1_ref, h2_ref, aff_ref, *, cw, tm, ne):
    acc = (jnp.dot(yc_ref[...], wo_ref[0:cw, :], preferred_element_type=F32)
           + jnp.dot(ya_ref[...], wo_ref[cw:, :], preferred_element_type=F32))
    x1 = x_ref[...] + g1_ref[...] * acc
    x1_ref[...] = x1
    h2 = _norm_mod(x1, gf_ref[...], sh_ref[...], sc_ref[...])
    for s in range(h2.shape[-1] // LANES):
        _tile_rows(h2_ref, 0, tm, s)[...] = h2[:, s * LANES:(s + 1) * LANES]
    hi = h2.astype(BF16)
    lo = (h2 - hi.astype(F32)).astype(BF16)
    part_hi = jnp.dot(hi, wr_ref[...], preferred_element_type=F32)
    part_lo = jnp.dot(lo, wr_ref[...], preferred_element_type=F32)
    logits = part_hi + pltpu.roll(part_hi, LANES - ne, axis=1) + part_lo
    logits = jnp.where(lax.broadcasted_iota(I32, logits.shape, 1) < ne, logits, NEG_INF)
    e = jnp.exp(logits - jnp.max(logits, axis=-1, keepdims=True))
    aff_ref[...] = (e / jnp.sum(e, axis=-1, keepdims=True)).T[0:ne, :]


def _outproj(y_conv, y_attn, x, wo_bf, mods4, g_ffn, w_router, tm):
    bsz, t, d = x.shape
    cw = y_conv.shape[-1]
    ne = w_router.shape[1]
    assert d == SUBLANES * LANES and 2 * ne <= LANES
    nt = t // tm
    w_hi = w_router.astype(BF16)
    w_lo = (w_router - w_hi.astype(F32)).astype(BF16)
    wr_t = jnp.zeros((d, LANES), BF16).at[:, 0:ne].set(w_hi).at[:, ne:2 * ne].set(w_lo)
    return pl.pallas_call(
        functools.partial(_outproj_body, cw=cw, tm=tm, ne=ne),
        grid=(bsz, nt),
        in_specs=[pl.BlockSpec((None, tm, cw), lambda b, i: (b, i, 0)),
                  pl.BlockSpec((None, tm, y_attn.shape[-1]), lambda b, i: (b, i, 0)),
                  pl.BlockSpec((None, tm, d), lambda b, i: (b, i, 0)),
                  pl.BlockSpec(wo_bf.shape, lambda b, i: (0, 0)),
                  _mod_spec(2, lambda b: b)(d),
                  pl.BlockSpec((1, d), lambda b, i: (0, 0)),
                  _mod_spec(3, lambda b: b)(d),
                  _mod_spec(4, lambda b: b)(d),
                  pl.BlockSpec(wr_t.shape, lambda b, i: (0, 0))],
        out_specs=[pl.BlockSpec((None, tm, d), lambda b, i: (b, i, 0)),
                   pl.BlockSpec((tm * SUBLANES, LANES), lambda b, i: (b * nt + i, 0)),
                   pl.BlockSpec((None, ne, tm), lambda b, i: (b, 0, i))],
        out_shape=[jax.ShapeDtypeStruct((bsz, t, d), F32),
                   jax.ShapeDtypeStruct((bsz * t * SUBLANES, LANES), F32),
                   jax.ShapeDtypeStruct((bsz, ne, t), F32)],
        compiler_params=_params(("arbitrary", "arbitrary"), VMEM_LIMIT),
        name="outproj",
    )(y_conv, y_attn, x, wo_bf, mods4, g_ffn, mods4, mods4, wr_t)


def _route_body(affe_ref, affr_ref, u_ref, ones_ref, l_ref, tok_ref, *, cap, ne, nblk):
    aff_e = affe_ref[...]

    def search(it, bits):
        cand = bits | jnp.left_shift(jnp.int32(1), 30 - it)
        cnt = jnp.sum(jnp.where(aff_e >= pltpu.bitcast(cand, F32), 1.0, 0.0), axis=1, keepdims=True)
        return jnp.where(cnt >= cap, cand, bits)

    tau = pltpu.bitcast(lax.fori_loop(0, 31, search, jnp.zeros((ne, 1), I32)), F32)
    n_gt = jnp.sum(jnp.where(aff_e > tau, 1.0, 0.0), axis=1, keepdims=True)
    need = cap - n_gt

    def per_row(col):
        return jnp.concatenate([jnp.broadcast_to(col[e:e + 1, :], (nblk, 1)) for e in range(ne)], axis=0)

    tau_r = per_row(tau)
    need_r = per_row(need)
    kr = affr_ref[...]

    def scan(mf):
        mb = mf.astype(BF16)
        incl = jnp.dot(mb, u_ref[...], preferred_element_type=F32)
        tot = jnp.dot(mb, ones_ref[...], preferred_element_type=F32)
        off = jnp.dot(l_ref[...], tot.astype(BF16), preferred_element_type=F32)
        return incl, tot, off

    eq = jnp.where(kr == tau_r, 1.0, 0.0)
    incl_e, _, off_e = scan(eq)
    rank_eq = off_e + incl_e - eq
    sel = jnp.where((kr > tau_r) | ((eq > 0.5) & (rank_eq < need_r)), 1.0, 0.0)
    incl, _, off = scan(sel)

    nbits = LANES.bit_length() - 1
    lane = lax.broadcasted_iota(I32, kr.shape, 1)
    row = lax.broadcasted_iota(I32, kr.shape, 0)
    dist = lane - (incl.astype(I32) - 1)
    word = jnp.where(sel > 0.5, (1 << (2 * nbits)) | (dist << nbits) | lane, 0)

    def bit(x, pos):
        return (x >> pos) & 1

    for k in range(nbits):
        arriving = pltpu.roll(word, LANES - (1 << k), axis=1)
        take = bit(arriving, 2 * nbits) & bit(arriving, nbits + k)
        stay = bit(word, 2 * nbits) & (1 - bit(word, nbits + k))
        word = jnp.where(take == 1, arriving, jnp.where(stay == 1, word, 0))

    off_i = off.astype(I32)
    shift = off_i & (LANES - 1)
    for k in range(nbits):
        word = jnp.where(bit(shift, k) == 1, pltpu.roll(word, 1 << k, axis=1), word)
    valid = bit(word, 2 * nbits) == 1
    wrapped = lane < shift
    blk_bits = nblk.bit_length() - 1
    q0 = (row >> blk_bits) * (cap // LANES) + (off_i >> nbits)
    digits = (jnp.where(valid, row & (nblk - 1), 0), jnp.where(valid, word & (LANES - 1), 0))

    def place(sel_rows, q):
        onehot_t = jnp.where(q == lane, 1.0, 0.0).astype(BF16)
        outs = []
        for dgt in digits:
            x = jnp.where(sel_rows, dgt, 0).astype(F32).astype(BF16)
            outs.append(lax.dot_general(onehot_t, x, (((0,), (0,)), ((), ())), preferred_element_type=F32))
        return outs[0] * LANES + outs[1]

    tok_ref[...] = (place(jnp.logical_not(wrapped), q0) + place(wrapped, q0 + 1)).astype(I32)


def _route(aff, cap):
    bsz, ne, t = aff.shape
    nblk = t // LANES
    nr = ne * nblk
    nq = ne * cap // LANES
    assert nq == LANES and cap % LANES == 0 and nblk & (nblk - 1) == 0 and nblk <= 256
    aff_r = aff.reshape(bsz, nr, LANES)
    ar = np.arange(LANES)
    upper = jnp.asarray(ar[:, None] <= ar[None, :], BF16)
    ones = jnp.ones((LANES, LANES), BF16)
    rr = np.arange(nr)
    lower = jnp.asarray((rr[:, None] // nblk == rr[None, :] // nblk) & (rr[None, :] < rr[:, None]), BF16)
    const = lambda a: pl.BlockSpec(a.shape, lambda b: (0,) * a.ndim)
    return pl.pallas_call(
        functools.partial(_route_body, cap=cap, ne=ne, nblk=nblk),
        grid=(bsz,),
        in_specs=[pl.BlockSpec((None, ne, t), lambda b: (b, 0, 0)),
                  pl.BlockSpec((None, nr, LANES), lambda b: (b, 0, 0)),
                  const(upper), const(ones), const(lower)],
        out_specs=pl.BlockSpec((None, nq, LANES), lambda b: (b, 0, 0)),
        out_shape=jax.ShapeDtypeStruct((bsz, nq, LANES), I32),
        compiler_params=_params(("arbitrary",), VMEM_LIMIT),
        name="route",
    )(aff, aff_r, upper, ones, lower)


MOE_HC = 512
MOE_TM = 512
ROW_UNROLL = 8


def _gather_copy(h2_hbm, xs3, sem, token, slot):
    src = h2_hbm.at[pl.ds(pl.multiple_of(token * SUBLANES, SUBLANES), SUBLANES), :]
    dst = xs3.at[pl.ds(pl.multiple_of(slot * SUBLANES, SUBLANES), SUBLANES), :]
    return pltpu.make_async_copy(src, dst, sem)


def _moe_body(tok_ref, nxt_ref, h2_hbm, wg_ref, wu_ref, wd_ref, y_ref, xs3, xsb, wgb, wub, wdb, sem,
              *, bsz, t, cap, ne, nch):
    e = pl.program_id(0)
    c = pl.program_id(1)
    n_rows = bsz * cap
    n_tiles = n_rows // MOE_TM
    per_tile = n_rows // (nch * n_tiles)
    n_lane_groups = xsb.shape[-1] // LANES

    def fetch(ref, b, slot):
        _gather_copy(h2_hbm, xs3, sem, b * t + ref[b, 0, slot], b * cap + slot).start()

    def wait_staging():
        pltpu.make_async_copy(h2_hbm.at[pl.ds(0, n_rows * SUBLANES), :], xs3, sem).wait()

    @pl.when((e == 0) & (c == 0))
    def _first_expert():
        for b in range(bsz):
            def rows(g, carry):
                for k in range(ROW_UNROLL):
                    fetch(tok_ref, b, g * ROW_UNROLL + k)
                return carry

            lax.fori_loop(0, cap // ROW_UNROLL, rows, 0)

    def tiles(first_chunk):
        wgb[...] = wg_ref[...].astype(BF16)
        wub[...] = wu_ref[...].astype(BF16)
        wdb[...] = wd_ref[...].astype(BF16)
        for m in range(n_tiles):
            r0 = m * MOE_TM
            if first_chunk:
                xm = jnp.concatenate([_tile_rows(xs3, r0, MOE_TM, s)[...].astype(BF16)
                                      for s in range(n_lane_groups)], axis=-1)
                xsb[r0:r0 + MOE_TM, :] = xm
            else:
                xm = xsb[r0:r0 + MOE_TM, :]
            gate = jnp.dot(xm, wgb[...], preferred_element_type=F32)
            first = (c * n_tiles + m) * per_tile
            b = first // cap
            for k in range(per_tile):
                fetch(nxt_ref, b, first - b * cap + k)
            up = jnp.dot(xm, wub[...], preferred_element_type=F32)
            hid = (gate * jax.nn.sigmoid(gate) * up).astype(BF16)
            contrib = jnp.dot(hid, wdb[...], preferred_element_type=F32)
            if first_chunk:
                y_ref[r0:r0 + MOE_TM, :] = contrib
            else:
                y_ref[r0:r0 + MOE_TM, :] += contrib

    @pl.when(c == 0)
    def _():
        wait_staging()
        tiles(True)

    @pl.when(c > 0)
    def _():
        tiles(False)

    @pl.when((e == ne - 1) & (c == nch - 1))
    def _drain():
        wait_staging()


def _moe(tok, h2_tiles, w_gate, w_up, w_down, bsz, t, cap):
    ne, d, dh = w_gate.shape
    nch = dh // MOE_HC
    n_rows = bsz * cap
    per_tile = n_rows // (nch * (n_rows // MOE_TM))
    assert cap % ROW_UNROLL == 0 and n_rows % MOE_TM == 0 and cap % per_tile == 0 and n_rows // nch <= MOE_TM
    assert d == SUBLANES * LANES
    tok_spec = lambda ahead: pl.BlockSpec((bsz, None, 1, cap),
                                          lambda e, c: (0, jnp.minimum(e + ahead, ne - 1), 0, 0),
                                          memory_space=pltpu.SMEM)
    return pl.pallas_call(
        functools.partial(_moe_body, bsz=bsz, t=t, cap=cap, ne=ne, nch=nch),
        grid=(ne, nch),
        in_specs=[tok_spec(0), tok_spec(1),
                  pl.BlockSpec(memory_space=pl.ANY),
                  pl.BlockSpec((None, d, MOE_HC), lambda e, c: (e, 0, c)),
                  pl.BlockSpec((None, d, MOE_HC), lambda e, c: (e, 0, c)),
                  pl.BlockSpec((None, MOE_HC, d), lambda e, c: (e, c, 0))],
        out_specs=pl.BlockSpec((None, n_rows, d), lambda e, c: (e, 0, 0)),
        out_shape=jax.ShapeDtypeStruct((ne, n_rows, d), F32),
        scratch_shapes=[pltpu.VMEM((n_rows * SUBLANES, LANES), F32),
                        pltpu.VMEM((n_rows, d), BF16),
                        pltpu.VMEM((d, MOE_HC), BF16),
                        pltpu.VMEM((d, MOE_HC), BF16),
                        pltpu.VMEM((MOE_HC, d), BF16),
                        pltpu.SemaphoreType.DMA],
        compiler_params=_params(("arbitrary", "arbitrary"), VMEM_LIMIT),
        name="moe",
    )(tok, tok, h2_tiles, w_gate, w_up, w_down)


COMBINE_CHUNK = 128


def _combine_body(tok_ref, aff_ref, y_ref, x1_ref, g2_ref, gf_ref, o_ref, acc, ysc, *, cap, ne, tm):
    s = pl.program_id(1)

    @pl.when(s == 0)
    def _():
        acc[...] = jnp.zeros_like(acc)

    @pl.when(s < ne)
    def _scatter():
        def tile_of(row):
            start = row * SUBLANES
            return pl.ds(start if isinstance(start, int) else pl.multiple_of(start, SUBLANES), SUBLANES)

        for r0 in range(0, cap, COMBINE_CHUNK):
            for g in range(SUBLANES):
                _tile_rows(ysc, r0, COMBINE_CHUNK, g)[...] = y_ref[r0:r0 + COMBINE_CHUNK, g * LANES:(g + 1) * LANES]
            for base in range(r0, r0 + COMBINE_CHUNK, ROW_UNROLL):
                tokens = [tok_ref[0, base + k] for k in range(ROW_UNROLL)]
                new = [acc[tile_of(tk), :] + aff_ref[0, tk] * ysc[tile_of(base + k), :]
                       for k, tk in enumerate(tokens)]
                for tk, val in zip(tokens, new):
                    acc[tile_of(tk), :] = val

    @pl.when(s >= ne)
    def _finish():
        first = pl.multiple_of((s - ne) * tm, tm)
        moe = jnp.concatenate([_tile_rows(acc, first, tm, g)[...] for g in range(SUBLANES)], axis=-1)
        x = x1_ref[...] + g2_ref[...] * moe
        ms = jnp.mean(x * x, axis=-1, keepdims=True)
        o_ref[...] = x * lax.rsqrt(ms + EPS) * gf_ref[...]


def _combine(tok, aff, y, x1, mods4, g_final, cap, tm):
    bsz, t, d = x1.shape
    ne = aff.shape[1]
    nt = t // tm
    assert d == SUBLANES * LANES
    expert = lambda s: jnp.minimum(s, ne - 1)
    tile = lambda s: jnp.maximum(s - ne, 0)
    smem = lambda n: pl.BlockSpec((None, None, 1, n), lambda b, s: (b, expert(s), 0, 0), memory_space=pltpu.SMEM)
    token_tile = pl.BlockSpec((None, tm, d), lambda b, s: (b, tile(s), 0))
    return pl.pallas_call(
        functools.partial(_combine_body, cap=cap, ne=ne, tm=tm),
        grid=(bsz, ne + nt),
        in_specs=[smem(cap), smem(t),
                  pl.BlockSpec((None, cap, d), lambda b, s: (expert(s), b, 0)),
                  token_tile,
                  _mod_spec(5, lambda b: b)(d),
                  pl.BlockSpec((1, d), lambda b, s: (0, 0))],
        out_specs=token_tile,
        out_shape=jax.ShapeDtypeStruct((bsz, t, d), F32),
        scratch_shapes=[pltpu.VMEM((t * SUBLANES, LANES), F32),
                        pltpu.VMEM((cap * SUBLANES, LANES), F32)],
        compiler_params=_params(("arbitrary", "arbitrary"), VMEM_LIMIT),
        name="combine",
    )(tok, aff, y, x1, mods4, g_final)


def _layer(x, ctx, mods4, ctx_row, norm_mix_g, w_in, conv_w, conv_b, conv_ln_g, conv_ln_b, rpb, w_out,
           norm_ffn_g, w_router, w_gate, w_up, w_down, norm_final_g):
    bsz, t, d = x.shape
    cw = conv_w.shape[1]
    n_heads = rpb.shape[0]
    aw = (w_in.shape[1] - 2 * cw) // 3
    head_dim = aw // n_heads
    rows = t // GRID_W
    ne = w_router.shape[1]
    cap = EC_FACTOR * t // ne
    row = lambda a: a.reshape(1, -1)

    w_in_bf = w_in.astype(BF16)
    u, q, k, v = _inproj(x, row(norm_mix_g), mods4, w_in_bf, cw, aw, head_dim ** -0.5, tm=1024)
    kx, vx = _inproj_ctx(ctx, row(norm_mix_g), mods4, w_in_bf, 2 * cw + aw, aw, ctx_row)
    y_conv = _conv(u, conv_w, conv_b, conv_ln_g, conv_ln_b, tt=512)
    assert rows >= (rpb.shape[1] + 1) // 2
    y_attn = _attn(q, k, v, kx, vx, _bias_table(rpb))
    x1, h2_tiles, aff = _outproj(y_conv, y_attn, x, w_out.astype(BF16), mods4, row(norm_ffn_g),
                                 w_router, tm=1024)
    tok = _route(aff, cap).reshape(bsz, ne, 1, cap)
    y = _moe(tok, h2_tiles, w_gate, w_up, w_down, bsz, t, cap)
    return _combine(tok, aff.reshape(bsz, ne, 1, t), y, x1, mods4, row(norm_final_g), cap, tm=512)


def kernel(x, c, ctx, c_ctx, w_mod, b_mod, norm_mix_g, w_in, conv_w, conv_b, conv_ln_g, conv_ln_b, rpb,
           w_out, norm_ffn_g, w_router, w_gate, w_up, w_down, norm_final_g):
    bsz, t, d = x.shape
    depth = w_mod.shape[0]
    assert depth == 1
    assert bsz + 1 <= SUBLANES
    c8 = jnp.zeros((SUBLANES, d), F32).at[:bsz].set(c).at[bsz].set(c_ctx)
    mods4 = _adaln(c8, w_mod[0], b_mod[0].reshape(1, -1)).reshape(SUBLANES, N_MOD, 1, d)
    return _layer(x, ctx, mods4, bsz, norm_mix_g[0], w_in[0], conv_w[0], conv_b[0], conv_ln_g[0],
                  conv_ln_b[0], rpb[0], w_out[0], norm_ffn_g[0], w_router[0], w_gate[0], w_up[0],
                  w_down[0], norm_final_g)
```

```python
import functools

import jax
import jax.numpy as jnp
import numpy as np
from jax import lax
from jax.experimental import pallas as pl
from jax.experimental.pallas import tpu as pltpu

F32 = jnp.float32
BF16 = jnp.bfloat16
I32 = jnp.int32
HIGHEST = lax.Precision.HIGHEST

GRID_W = 64
NA_COLS = 16
EC_FACTOR = 2
N_MOD = 6
EPS = 1e-6
NEG_INF = -1e30
LANES = 128
SUBLANES = 8
VMEM_LIMIT = 56 * 1024 * 1024

_NT = (((1,), (1,)), ((), ()))


def _params(sem, vmem=None):
    return pltpu.CompilerParams(dimension_semantics=sem, vmem_limit_bytes=vmem)


def _tile_rows(ref, first, n, s):
    return ref.at[pl.ds(first * SUBLANES + s, n, stride=SUBLANES), :]


def _adaln_body(c_ref, w_ref, b_ref, o_ref):
    c = c_ref[...]
    s = c * jax.nn.sigmoid(c)
    o_ref[...] = jnp.dot(s, w_ref[...], preferred_element_type=F32, precision=HIGHEST) + b_ref[...]


def _adaln(c8, w_mod, b_mod):
    d, n = w_mod.shape
    bn = 512
    return pl.pallas_call(
        _adaln_body,
        grid=(n // bn,),
        in_specs=[pl.BlockSpec((SUBLANES, d), lambda j: (0, 0)),
                  pl.BlockSpec((d, bn), lambda j: (0, j)),
                  pl.BlockSpec((1, bn), lambda j: (0, j))],
        out_specs=pl.BlockSpec((SUBLANES, bn), lambda j: (0, j)),
        out_shape=jax.ShapeDtypeStruct((SUBLANES, n), F32),
        compiler_params=_params(("arbitrary",)),
        name="adaln",
    )(c8, w_mod, b_mod)


def _norm_mod(x, g, shift, scale):
    ms = jnp.mean(x * x, axis=-1, keepdims=True)
    h = x * lax.rsqrt(ms + EPS) * g
    return h * (1.0 + scale) + shift


def _inproj_body(x_ref, g_ref, sh_ref, sc_ref, w_ref, u_ref, q_ref, k_ref, v_ref, *, cw, aw, qscale):
    hb = _norm_mod(x_ref[...], g_ref[...], sh_ref[...], sc_ref[...]).astype(BF16)

    def proj(lo, width):
        return jnp.dot(hb, w_ref[:, lo:lo + width], preferred_element_type=F32)

    a = proj(0, cw)
    gate = proj(cw, cw)
    u_ref[...] = a * jax.nn.sigmoid(gate)
    q_ref[...] = (proj(2 * cw, aw) * qscale).astype(BF16)
    k_ref[...] = proj(2 * cw + aw, aw).astype(BF16)
    v_ref[...] = proj(2 * cw + 2 * aw, aw).astype(BF16)


def _inproj_kv_body(x_ref, g_ref, sh_ref, sc_ref, wk_ref, wv_ref, k_ref, v_ref):
    hb = _norm_mod(x_ref[...], g_ref[...], sh_ref[...], sc_ref[...]).astype(BF16)
    k_ref[...] = jnp.dot(hb, wk_ref[...], preferred_element_type=F32).astype(BF16)
    v_ref[...] = jnp.dot(hb, wv_ref[...], preferred_element_type=F32).astype(BF16)


def _mod_spec(slot, row_of):
    return lambda d: pl.BlockSpec((None, None, 1, d), lambda b, i: (row_of(b), slot, 0, 0))


def _inproj(x, g, mods4, w_bf, cw, aw, qscale, tm):
    bsz, t, d = x.shape
    tok = lambda w, dt: (pl.BlockSpec((None, tm, w), lambda b, i: (b, i, 0)), jax.ShapeDtypeStruct((bsz, t, w), dt))
    outs = [tok(cw, F32), tok(aw, BF16), tok(aw, BF16), tok(aw, BF16)]
    return pl.pallas_call(
        functools.partial(_inproj_body, cw=cw, aw=aw, qscale=qscale),
        grid=(bsz, t // tm),
        in_specs=[pl.BlockSpec((None, tm, d), lambda b, i: (b, i, 0)),
                  pl.BlockSpec((1, d), lambda b, i: (0, 0)),
                  _mod_spec(0, lambda b: b)(d),
                  _mod_spec(1, lambda b: b)(d),
                  pl.BlockSpec(w_bf.shape, lambda b, i: (0, 0))],
        out_specs=[o[0] for o in outs],
        out_shape=[o[1] for o in outs],
        compiler_params=_params(("arbitrary", "arbitrary"), VMEM_LIMIT),
        name="inproj",
    )(x, g, mods4, mods4, w_bf)


def _inproj_ctx(ctx, g, mods4, w_bf, k_lo, aw, ctx_row):
    bsz, l, d = ctx.shape
    assert k_lo % aw == 0
    spec = pl.BlockSpec((None, l, aw), lambda b, i: (b, 0, 0))
    shape = jax.ShapeDtypeStruct((bsz, l, aw), BF16)
    w_cols = lambda j: pl.BlockSpec((d, aw), lambda b, i: (0, k_lo // aw + j))
    return pl.pallas_call(
        _inproj_kv_body,
        grid=(bsz, 1),
        in_specs=[pl.BlockSpec((None, l, d), lambda b, i: (b, 0, 0)),
                  pl.BlockSpec((1, d), lambda b, i: (0, 0)),
                  _mod_spec(0, lambda b: ctx_row)(d),
                  _mod_spec(1, lambda b: ctx_row)(d),
                  w_cols(0), w_cols(1)],
        out_specs=[spec, spec],
        out_shape=[shape, shape],
        compiler_params=_params(("arbitrary", "arbitrary"), VMEM_LIMIT),
        name="inproj_ctx",
    )(ctx, g, mods4, mods4, w_bf, w_bf)


CONV_HALO = 16
CONV_RC = 128
CONV_CW = 128


def _conv_body(prev_ref, cur_ref, next_ref, w_ref, b_ref, lg_ref, lb_ref, o_ref, ext, ysc, *, width, tt):
    i = pl.program_id(1)
    n = pl.num_programs(1)
    ext[0:CONV_HALO, :] = jnp.where(i > 0, prev_ref[...], 0.0)
    ext[CONV_HALO:CONV_HALO + tt, :] = cur_ref[...]
    ext[CONV_HALO + tt:, :] = jnp.where(i < n - 1, next_ref[...], 0.0)
    lead = CONV_HALO - width // 2
    for rc in range(tt // CONV_RC):
        r0 = rc * CONV_RC
        for ch in range(cur_ref.shape[-1] // CONV_CW):
            cs = slice(ch * CONV_CW, (ch + 1) * CONV_CW)
            out = None
            for m in range(SUBLANES):
                nrow = CONV_RC + (SUBLANES if m else 0)
                part = None
                for j in range(width):
                    if (lead + j) % SUBLANES != m:
                        continue
                    a = r0 + lead + j - m
                    term = w_ref[j:j + 1, cs] * ext[a:a + nrow, cs]
                    part = term if part is None else part + term
                piece = part[m:m + CONV_RC]
                out = piece if out is None else out + piece
            ysc[r0:r0 + CONV_RC, cs] = out
    y = ysc[...] + b_ref[...]
    mu = jnp.mean(y, axis=-1, keepdims=True)
    yc = y - mu
    var = jnp.mean(yc * yc, axis=-1, keepdims=True)
    yn = yc * lax.rsqrt(var + EPS) * lg_ref[...] + lb_ref[...]
    o_ref[...] = (yn * jax.nn.sigmoid(yn)).astype(BF16)


def _conv(u, conv_w, conv_b, ln_g, ln_b, tt):
    bsz, t, cw = u.shape
    width = conv_w.shape[0]
    assert width // 2 < CONV_HALO and tt % CONV_HALO == 0 and tt % CONV_RC == 0 and cw % CONV_CW == 0
    hb = tt // CONV_HALO
    nh = t // CONV_HALO
    row = lambda a: a.reshape(1, cw)
    return pl.pallas_call(
        functools.partial(_conv_body, width=width, tt=tt),
        grid=(bsz, t // tt),
        in_specs=[pl.BlockSpec((None, CONV_HALO, cw), lambda b, i: (b, jnp.maximum(i * hb - 1, 0), 0)),
                  pl.BlockSpec((None, tt, cw), lambda b, i: (b, i, 0)),
                  pl.BlockSpec((None, CONV_HALO, cw), lambda b, i: (b, jnp.minimum((i + 1) * hb, nh - 1), 0)),
                  pl.BlockSpec((width, cw), lambda b, i: (0, 0)),
                  pl.BlockSpec((1, cw), lambda b, i: (0, 0)),
                  pl.BlockSpec((1, cw), lambda b, i: (0, 0)),
                  pl.BlockSpec((1, cw), lambda b, i: (0, 0))],
        out_specs=pl.BlockSpec((None, tt, cw), lambda b, i: (b, i, 0)),
        out_shape=jax.ShapeDtypeStruct((bsz, t, cw), BF16),
        scratch_shapes=[pltpu.VMEM((tt + 2 * CONV_HALO, cw), F32), pltpu.VMEM((tt, cw), F32)],
        compiler_params=_params(("arbitrary", "arbitrary")),
        name="conv",
    )(u, u, u, conv_w, row(conv_b), row(ln_g), row(ln_b))


ATT_RB = 8


def _bias_body(rpb_ref, onehot_ref, valid_ref, o_ref):
    t = jnp.dot(rpb_ref[...], onehot_ref[...], preferred_element_type=F32, precision=HIGHEST)
    o_ref[...] = jnp.where(valid_ref[...] > 0.5, t, NEG_INF)


def _bias_table(rpb):
    nh, ndr, ndc = rpb.shape
    qc = np.arange(GRID_W)[:, None]
    kc = np.arange(GRID_W)[None, :]
    c0 = np.clip(qc - NA_COLS // 2, 0, GRID_W - NA_COLS)
    valid = ((kc >= c0) & (kc < c0 + NA_COLS)).astype(np.float32).reshape(1, GRID_W * GRID_W)
    d_col = np.clip(kc - qc + (NA_COLS - 1), 0, 2 * NA_COLS - 2).reshape(1, GRID_W * GRID_W)
    ndc_pad = -(-ndc // SUBLANES) * SUBLANES
    onehot = (np.arange(ndc_pad)[:, None] == d_col).astype(np.float32)
    rpb2 = jnp.pad(rpb.reshape(nh * ndr, ndc), ((0, 0), (0, ndc_pad - ndc)))
    toeplitz = pl.pallas_call(
        _bias_body,
        out_shape=jax.ShapeDtypeStruct((nh * ndr, GRID_W * GRID_W), F32),
        name="bias",
    )(rpb2, onehot, valid).reshape(nh, ndr, GRID_W, GRID_W)
    return jnp.concatenate([toeplitz[:, :-1], toeplitz[:, 1:]], axis=-1)


def _attn_body(q_ref, kp_ref, kc_ref, kn_ref, vp_ref, vc_ref, vn_ref, kx_ref, vx_ref, bias_ref, o_ref,
               kwin, vwin, s_sc, p_sc, m_sc, l_sc, *, rows, kh, n_pairs):
    i = pl.program_id(1)
    blk = ATT_RB * GRID_W
    for w, (a, b) in enumerate(((kp_ref, vp_ref), (kc_ref, vc_ref), (kn_ref, vn_ref))):
        kwin[w * blk:(w + 1) * blk, :] = a[...]
        vwin[w * blk:(w + 1) * blk, :] = b[...]
    lo = lax.broadcasted_iota(I32, (GRID_W, LANES), 1) < LANES // 2
    band = kh * GRID_W

    def lane_groups(x, op):
        parts = [x[:, g * LANES:(g + 1) * LANES] for g in range(x.shape[-1] // LANES)]
        while len(parts) > 1:
            parts = [op(parts[a], parts[a + 1]) for a in range(0, len(parts) - 1, 2)] + parts[len(parts) & ~1:]
        return parts[0]

    def row_body(rr, carry):
        r = i * ATT_RB + rr
        r0 = jnp.clip(r - kh // 2, 0, rows - kh)
        delta = r - r0
        off = pl.multiple_of((r0 - (i - 1) * ATT_RB) * GRID_W, GRID_W)
        qoff = pl.multiple_of(rr * GRID_W, GRID_W)
        def scores(p):
            cols = slice(p * LANES, (p + 1) * LANES)
            qp = q_ref[pl.ds(qoff, GRID_W), cols]
            zeros = jnp.zeros_like(qp)
            q2 = jnp.concatenate([jnp.where(lo, qp, zeros), jnp.where(lo, zeros, qp)], axis=0)
            bias = jnp.concatenate(
                [jnp.concatenate([bias_ref[2 * p + s, 2 * g + (kh - 1) - delta] for g in range(kh // 2)], axis=-1)
                 for s in range(2)], axis=0)
            s_loc = lax.dot_general(q2, kwin[pl.ds(off, band), cols], _NT, preferred_element_type=F32) + bias
            s_ctx = lax.dot_general(q2, kx_ref[:, cols], _NT, preferred_element_type=F32)
            s_sc[p, :, 0:band] = s_loc
            s_sc[p, :, band:] = s_ctx
            m_sc[p] = jnp.maximum(lane_groups(s_loc, jnp.maximum), lane_groups(s_ctx, jnp.maximum))

        def numerators(p):
            m = jnp.max(m_sc[p], axis=-1, keepdims=True)
            e = jnp.exp(s_sc[p] - m)
            p_sc[p] = e.astype(BF16)
            l_sc[p] = lane_groups(e, jnp.add)

        def values(p):
            cols = slice(p * LANES, (p + 1) * LANES)
            o = (jnp.dot(p_sc[p, :, 0:band], vwin[pl.ds(off, band), cols], preferred_element_type=F32)
                 + jnp.dot(p_sc[p, :, band:], vx_ref[:, cols], preferred_element_type=F32))
            o = o / jnp.sum(l_sc[p], axis=-1, keepdims=True)
            o_ref[pl.ds(qoff, GRID_W), cols] = jnp.where(lo, o[0:GRID_W], o[GRID_W:]).astype(BF16)

        for stage in (scores, numerators, values):
            for p in range(n_pairs):
                stage(p)
        return carry

    lax.fori_loop(0, ATT_RB, row_body, 0, unroll=True)


def _attn(q, k, v, kx, vx, bias):
    bsz, t, aw = q.shape
    l = kx.shape[1]
    rows = t // GRID_W
    kh = (bias.shape[1] + 2) // 2
    nb = rows // ATT_RB
    blk = ATT_RB * GRID_W
    assert kh == ATT_RB and rows % ATT_RB == 0 and aw % LANES == 0 and 2 * GRID_W == LANES
    cur = pl.BlockSpec((None, blk, aw), lambda b, i: (b, i, 0))
    prev = pl.BlockSpec((None, blk, aw), lambda b, i: (b, jnp.maximum(i - 1, 0), 0))
    nxt = pl.BlockSpec((None, blk, aw), lambda b, i: (b, jnp.minimum(i + 1, nb - 1), 0))
    cx = pl.BlockSpec((None, l, aw), lambda b, i: (b, 0, 0))
    nh = bias.shape[0]
    nkeys = kh * GRID_W + l
    assert l % LANES == 0 and nh == 2 * (aw // LANES)
    return pl.pallas_call(
        functools.partial(_attn_body, rows=rows, kh=kh, n_pairs=aw // LANES),
        grid=(bsz, nb),
        in_specs=[cur, prev, cur, nxt, prev, cur, nxt, cx, cx,
                  pl.BlockSpec(bias.shape, lambda b, i: (0, 0, 0, 0))],
        out_specs=cur,
        out_shape=jax.ShapeDtypeStruct((bsz, t, aw), BF16),
        scratch_shapes=[pltpu.VMEM((3 * blk, aw), BF16), pltpu.VMEM((3 * blk, aw), BF16),
                        pltpu.VMEM((nh // 2, 2 * GRID_W, nkeys), F32), pltpu.VMEM((nh // 2, 2 * GRID_W, nkeys), BF16),
                        pltpu.VMEM((nh // 2, 2 * GRID_W, LANES), F32), pltpu.VMEM((nh // 2, 2 * GRID_W, LANES), F32)],
        compiler_params=_params(("arbitrary", "arbitrary"), VMEM_LIMIT),
        name="attn",
    )(q, k, k, k, v, v, v, kx, vx, bias)


def _outproj_body(yc_ref, ya_ref, x_ref, wo_ref, g1_ref, gf_ref, sh_ref, sc_ref, wr_ref,
                  x1_ref, h2_ref, aff_ref, *, cw, tm, ne):
    acc = (jnp.dot(yc_ref[...], wo_ref[0:cw, :], preferred_element_type=F32)
           + jnp.dot(ya_ref[...], wo_ref[cw:, :], preferred_element_type=F32))
    x1 = x_ref[...] + g1_ref[...] * acc
    x1_ref[...] = x1
    h2 = _norm_mod(x1, gf_ref[...], sh_ref[...], sc_ref[...])
    for s in range(h2.shape[-1] // LANES):
        _tile_rows(h2_ref, 0, tm, s)[...] = h2[:, s * LANES:(s + 1) * LANES]
    hi = h2.astype(BF16)
    lo = (h2 - hi.astype(F32)).astype(BF16)
    part_hi = jnp.dot(hi, wr_ref[...], preferred_element_type=F32)
    part_lo = jnp.dot(lo, wr_ref[...], preferred_element_type=F32)
    logits = part_hi + pltpu.roll(part_hi, LANES - ne, axis=1) + part_lo
    logits = jnp.where(lax.broadcasted_iota(I32, logits.shape, 1) < ne, logits, NEG_INF)
    e = jnp.exp(logits - jnp.max(logits, axis=-1, keepdims=True))
    aff_ref[...] = (e / jnp.sum(e, axis=-1, keepdims=True)).T[0:ne, :]


def _outproj(y_conv, y_attn, x, wo_bf, mods4, g_ffn, w_router, tm):
    bsz, t, d = x.shape
    cw = y_conv.shape[-1]
    ne = w_router.shape[1]
    assert d == SUBLANES * LANES and 2 * ne <= LANES
    nt = t // tm
    w_hi = w_router.astype(BF16)
    w_lo = (w_router - w_hi.astype(F32)).astype(BF16)
    wr_t = jnp.zeros((d, LANES), BF16).at[:, 0:ne].set(w_hi).at[:, ne:2 * ne].set(w_lo)
    return pl.pallas_call(
        functools.partial(_outproj_body, cw=cw, tm=tm, ne=ne),
        grid=(bsz, nt),
        in_specs=[pl.BlockSpec((None, tm, cw), lambda b, i: (b, i, 0)),
                  pl.BlockSpec((None, tm, y_attn.shape[-1]), lambda b, i: (b, i, 0)),
                  pl.BlockSpec((None, tm, d), lambda b, i: (b, i, 0)),
                  pl.BlockSpec(wo_bf.shape, lambda b, i: (0, 0)),
                  _mod_spec(2, lambda b: b)(d),
                  pl.BlockSpec((1, d), lambda b, i: (0, 0)),
                  _mod_spec(3, lambda b: b)(d),
                  _mod_spec(4, lambda b: b)(d),
                  pl.BlockSpec(wr_t.shape, lambda b, i: (0, 0))],
        out_specs=[pl.BlockSpec((None, tm, d), lambda b, i: (b, i, 0)),
                   pl.BlockSpec((tm * SUBLANES, LANES), lambda b, i: (b * nt + i, 0)),
                   pl.BlockSpec((None, ne, tm), lambda b, i: (b, 0, i))],
        out_shape=[jax.ShapeDtypeStruct((bsz, t, d), F32),
                   jax.ShapeDtypeStruct((bsz * t * SUBLANES, LANES), F32),
                   jax.ShapeDtypeStruct((bsz, ne, t), F32)],
        compiler_params=_params(("arbitrary", "arbitrary"), VMEM_LIMIT),
        name="outproj",
    )(y_conv, y_attn, x, wo_bf, mods4, g_ffn, mods4, mods4, wr_t)


def _route_body(affe_ref, affr_ref, u_ref, ones_ref, l_ref, tok_ref, *, cap, ne, nblk):
    aff_e = affe_ref[...]

    def search(it, bits):
        cand = bits | jnp.left_shift(jnp.int32(1), 30 - it)
        cnt = jnp.sum(jnp.where(aff_e >= pltpu.bitcast(cand, F32), 1.0, 0.0), axis=1, keepdims=True)
        return jnp.where(cnt >= cap, cand, bits)

    tau = pltpu.bitcast(lax.fori_loop(0, 31, search, jnp.zeros((ne, 1), I32)), F32)
    n_gt = jnp.sum(jnp.where(aff_e > tau, 1.0, 0.0), axis=1, keepdims=True)
    need = cap - n_gt

    def per_row(col):
        return jnp.concatenate([jnp.broadcast_to(col[e:e + 1, :], (nblk, 1)) for e in range(ne)], axis=0)

    tau_r = per_row(tau)
    need_r = per_row(need)
    kr = affr_ref[...]

    def scan(mf):
        mb = mf.astype(BF16)
        incl = jnp.dot(mb, u_ref[...], preferred_element_type=F32)
        tot = jnp.dot(mb, ones_ref[...], preferred_element_type=F32)
        off = jnp.dot(l_ref[...], tot.astype(BF16), preferred_element_type=F32)
        return incl, tot, off

    eq = jnp.where(kr == tau_r, 1.0, 0.0)
    incl_e, _, off_e = scan(eq)
    rank_eq = off_e + incl_e - eq
    sel = jnp.where((kr > tau_r) | ((eq > 0.5) & (rank_eq < need_r)), 1.0, 0.0)
    incl, _, off = scan(sel)

    nbits = LANES.bit_length() - 1
    lane = lax.broadcasted_iota(I32, kr.shape, 1)
    row = lax.broadcasted_iota(I32, kr.shape, 0)
    dist = lane - (incl.astype(I32) - 1)
    word = jnp.where(sel > 0.5, (1 << (2 * nbits)) | (dist << nbits) | lane, 0)

    def bit(x, pos):
        return (x >> pos) & 1

    for k in range(nbits):
        arriving = pltpu.roll(word, LANES - (1 << k), axis=1)
        take = bit(arriving, 2 * nbits) & bit(arriving, nbits + k)
        stay = bit(word, 2 * nbits) & (1 - bit(word, nbits + k))
        word = jnp.where(take == 1, arriving, jnp.where(stay == 1, word, 0))

    off_i = off.astype(I32)
    shift = off_i & (LANES - 1)
    for k in range(nbits):
        word = jnp.where(bit(shift, k) == 1, pltpu.roll(word, 1 << k, axis=1), word)
    valid = bit(word, 2 * nbits) == 1
    wrapped = lane < shift
    blk_bits = nblk.bit_length() - 1
    q0 = (row >> blk_bits) * (cap // LANES) + (off_i >> nbits)
    digits = (jnp.where(valid, row & (nblk - 1), 0), jnp.where(valid, word & (LANES - 1), 0))

    def place(sel_rows, q):
        onehot_t = jnp.where(q == lane, 1.0, 0.0).astype(BF16)
        outs = []
        for dgt in digits:
            x = jnp.where(sel_rows, dgt, 0).astype(F32).astype(BF16)
            outs.append(lax.dot_general(onehot_t, x, (((0,), (0,)), ((), ())), preferred_element_type=F32))
        return outs[0] * LANES + outs[1]

    tok_ref[...] = (place(jnp.logical_not(wrapped), q0) + place(wrapped, q0 + 1)).astype(I32)


def _route(aff, cap):
    bsz, ne, t = aff.shape
    nblk = t // LANES
    nr = ne * nblk
    nq = ne * cap // LANES
    assert nq == LANES and cap % LANES == 0 and nblk & (nblk - 1) == 0 and nblk <= 256
    aff_r = aff.reshape(bsz, nr, LANES)
    ar = np.arange(LANES)
    upper = jnp.asarray(ar[:, None] <= ar[None, :], BF16)
    ones = jnp.ones((LANES, LANES), BF16)
    rr = np.arange(nr)
    lower = jnp.asarray((rr[:, None] // nblk == rr[None, :] // nblk) & (rr[None, :] < rr[:, None]), BF16)
    const = lambda a: pl.BlockSpec(a.shape, lambda b: (0,) * a.ndim)
    return pl.pallas_call(
        functools.partial(_route_body, cap=cap, ne=ne, nblk=nblk),
        grid=(bsz,),
        in_specs=[pl.BlockSpec((None, ne, t), lambda b: (b, 0, 0)),
                  pl.BlockSpec((None, nr, LANES), lambda b: (b, 0, 0)),
                  const(upper), const(ones), const(lower)],
        out_specs=pl.BlockSpec((None, nq, LANES), lambda b: (b, 0, 0)),
        out_shape=jax.ShapeDtypeStruct((bsz, nq, LANES), I32),
        compiler_params=_params(("arbitrary",), VMEM_LIMIT),
        name="route",
    )(aff, aff_r, upper, ones, lower)


MOE_HC = 512
MOE_TM = 512
ROW_UNROLL = 8


def _gather_copy(h2_hbm, xs3, sem, token, slot):
    src = h2_hbm.at[pl.ds(pl.multiple_of(token * SUBLANES, SUBLANES), SUBLANES), :]
    dst = xs3.at[pl.ds(pl.multiple_of(slot * SUBLANES, SUBLANES), SUBLANES), :]
    return pltpu.make_async_copy(src, dst, sem)


def _moe_body(tok_ref, nxt_ref, h2_hbm, wg_ref, wu_ref, wd_ref, y_ref, xs3, xsb, wgb, wub, wdb, sem,
              *, bsz, t, cap, ne, nch):
    e = pl.program_id(0)
    c = pl.program_id(1)
    n_rows = bsz * cap
    n_tiles = n_rows // MOE_TM
    per_tile = n_rows // (nch * n_tiles)
    n_lane_groups = xsb.shape[-1] // LANES

    def fetch(ref, b, slot):
        _gather_copy(h2_hbm, xs3, sem, b * t + ref[b, 0, slot], b * cap + slot).start()

    def wait_staging():
        pltpu.make_async_copy(h2_hbm.at[pl.ds(0, n_rows * SUBLANES), :], xs3, sem).wait()

    @pl.when((e == 0) & (c == 0))
    def _first_expert():
        for b in range(bsz):
            def rows(g, carry):
                for k in range(ROW_UNROLL):
                    fetch(tok_ref, b, g * ROW_UNROLL + k)
                return carry

            lax.fori_loop(0, cap // ROW_UNROLL, rows, 0)

    def tiles(first_chunk):
        wgb[...] = wg_ref[...].astype(BF16)
        wub[...] = wu_ref[...].astype(BF16)
        wdb[...] = wd_ref[...].astype(BF16)
        for m in range(n_tiles):
            r0 = m * MOE_TM
            if first_chunk:
                xm = jnp.concatenate([_tile_rows(xs3, r0, MOE_TM, s)[...].astype(BF16)
                                      for s in range(n_lane_groups)], axis=-1)
                xsb[r0:r0 + MOE_TM, :] = xm
            else:
                xm = xsb[r0:r0 + MOE_TM, :]
            gate = jnp.dot(xm, wgb[...], preferred_element_type=F32)
            first = (c * n_tiles + m) * per_tile
            b = first // cap
            for k in range(per_tile):
                fetch(nxt_ref, b, first - b * cap + k)
            up = jnp.dot(xm, wub[...], preferred_element_type=F32)
            hid = (gate * jax.nn.sigmoid(gate) * up).astype(BF16)
            contrib = jnp.dot(hid, wdb[...], preferred_element_type=F32)
            if first_chunk:
                y_ref[r0:r0 + MOE_TM, :] = contrib
            else:
                y_ref[r0:r0 + MOE_TM, :] += contrib

    @pl.when(c == 0)
    def _():
        wait_staging()
        tiles(True)

    @pl.when(c > 0)
    def _():
        tiles(False)

    @pl.when((e == ne - 1) & (c == nch - 1))
    def _drain():
        wait_staging()


def _moe(tok, h2_tiles, w_gate, w_up, w_down, bsz, t, cap):
    ne, d, dh = w_gate.shape
    nch = dh // MOE_HC
    n_rows = bsz * cap
    per_tile = n_rows // (nch * (n_rows // MOE_TM))
    assert cap % ROW_UNROLL == 0 and n_rows % MOE_TM == 0 and cap % per_tile == 0 and n_rows // nch <= MOE_TM
    assert d == SUBLANES * LANES
    tok_spec = lambda ahead: pl.BlockSpec((bsz, None, 1, cap),
                                          lambda e, c: (0, jnp.minimum(e + ahead, ne - 1), 0, 0),
                                          memory_space=pltpu.SMEM)
    return pl.pallas_call(
        functools.partial(_moe_body, bsz=bsz, t=t, cap=cap, ne=ne, nch=nch),
        grid=(ne, nch),
        in_specs=[tok_spec(0), tok_spec(1),
                  pl.BlockSpec(memory_space=pl.ANY),
                  pl.BlockSpec((None, d, MOE_HC), lambda e, c: (e, 0, c)),
                  pl.BlockSpec((None, d, MOE_HC), lambda e, c: (e, 0, c)),
                  pl.BlockSpec((None, MOE_HC, d), lambda e, c: (e, c, 0))],
        out_specs=pl.BlockSpec((None, n_rows, d), lambda e, c: (e, 0, 0)),
        out_shape=jax.ShapeDtypeStruct((ne, n_rows, d), F32),
        scratch_shapes=[pltpu.VMEM((n_rows * SUBLANES, LANES), F32),
                        pltpu.VMEM((n_rows, d), BF16),
                        pltpu.VMEM((d, MOE_HC), BF16),
                        pltpu.VMEM((d, MOE_HC), BF16),
                        pltpu.VMEM((MOE_HC, d), BF16),
                        pltpu.SemaphoreType.DMA],
        compiler_params=_params(("arbitrary", "arbitrary"), VMEM_LIMIT),
        name="moe",
    )(tok, tok, h2_tiles, w_gate, w_up, w_down)


COMBINE_CHUNK = 128


def _combine_body(tok_ref, aff_ref, y_ref, x1_ref, g2_ref, gf_ref, o_ref, acc, ysc, *, cap, ne, tm):
    s = pl.program_id(1)

    @pl.when(s == 0)
    def _():
        acc[...] = jnp.zeros_like(acc)

    @pl.when(s < ne)
    def _scatter():
        def tile_of(row):
            start = row * SUBLANES
            return pl.ds(start if isinstance(start, int) else pl.multiple_of(start, SUBLANES), SUBLANES)

        for r0 in range(0, cap, COMBINE_CHUNK):
            for g in range(SUBLANES):
                _tile_rows(ysc, r0, COMBINE_CHUNK, g)[...] = y_ref[r0:r0 + COMBINE_CHUNK, g * LANES:(g + 1) * LANES]
            for base in range(r0, r0 + COMBINE_CHUNK, ROW_UNROLL):
                tokens = [tok_ref[0, base + k] for k in range(ROW_UNROLL)]
                new = [acc[tile_of(tk), :] + aff_ref[0, tk] * ysc[tile_of(base + k), :]
                       for k, tk in enumerate(tokens)]
                for tk, val in zip(tokens, new):
                    acc[tile_of(tk), :] = val

    @pl.when(s >= ne)
    def _finish():
        first = pl.multiple_of((s - ne) * tm, tm)
        moe = jnp.concatenate([_tile_rows(acc, first, tm, g)[...] for g in range(SUBLANES)], axis=-1)
        x = x1_ref[...] + g2_ref[...] * moe
        ms = jnp.mean(x * x, axis=-1, keepdims=True)
        o_ref[...] = x * lax.rsqrt(ms + EPS) * gf_ref[...]


def _combine(tok, aff, y, x1, mods4, g_final, cap, tm):
    bsz, t, d = x1.shape
    ne = aff.shape[1]
    nt = t // tm
    assert d == SUBLANES * LANES
    expert = lambda s: jnp.minimum(s, ne - 1)
    tile = lambda s: jnp.maximum(s - ne, 0)
    smem = lambda n: pl.BlockSpec((None, None, 1, n), lambda b, s: (b, expert(s), 0, 0), memory_space=pltpu.SMEM)
    token_tile = pl.BlockSpec((None, tm, d), lambda b, s: (b, tile(s), 0))
    return pl.pallas_call(
        functools.partial(_combine_body, cap=cap, ne=ne, tm=tm),
        grid=(bsz, ne + nt),
        in_specs=[smem(cap), smem(t),
                  pl.BlockSpec((None, cap, d), lambda b, s: (expert(s), b, 0)),
                  token_tile,
                  _mod_spec(5, lambda b: b)(d),
                  pl.BlockSpec((1, d), lambda b, s: (0, 0))],
        out_specs=token_tile,
        out_shape=jax.ShapeDtypeStruct((bsz, t, d), F32),
        scratch_shapes=[pltpu.VMEM((t * SUBLANES, LANES), F32),
                        pltpu.VMEM((cap * SUBLANES, LANES), F32)],
        compiler_params=_params(("arbitrary", "arbitrary"), VMEM_LIMIT),
        name="combine",
    )(tok, aff, y, x1, mods4, g_final)


def _layer(x, ctx, mods4, ctx_row, norm_mix_g, w_in, conv_w, conv_b, conv_ln_g, conv_ln_b, rpb, w_out,
           norm_ffn_g, w_router, w_gate, w_up, w_down, norm_final_g):
    bsz, t, d = x.shape
    cw = conv_w.shape[1]
    n_heads = rpb.shape[0]
    aw = (w_in.shape[1] - 2 * cw) // 3
    head_dim = aw // n_heads
    rows = t // GRID_W
    ne = w_router.shape[1]
    cap = EC_FACTOR * t // ne
    row = lambda a: a.reshape(1, -1)

    w_in_bf = w_in.astype(BF16)
    u, q, k, v = _inproj(x, row(norm_mix_g), mods4, w_in_bf, cw, aw, head_dim ** -0.5, tm=1024)
    kx, vx = _inproj_ctx(ctx, row(norm_mix_g), mods4, w_in_bf, 2 * cw + aw, aw, ctx_row)
    y_conv = _conv(u, conv_w, conv_b, conv_ln_g, conv_ln_b, tt=512)
    assert rows >= (rpb.shape[1] + 1) // 2
    y_attn = _attn(q, k, v, kx, vx, _bias_table(rpb))
    x1, h2_tiles, aff = _outproj(y_conv, y_attn, x, w_out.astype(BF16), mods4, row(norm_ffn_g),
                                 w_router, tm=1024)
    tok = _route(aff, cap).reshape(bsz, ne, 1, cap)
    y = _moe(tok, h2_tiles, w_gate, w_up, w_down, bsz, t, cap)
    return _combine(tok, aff.reshape(bsz, ne, 1, t), y, x1, mods4, row(norm_final_g), cap, tm=512)


def kernel(x, c, ctx, c_ctx, w_mod, b_mod, norm_mix_g, w_in, conv_w, conv_b, conv_ln_g, conv_ln_b, rpb,
           w_out, norm_ffn_g, w_router, w_gate, w_up, w_down, norm_final_g):
    bsz, t, d = x.shape
    depth = w_mod.shape[0]
    assert depth == 1
    assert bsz + 1 <= SUBLANES
    c8 = jnp.zeros((SUBLANES, d), F32).at[:bsz].set(c).at[bsz].set(c_ctx)
    mods4 = _adaln(c8, w_mod[0], b_mod[0].reshape(1, -1)).reshape(SUBLANES, N_MOD, 1, d)
    return _layer(x, ctx, mods4, bsz, norm_mix_g[0], w_in[0], conv_w[0], conv_b[0], conv_ln_g[0],
                  conv_ln_b[0], rpb[0], w_out[0], norm_ffn_g[0], w_router[0], w_gate[0], w_up[0],
                  w_down[0], norm_final_g)
```

```python
import functools

import jax
import jax.numpy as jnp
import numpy as np
from jax import lax
from jax.experimental import pallas as pl
from jax.experimental.pallas import tpu as pltpu

F32 = jnp.float32
BF16 = jnp.bfloat16
I32 = jnp.int32
HIGHEST = lax.Precision.HIGHEST

GRID_W = 64
NA_COLS = 16
EC_FACTOR = 2
N_MOD = 6
EPS = 1e-6
NEG_INF = -1e30
LANES = 128
SUBLANES = 8
VMEM_LIMIT = 56 * 1024 * 1024

_NT = (((1,), (1,)), ((), ()))


def _params(sem, vmem=None):
    return pltpu.CompilerParams(dimension_semantics=sem, vmem_limit_bytes=vmem)


def _tile_rows(ref, first, n, s):
    return ref.at[pl.ds(first * SUBLANES + s, n, stride=SUBLANES), :]


def _adaln_body(c_ref, w_ref, b_ref, o_ref):
    c = c_ref[...]
    s = c * jax.nn.sigmoid(c)
    o_ref[...] = jnp.dot(s, w_ref[...], preferred_element_type=F32, precision=HIGHEST) + b_ref[...]


def _adaln(c8, w_mod, b_mod):
    d, n = w_mod.shape
    bn = 512
    return pl.pallas_call(
        _adaln_body,
        grid=(n // bn,),
        in_specs=[pl.BlockSpec((SUBLANES, d), lambda j: (0, 0)),
                  pl.BlockSpec((d, bn), lambda j: (0, j)),
                  pl.BlockSpec((1, bn), lambda j: (0, j))],
        out_specs=pl.BlockSpec((SUBLANES, bn), lambda j: (0, j)),
        out_shape=jax.ShapeDtypeStruct((SUBLANES, n), F32),
        compiler_params=_params(("arbitrary",)),
        name="adaln",
    )(c8, w_mod, b_mod)


def _norm_mod(x, g, shift, scale):
    ms = jnp.mean(x * x, axis=-1, keepdims=True)
    h = x * lax.rsqrt(ms + EPS) * g
    return h * (1.0 + scale) + shift


def _inproj_body(x_ref, g_ref, sh_ref, sc_ref, w_ref, u_ref, q_ref, k_ref, v_ref, *, cw, aw, qscale):
    hb = _norm_mod(x_ref[...], g_ref[...], sh_ref[...], sc_ref[...]).astype(BF16)

    def proj(lo, width):
        return jnp.dot(hb, w_ref[:, lo:lo + width], preferred_element_type=F32)

    a = proj(0, cw)
    gate = proj(cw, cw)
    u_ref[...] = a * jax.nn.sigmoid(gate)
    q_ref[...] = (proj(2 * cw, aw) * qscale).astype(BF16)
    k_ref[...] = proj(2 * cw + aw, aw).astype(BF16)
    v_ref[...] = proj(2 * cw + 2 * aw, aw).astype(BF16)


def _inproj_kv_body(x_ref, g_ref, sh_ref, sc_ref, wk_ref, wv_ref, k_ref, v_ref):
    hb = _norm_mod(x_ref[...], g_ref[...], sh_ref[...], sc_ref[...]).astype(BF16)
    k_ref[...] = jnp.dot(hb, wk_ref[...], preferred_element_type=F32).astype(BF16)
    v_ref[...] = jnp.dot(hb, wv_ref[...], preferred_element_type=F32).astype(BF16)


def _mod_spec(slot, row_of):
    return lambda d: pl.BlockSpec((None, None, 1, d), lambda b, i: (row_of(b), slot, 0, 0))


def _inproj(x, g, mods4, w_bf, cw, aw, qscale, tm):
    bsz, t, d = x.shape
    tok = lambda w, dt: (pl.BlockSpec((None, tm, w), lambda b, i: (b, i, 0)), jax.ShapeDtypeStruct((bsz, t, w), dt))
    outs = [tok(cw, F32), tok(aw, BF16), tok(aw, BF16), tok(aw, BF16)]
    return pl.pallas_call(
        functools.partial(_inproj_body, cw=cw, aw=aw, qscale=qscale),
        grid=(bsz, t // tm),
        in_specs=[pl.BlockSpec((None, tm, d), lambda b, i: (b, i, 0)),
                  pl.BlockSpec((1, d), lambda b, i: (0, 0)),
                  _mod_spec(0, lambda b: b)(d),
                  _mod_spec(1, lambda b: b)(d),
                  pl.BlockSpec(w_bf.shape, lambda b, i: (0, 0))],
        out_specs=[o[0] for o in outs],
        out_shape=[o[1] for o in outs],
        compiler_params=_params(("arbitrary", "arbitrary"), VMEM_LIMIT),
        name="inproj",
    )(x, g, mods4, mods4, w_bf)


def _inproj_ctx(ctx, g, mods4, w_bf, k_lo, aw, ctx_row):
    bsz, l, d = ctx.shape
    assert k_lo % aw == 0
    spec = pl.BlockSpec((None, l, aw), lambda b, i: (b, 0, 0))
    shape = jax.ShapeDtypeStruct((bsz, l, aw), BF16)
    w_cols = lambda j: pl.BlockSpec((d, aw), lambda b, i: (0, k_lo // aw + j))
    return pl.pallas_call(
        _inproj_kv_body,
        grid=(bsz, 1),
        in_specs=[pl.BlockSpec((None, l, d), lambda b, i: (b, 0, 0)),
                  pl.BlockSpec((1, d), lambda b, i: (0, 0)),
                  _mod_spec(0, lambda b: ctx_row)(d),
                  _mod_spec(1, lambda b: ctx_row)(d),
                  w_cols(0), w_cols(1)],
        out_specs=[spec, spec],
        out_shape=[shape, shape],
        compiler_params=_params(("arbitrary", "arbitrary"), VMEM_LIMIT),
        name="inproj_ctx",
    )(ctx, g, mods4, mods4, w_bf, w_bf)


CONV_HALO = 16
CONV_RC = 128
CONV_CW = 128


def _conv_body(prev_ref, cur_ref, next_ref, w_ref, b_ref, lg_ref, lb_ref, o_ref, ext, ysc, *, width, tt):
    i = pl.program_id(1)
    n = pl.num_programs(1)
    ext[0:CONV_HALO, :] = jnp.where(i > 0, prev_ref[...], 0.0)
    ext[CONV_HALO:CONV_HALO + tt, :] = cur_ref[...]
    ext[CONV_HALO + tt:, :] = jnp.where(i < n - 1, next_ref[...], 0.0)
    lead = CONV_HALO - width // 2
    for rc in range(tt // CONV_RC):
        r0 = rc * CONV_RC
        for ch in range(cur_ref.shape[-1] // CONV_CW):
            cs = slice(ch * CONV_CW, (ch + 1) * CONV_CW)
            out = None
            for m in range(SUBLANES):
                nrow = CONV_RC + (SUBLANES if m else 0)
                part = None
                for j in range(width):
                    if (lead + j) % SUBLANES != m:
                        continue
                    a = r0 + lead + j - m
                    term = w_ref[j:j + 1, cs] * ext[a:a + nrow, cs]
                    part = term if part is None else part + term
                piece = part[m:m + CONV_RC]
                out = piece if out is None else out + piece
            ysc[r0:r0 + CONV_RC, cs] = out
    y = ysc[...] + b_ref[...]
    mu = jnp.mean(y, axis=-1, keepdims=True)
    yc = y - mu
    var = jnp.mean(yc * yc, axis=-1, keepdims=True)
    yn = yc * lax.rsqrt(var + EPS) * lg_ref[...] + lb_ref[...]
    o_ref[...] = (yn * jax.nn.sigmoid(yn)).astype(BF16)


def _conv(u, conv_w, conv_b, ln_g, ln_b, tt):
    bsz, t, cw = u.shape
    width = conv_w.shape[0]
    assert width // 2 < CONV_HALO and tt % CONV_HALO == 0 and tt % CONV_RC == 0 and cw % CONV_CW == 0
    hb = tt // CONV_HALO
    nh = t // CONV_HALO
    row = lambda a: a.reshape(1, cw)
    return pl.pallas_call(
        functools.partial(_conv_body, width=width, tt=tt),
        grid=(bsz, t // tt),
        in_specs=[pl.BlockSpec((None, CONV_HALO, cw), lambda b, i: (b, jnp.maximum(i * hb - 1, 0), 0)),
                  pl.BlockSpec((None, tt, cw), lambda b, i: (b, i, 0)),
                  pl.BlockSpec((None, CONV_HALO, cw), lambda b, i: (b, jnp.minimum((i + 1) * hb, nh - 1), 0)),
                  pl.BlockSpec((width, cw), lambda b, i: (0, 0)),
                  pl.BlockSpec((1, cw), lambda b, i: (0, 0)),
                  pl.BlockSpec((1, cw), lambda b, i: (0, 0)),
                  pl.BlockSpec((1, cw), lambda b, i: (0, 0))],
        out_specs=pl.BlockSpec((None, tt, cw), lambda b, i: (b, i, 0)),
        out_shape=jax.ShapeDtypeStruct((bsz, t, cw), BF16),
        scratch_shapes=[pltpu.VMEM((tt + 2 * CONV_HALO, cw), F32), pltpu.VMEM((tt, cw), F32)],
        compiler_params=_params(("arbitrary", "arbitrary")),
        name="conv",
    )(u, u, u, conv_w, row(conv_b), row(ln_g), row(ln_b))


ATT_RB = 8


def _bias_body(rpb_ref, onehot_ref, valid_ref, o_ref):
    t = jnp.dot(rpb_ref[...], onehot_ref[...], preferred_element_type=F32, precision=HIGHEST)
    o_ref[...] = jnp.where(valid_ref[...] > 0.5, t, NEG_INF)


def _bias_table(rpb):
    nh, ndr, ndc = rpb.shape
    qc = np.arange(GRID_W)[:, None]
    kc = np.arange(GRID_W)[None, :]
    c0 = np.clip(qc - NA_COLS // 2, 0, GRID_W - NA_COLS)
    valid = ((kc >= c0) & (kc < c0 + NA_COLS)).astype(np.float32).reshape(1, GRID_W * GRID_W)
    d_col = np.clip(kc - qc + (NA_COLS - 1), 0, 2 * NA_COLS - 2).reshape(1, GRID_W * GRID_W)
    ndc_pad = -(-ndc // SUBLANES) * SUBLANES
    onehot = (np.arange(ndc_pad)[:, None] == d_col).astype(np.float32)
    rpb2 = jnp.pad(rpb.reshape(nh * ndr, ndc), ((0, 0), (0, ndc_pad - ndc)))
    toeplitz = pl.pallas_call(
        _bias_body,
        out_shape=jax.ShapeDtypeStruct((nh * ndr, GRID_W * GRID_W), F32),
        name="bias",
    )(rpb2, onehot, valid).reshape(nh, ndr, GRID_W, GRID_W)
    return jnp.concatenate([toeplitz[:, :-1], toeplitz[:, 1:]], axis=-1)


def _attn_body(q_ref, kp_ref, kc_ref, kn_ref, vp_ref, vc_ref, vn_ref, kx_ref, vx_ref, bias_ref, o_ref,
               kwin, vwin, s_sc, p_sc, m_sc, l_sc, *, rows, kh, n_pairs):
    i = pl.program_id(1)
    blk = ATT_RB * GRID_W
    for w, (a, b) in enumerate(((kp_ref, vp_ref), (kc_ref, vc_ref), (kn_ref, vn_ref))):
        kwin[w * blk:(w + 1) * blk, :] = a[...]
        vwin[w * blk:(w + 1) * blk, :] = b[...]
    lo = lax.broadcasted_iota(I32, (GRID_W, LANES), 1) < LANES // 2
    band = kh * GRID_W

    def lane_groups(x, op):
        parts = [x[:, g * LANES:(g + 1) * LANES] for g in range(x.shape[-1] // LANES)]
        while len(parts) > 1:
            parts = [op(parts[a], parts[a + 1]) for a in range(0, len(parts) - 1, 2)] + parts[len(parts) & ~1:]
        return parts[0]

    def row_body(rr, carry):
        r = i * ATT_RB + rr
        r0 = jnp.clip(r - kh // 2, 0, rows - kh)
        delta = r - r0
        off = pl.multiple_of((r0 - (i - 1) * ATT_RB) * GRID_W, GRID_W)
        qoff = pl.multiple_of(rr * GRID_W, GRID_W)
        def scores(p):
            cols = slice(p * LANES, (p + 1) * LANES)
            qp = q_ref[pl.ds(qoff, GRID_W), cols]
            zeros = jnp.zeros_like(qp)
            q2 = jnp.concatenate([jnp.where(lo, qp, zeros), jnp.where(lo, zeros, qp)], axis=0)
            bias = jnp.concatenate(
                [jnp.concatenate([bias_ref[2 * p + s, 2 * g + (kh - 1) - delta] for g in range(kh // 2)], axis=-1)
                 for s in range(2)], axis=0)
            s_loc = lax.dot_general(q2, kwin[pl.ds(off, band), cols], _NT, preferred_element_type=F32) + bias
            s_ctx = lax.dot_general(q2, kx_ref[:, cols], _NT, preferred_element_type=F32)
            s_sc[p, :, 0:band] = s_loc
            s_sc[p, :, band:] = s_ctx
            m_sc[p] = jnp.maximum(lane_groups(s_loc, jnp.maximum), lane_groups(s_ctx, jnp.maximum))

        def numerators(p):
            m = jnp.max(m_sc[p], axis=-1, keepdims=True)
            e = jnp.exp(s_sc[p] - m)
            p_sc[p] = e.astype(BF16)
            l_sc[p] = lane_groups(e, jnp.add)

        def values(p):
            cols = slice(p * LANES, (p + 1) * LANES)
            o = (jnp.dot(p_sc[p, :, 0:band], vwin[pl.ds(off, band), cols], preferred_element_type=F32)
                 + jnp.dot(p_sc[p, :, band:], vx_ref[:, cols], preferred_element_type=F32))
            o = o / jnp.sum(l_sc[p], axis=-1, keepdims=True)
            o_ref[pl.ds(qoff, GRID_W), cols] = jnp.where(lo, o[0:GRID_W], o[GRID_W:]).astype(BF16)

        for stage in (scores, numerators, values):
            for p in range(n_pairs):
                stage(p)
        return carry

    lax.fori_loop(0, ATT_RB, row_body, 0, unroll=True)


def _attn(q, k, v, kx, vx, bias):
    bsz, t, aw = q.shape
    l = kx.shape[1]
    rows = t // GRID_W
    kh = (bias.shape[1] + 2) // 2
    nb = rows // ATT_RB
    blk = ATT_RB * GRID_W
    assert kh == ATT_RB and rows % ATT_RB == 0 and aw % LANES == 0 and 2 * GRID_W == LANES
    cur = pl.BlockSpec((None, blk, aw), lambda b, i: (b, i, 0))
    prev = pl.BlockSpec((None, blk, aw), lambda b, i: (b, jnp.maximum(i - 1, 0), 0))
    nxt = pl.BlockSpec((None, blk, aw), lambda b, i: (b, jnp.minimum(i + 1, nb - 1), 0))
    cx = pl.BlockSpec((None, l, aw), lambda b, i: (b, 0, 0))
    nh = bias.shape[0]
    nkeys = kh * GRID_W + l
    assert l % LANES == 0 and nh == 2 * (aw // LANES)
    return pl.pallas_call(
        functools.partial(_attn_body, rows=rows, kh=kh, n_pairs=aw // LANES),
        grid=(bsz, nb),
        in_specs=[cur, prev, cur, nxt, prev, cur, nxt, cx, cx,
                  pl.BlockSpec(bias.shape, lambda b, i: (0, 0, 0, 0))],
        out_specs=cur,
        out_shape=jax.ShapeDtypeStruct((bsz, t, aw), BF16),
        scratch_shapes=[pltpu.VMEM((3 * blk, aw), BF16), pltpu.VMEM((3 * blk, aw), BF16),
                        pltpu.VMEM((nh // 2, 2 * GRID_W, nkeys), F32), pltpu.VMEM((nh // 2, 2 * GRID_W, nkeys), BF16),
                        pltpu.VMEM((nh // 2, 2 * GRID_W, LANES), F32), pltpu.VMEM((nh // 2, 2 * GRID_W, LANES), F32)],
        compiler_params=_params(("arbitrary", "arbitrary"), VMEM_LIMIT),
        name="attn",
    )(q, k, k, k, v, v, v, kx, vx, bias)


OUT_GROUPS = 4


def _outproj_body(yc_ref, ya_ref, x_ref, wo_ref, g1_ref, gf_ref, sh_ref, sc_ref, wr_ref,
                  x1_ref, h2_ref, aff_ref, *, cw, tm, ne):
    rg = tm // OUT_GROUPS

    def project(g):
        rows = slice(g * rg, (g + 1) * rg)
        acc = (jnp.dot(yc_ref[rows, :], wo_ref[0:cw, :], preferred_element_type=F32)
               + jnp.dot(ya_ref[rows, :], wo_ref[cw:, :], preferred_element_type=F32))
        x1 = x_ref[rows, :] + g1_ref[...] * acc
        x1_ref[rows, :] = x1
        return x1

    def route(g, x1):
        h2 = _norm_mod(x1, gf_ref[...], sh_ref[...], sc_ref[...])
        for s in range(h2.shape[-1] // LANES):
            _tile_rows(h2_ref, g * rg, rg, s)[...] = h2[:, s * LANES:(s + 1) * LANES]
        hi = h2.astype(BF16)
        lo = (h2 - hi.astype(F32)).astype(BF16)
        part_hi = jnp.dot(hi, wr_ref[...], preferred_element_type=F32)
        part_lo = jnp.dot(lo, wr_ref[...], preferred_element_type=F32)
        logits = part_hi + pltpu.roll(part_hi, LANES - ne, axis=1) + part_lo
        logits = jnp.where(lax.broadcasted_iota(I32, logits.shape, 1) < ne, logits, NEG_INF)
        e = jnp.exp(logits - jnp.max(logits, axis=-1, keepdims=True))
        aff = (e / jnp.sum(e, axis=-1, keepdims=True)).T[0:ne, :]
        aff_ref[:, g * rg:(g + 1) * rg] = aff

    x1s = [project(g) for g in range(OUT_GROUPS)]
    for g in range(OUT_GROUPS):
        route(g, x1s[g])


def _outproj(y_conv, y_attn, x, wo_bf, mods4, g_ffn, w_router, tm):
    bsz, t, d = x.shape
    cw = y_conv.shape[-1]
    ne = w_router.shape[1]
    assert d == SUBLANES * LANES and 2 * ne <= LANES
    nt = t // tm
    w_hi = w_router.astype(BF16)
    w_lo = (w_router - w_hi.astype(F32)).astype(BF16)
    wr_t = jnp.zeros((d, LANES), BF16).at[:, 0:ne].set(w_hi).at[:, ne:2 * ne].set(w_lo)
    return pl.pallas_call(
        functools.partial(_outproj_body, cw=cw, tm=tm, ne=ne),
        grid=(bsz, nt),
        in_specs=[pl.BlockSpec((None, tm, cw), lambda b, i: (b, i, 0)),
                  pl.BlockSpec((None, tm, y_attn.shape[-1]), lambda b, i: (b, i, 0)),
                  pl.BlockSpec((None, tm, d), lambda b, i: (b, i, 0)),
                  pl.BlockSpec(wo_bf.shape, lambda b, i: (0, 0)),
                  _mod_spec(2, lambda b: b)(d),
                  pl.BlockSpec((1, d), lambda b, i: (0, 0)),
                  _mod_spec(3, lambda b: b)(d),
                  _mod_spec(4, lambda b: b)(d),
                  pl.BlockSpec(wr_t.shape, lambda b, i: (0, 0))],
        out_specs=[pl.BlockSpec((None, tm, d), lambda b, i: (b, i, 0)),
                   pl.BlockSpec((tm * SUBLANES, LANES), lambda b, i: (b * nt + i, 0)),
                   pl.BlockSpec((None, ne, tm), lambda b, i: (b, 0, i))],
        out_shape=[jax.ShapeDtypeStruct((bsz, t, d), F32),
                   jax.ShapeDtypeStruct((bsz * t * SUBLANES, LANES), F32),
                   jax.ShapeDtypeStruct((bsz, ne, t), F32)],
        compiler_params=_params(("arbitrary", "arbitrary"), VMEM_LIMIT),
        name="outproj",
    )(y_conv, y_attn, x, wo_bf, mods4, g_ffn, mods4, mods4, wr_t)


def _route_body(affe_ref, affr_ref, u_ref, ones_ref, l_ref, tok_ref, *, cap, ne, nblk):
    aff_e = affe_ref[...]

    def search(it, bits):
        cand = bits | jnp.left_shift(jnp.int32(1), 30 - it)
        cnt = jnp.sum(jnp.where(aff_e >= pltpu.bitcast(cand, F32), 1.0, 0.0), axis=1, keepdims=True)
        return jnp.where(cnt >= cap, cand, bits)

    tau = pltpu.bitcast(lax.fori_loop(0, 31, search, jnp.zeros((ne, 1), I32)), F32)
    n_gt = jnp.sum(jnp.where(aff_e > tau, 1.0, 0.0), axis=1, keepdims=True)
    need = cap - n_gt

    def per_row(col):
        return jnp.concatenate([jnp.broadcast_to(col[e:e + 1, :], (nblk, 1)) for e in range(ne)], axis=0)

    tau_r = per_row(tau)
    need_r = per_row(need)
    kr = affr_ref[...]

    def scan(mf):
        mb = mf.astype(BF16)
        incl = jnp.dot(mb, u_ref[...], preferred_element_type=F32)
        tot = jnp.dot(mb, ones_ref[...], preferred_element_type=F32)
        off = jnp.dot(l_ref[...], tot.astype(BF16), preferred_element_type=F32)
        return incl, tot, off

    eq = jnp.where(kr == tau_r, 1.0, 0.0)
    incl_e, _, off_e = scan(eq)
    rank_eq = off_e + incl_e - eq
    sel = jnp.where((kr > tau_r) | ((eq > 0.5) & (rank_eq < need_r)), 1.0, 0.0)
    incl, _, off = scan(sel)

    nbits = LANES.bit_length() - 1
    lane = lax.broadcasted_iota(I32, kr.shape, 1)
    row = lax.broadcasted_iota(I32, kr.shape, 0)
    dist = lane - (incl.astype(I32) - 1)
    word = jnp.where(sel > 0.5, (1 << (2 * nbits)) | (dist << nbits) | lane, 0)

    def bit(x, pos):
        return (x >> pos) & 1

    for k in range(nbits):
        arriving = pltpu.roll(word, LANES - (1 << k), axis=1)
        take = bit(arriving, 2 * nbits) & bit(arriving, nbits + k)
        stay = bit(word, 2 * nbits) & (1 - bit(word, nbits + k))
        word = jnp.where(take == 1, arriving, jnp.where(stay == 1, word, 0))

    off_i = off.astype(I32)
    shift = off_i & (LANES - 1)
    for k in range(nbits):
        word = jnp.where(bit(shift, k) == 1, pltpu.roll(word, 1 << k, axis=1), word)
    valid = bit(word, 2 * nbits) == 1
    wrapped = lane < shift
    blk_bits = nblk.bit_length() - 1
    q0 = (row >> blk_bits) * (cap // LANES) + (off_i >> nbits)
    digits = (jnp.where(valid, row & (nblk - 1), 0), jnp.where(valid, word & (LANES - 1), 0))

    def place(sel_rows, q):
        onehot_t = jnp.where(q == lane, 1.0, 0.0).astype(BF16)
        outs = []
        for dgt in digits:
            x = jnp.where(sel_rows, dgt, 0).astype(F32).astype(BF16)
            outs.append(lax.dot_general(onehot_t, x, (((0,), (0,)), ((), ())), preferred_element_type=F32))
        return outs[0] * LANES + outs[1]

    tok_ref[...] = (place(jnp.logical_not(wrapped), q0) + place(wrapped, q0 + 1)).astype(I32)


def _route(aff, cap):
    bsz, ne, t = aff.shape
    nblk = t // LANES
    nr = ne * nblk
    nq = ne * cap // LANES
    assert nq == LANES and cap % LANES == 0 and nblk & (nblk - 1) == 0 and nblk <= 256
    aff_r = aff.reshape(bsz, nr, LANES)
    ar = np.arange(LANES)
    upper = jnp.asarray(ar[:, None] <= ar[None, :], BF16)
    ones = jnp.ones((LANES, LANES), BF16)
    rr = np.arange(nr)
    lower = jnp.asarray((rr[:, None] // nblk == rr[None, :] // nblk) & (rr[None, :] < rr[:, None]), BF16)
    const = lambda a: pl.BlockSpec(a.shape, lambda b: (0,) * a.ndim)
    return pl.pallas_call(
        functools.partial(_route_body, cap=cap, ne=ne, nblk=nblk),
        grid=(bsz,),
        in_specs=[pl.BlockSpec((None, ne, t), lambda b: (b, 0, 0)),
                  pl.BlockSpec((None, nr, LANES), lambda b: (b, 0, 0)),
                  const(upper), const(ones), const(lower)],
        out_specs=pl.BlockSpec((None, nq, LANES), lambda b: (b, 0, 0)),
        out_shape=jax.ShapeDtypeStruct((bsz, nq, LANES), I32),
        compiler_params=_params(("arbitrary",), VMEM_LIMIT),
        name="route",
    )(aff, aff_r, upper, ones, lower)


MOE_HC = 512
MOE_TM = 512
ROW_UNROLL = 8


def _gather_copy(h2_hbm, xs3, sem, token, slot):
    src = h2_hbm.at[pl.ds(pl.multiple_of(token * SUBLANES, SUBLANES), SUBLANES), :]
    dst = xs3.at[pl.ds(pl.multiple_of(slot * SUBLANES, SUBLANES), SUBLANES), :]
    return pltpu.make_async_copy(src, dst, sem)


def _moe_body(tok_ref, nxt_ref, h2_hbm, wg_ref, wu_ref, wd_ref, y_ref, xs3, xsb, wgb, wub, wdb, sem,
              *, bsz, t, cap, ne, nch):
    e = pl.program_id(0)
    c = pl.program_id(1)
    n_rows = bsz * cap
    n_tiles = n_rows // MOE_TM
    per_tile = n_rows // (nch * n_tiles)
    n_lane_groups = xsb.shape[-1] // LANES

    def fetch(ref, b, slot):
        _gather_copy(h2_hbm, xs3, sem, b * t + ref[b, 0, slot], b * cap + slot).start()

    def wait_staging():
        pltpu.make_async_copy(h2_hbm.at[pl.ds(0, n_rows * SUBLANES), :], xs3, sem).wait()

    @pl.when((e == 0) & (c == 0))
    def _first_expert():
        for b in range(bsz):
            def rows(g, carry):
                for k in range(ROW_UNROLL):
                    fetch(tok_ref, b, g * ROW_UNROLL + k)
                return carry

            lax.fori_loop(0, cap // ROW_UNROLL, rows, 0)

    def tiles(first_chunk):
        wgb[...] = wg_ref[...].astype(BF16)
        wub[...] = wu_ref[...].astype(BF16)
        wdb[...] = wd_ref[...].astype(BF16)
        for m in range(n_tiles):
            r0 = m * MOE_TM
            if first_chunk:
                xm = jnp.concatenate([_tile_rows(xs3, r0, MOE_TM, s)[...].astype(BF16)
                                      for s in range(n_lane_groups)], axis=-1)
                xsb[r0:r0 + MOE_TM, :] = xm
            else:
                xm = xsb[r0:r0 + MOE_TM, :]
            gate = jnp.dot(xm, wgb[...], preferred_element_type=F32)
            first = (c * n_tiles + m) * per_tile
            b = first // cap
            for k in range(per_tile):
                fetch(nxt_ref, b, first - b * cap + k)
            up = jnp.dot(xm, wub[...], preferred_element_type=F32)
            hid = (gate * jax.nn.sigmoid(gate) * up).astype(BF16)
            contrib = jnp.dot(hid, wdb[...], preferred_element_type=F32)
            if first_chunk:
                y_ref[r0:r0 + MOE_TM, :] = contrib
            else:
                y_ref[r0:r0 + MOE_TM, :] += contrib

    @pl.when(c == 0)
    def _():
        wait_staging()
        tiles(True)

    @pl.when(c > 0)
    def _():
        tiles(False)

    @pl.when((e == ne - 1) & (c == nch - 1))
    def _drain():
        wait_staging()


def _moe(tok, h2_tiles, w_gate, w_up, w_down, bsz, t, cap):
    ne, d, dh = w_gate.shape
    nch = dh // MOE_HC
    n_rows = bsz * cap
    per_tile = n_rows // (nch * (n_rows // MOE_TM))
    assert cap % ROW_UNROLL == 0 and n_rows % MOE_TM == 0 and cap % per_tile == 0 and n_rows // nch <= MOE_TM
    assert d == SUBLANES * LANES
    tok_spec = lambda ahead: pl.BlockSpec((bsz, None, 1, cap),
                                          lambda e, c: (0, jnp.minimum(e + ahead, ne - 1), 0, 0),
                                          memory_space=pltpu.SMEM)
    return pl.pallas_call(
        functools.partial(_moe_body, bsz=bsz, t=t, cap=cap, ne=ne, nch=nch),
        grid=(ne, nch),
        in_specs=[tok_spec(0), tok_spec(1),
                  pl.BlockSpec(memory_space=pl.ANY),
                  pl.BlockSpec((None, d, MOE_HC), lambda e, c: (e, 0, c)),
                  pl.BlockSpec((None, d, MOE_HC), lambda e, c: (e, 0, c)),
                  pl.BlockSpec((None, MOE_HC, d), lambda e, c: (e, c, 0))],
        out_specs=pl.BlockSpec((None, n_rows, d), lambda e, c: (e, 0, 0)),
        out_shape=jax.ShapeDtypeStruct((ne, n_rows, d), F32),
        scratch_shapes=[pltpu.VMEM((n_rows * SUBLANES, LANES), F32),
                        pltpu.VMEM((n_rows, d), BF16),
                        pltpu.VMEM((d, MOE_HC), BF16),
                        pltpu.VMEM((d, MOE_HC), BF16),
                        pltpu.VMEM((MOE_HC, d), BF16),
                        pltpu.SemaphoreType.DMA],
        compiler_params=_params(("arbitrary", "arbitrary"), VMEM_LIMIT),
        name="moe",
    )(tok, tok, h2_tiles, w_gate, w_up, w_down)


COMBINE_CHUNK = 128


def _combine_body(tok_ref, aff_ref, y_ref, x1_ref, g2_ref, gf_ref, o_ref, acc, ysc, *, cap, ne, tm):
    s = pl.program_id(1)

    @pl.when(s == 0)
    def _():
        acc[...] = jnp.zeros_like(acc)

    @pl.when(s < ne)
    def _scatter():
        def tile_of(row):
            start = row * SUBLANES
            return pl.ds(start if isinstance(start, int) else pl.multiple_of(start, SUBLANES), SUBLANES)

        for r0 in range(0, cap, COMBINE_CHUNK):
            for g in range(SUBLANES):
                _tile_rows(ysc, r0, COMBINE_CHUNK, g)[...] = y_ref[r0:r0 + COMBINE_CHUNK, g * LANES:(g + 1) * LANES]
            for base in range(r0, r0 + COMBINE_CHUNK, ROW_UNROLL):
                tokens = [tok_ref[0, base + k] for k in range(ROW_UNROLL)]
                new = [acc[tile_of(tk), :] + aff_ref[0, tk] * ysc[tile_of(base + k), :]
                       for k, tk in enumerate(tokens)]
                for tk, val in zip(tokens, new):
                    acc[tile_of(tk), :] = val

    @pl.when(s >= ne)
    def _finish():
        first = pl.multiple_of((s - ne) * tm, tm)
        moe = jnp.concatenate([_tile_rows(acc, first, tm, g)[...] for g in range(SUBLANES)], axis=-1)
        x = x1_ref[...] + g2_ref[...] * moe
        ms = jnp.mean(x * x, axis=-1, keepdims=True)
        o_ref[...] = x * lax.rsqrt(ms + EPS) * gf_ref[...]


def _combine(tok, aff, y, x1, mods4, g_final, cap, tm):
    bsz, t, d = x1.shape
    ne = aff.shape[1]
    nt = t // tm
    assert d == SUBLANES * LANES
    expert = lambda s: jnp.minimum(s, ne - 1)
    tile = lambda s: jnp.maximum(s - ne, 0)
    smem = lambda n: pl.BlockSpec((None, None, 1, n), lambda b, s: (b, expert(s), 0, 0), memory_space=pltpu.SMEM)
    token_tile = pl.BlockSpec((None, tm, d), lambda b, s: (b, tile(s), 0))
    return pl.pallas_call(
        functools.partial(_combine_body, cap=cap, ne=ne, tm=tm),
        grid=(bsz, ne + nt),
        in_specs=[smem(cap), smem(t),
                  pl.BlockSpec((None, cap, d), lambda b, s: (expert(s), b, 0)),
                  token_tile,
                  _mod_spec(5, lambda b: b)(d),
                  pl.BlockSpec((1, d), lambda b, s: (0, 0))],
        out_specs=token_tile,
        out_shape=jax.ShapeDtypeStruct((bsz, t, d), F32),
        scratch_shapes=[pltpu.VMEM((t * SUBLANES, LANES), F32),
                        pltpu.VMEM((cap * SUBLANES, LANES), F32)],
        compiler_params=_params(("arbitrary", "arbitrary"), VMEM_LIMIT),
        name="combine",
    )(tok, aff, y, x1, mods4, g_final)


def _layer(x, ctx, mods4, ctx_row, norm_mix_g, w_in, conv_w, conv_b, conv_ln_g, conv_ln_b, rpb, w_out,
           norm_ffn_g, w_router, w_gate, w_up, w_down, norm_final_g):
    bsz, t, d = x.shape
    cw = conv_w.shape[1]
    n_heads = rpb.shape[0]
    aw = (w_in.shape[1] - 2 * cw) // 3
    head_dim = aw // n_heads
    rows = t // GRID_W
    ne = w_router.shape[1]
    cap = EC_FACTOR * t // ne
    row = lambda a: a.reshape(1, -1)

    w_in_bf = w_in.astype(BF16)
    u, q, k, v = _inproj(x, row(norm_mix_g), mods4, w_in_bf, cw, aw, head_dim ** -0.5, tm=1024)
    kx, vx = _inproj_ctx(ctx, row(norm_mix_g), mods4, w_in_bf, 2 * cw + aw, aw, ctx_row)
    y_conv = _conv(u, conv_w, conv_b, conv_ln_g, conv_ln_b, tt=1024)
    assert rows >= (rpb.shape[1] + 1) // 2
    y_attn = _attn(q, k, v, kx, vx, _bias_table(rpb))
    x1, h2_tiles, aff = _outproj(y_conv, y_attn, x, w_out.astype(BF16), mods4, row(norm_ffn_g),
                                 w_router, tm=1024)
    tok = _route(aff, cap).reshape(bsz, ne, 1, cap)
    y = _moe(tok, h2_tiles, w_gate, w_up, w_down, bsz, t, cap)
    return _combine(tok, aff.reshape(bsz, ne, 1, t), y, x1, mods4, row(norm_final_g), cap, tm=512)


def kernel(x, c, ctx, c_ctx, w_mod, b_mod, norm_mix_g, w_in, conv_w, conv_b, conv_ln_g, conv_ln_b, rpb,
           w_out, norm_ffn_g, w_router, w_gate, w_up, w_down, norm_final_g):
    bsz, t, d = x.shape
    depth = w_mod.shape[0]
    assert depth == 1
    assert bsz + 1 <= SUBLANES
    c8 = jnp.zeros((SUBLANES, d), F32).at[:bsz].set(c).at[bsz].set(c_ctx)
    mods4 = _adaln(c8, w_mod[0], b_mod[0].reshape(1, -1)).reshape(SUBLANES, N_MOD, 1, d)
    return _layer(x, ctx, mods4, bsz, norm_mix_g[0], w_in[0], conv_w[0], conv_b[0], conv_ln_g[0],
                  conv_ln_b[0], rpb[0], w_out[0], norm_ffn_g[0], w_router[0], w_gate[0], w_up[0],
                  w_down[0], norm_final_g)
```

```python
import functools

import jax
import jax.numpy as jnp
import numpy as np
from jax import lax
from jax.experimental import pallas as pl
from jax.experimental.pallas import tpu as pltpu

F32 = jnp.float32
BF16 = jnp.bfloat16
I32 = jnp.int32
HIGHEST = lax.Precision.HIGHEST

GRID_W = 64
NA_COLS = 16
EC_FACTOR = 2
N_MOD = 6
EPS = 1e-6
NEG_INF = -1e30
LANES = 128
SUBLANES = 8
VMEM_LIMIT = 56 * 1024 * 1024

_NT = (((1,), (1,)), ((), ()))


def _params(sem, vmem=None):
    return pltpu.CompilerParams(dimension_semantics=sem, vmem_limit_bytes=vmem)


def _tile_rows(ref, first, n, s):
    return ref.at[pl.ds(first * SUBLANES + s, n, stride=SUBLANES), :]


def _adaln_body(c_ref, w_ref, b_ref, o_ref):
    c = c_ref[...]
    s = c * jax.nn.sigmoid(c)
    o_ref[...] = jnp.dot(s, w_ref[...], preferred_element_type=F32, precision=HIGHEST) + b_ref[...]


def _adaln(c8, w_mod, b_mod):
    d, n = w_mod.shape
    bn = 512
    return pl.pallas_call(
        _adaln_body,
        grid=(n // bn,),
        in_specs=[pl.BlockSpec((SUBLANES, d), lambda j: (0, 0)),
                  pl.BlockSpec((d, bn), lambda j: (0, j)),
                  pl.BlockSpec((1, bn), lambda j: (0, j))],
        out_specs=pl.BlockSpec((SUBLANES, bn), lambda j: (0, j)),
        out_shape=jax.ShapeDtypeStruct((SUBLANES, n), F32),
        compiler_params=_params(("arbitrary",)),
        name="adaln",
    )(c8, w_mod, b_mod)


def _norm_mod(x, g, shift, scale):
    ms = jnp.mean(x * x, axis=-1, keepdims=True)
    h = x * lax.rsqrt(ms + EPS) * g
    return h * (1.0 + scale) + shift


def _inproj_body(x_ref, g_ref, sh_ref, sc_ref, w_ref, u_ref, q_ref, k_ref, v_ref, *, cw, aw, qscale):
    hb = _norm_mod(x_ref[...], g_ref[...], sh_ref[...], sc_ref[...]).astype(BF16)

    def proj(lo, width):
        return jnp.dot(hb, w_ref[:, lo:lo + width], preferred_element_type=F32)

    a = proj(0, cw)
    gate = proj(cw, cw)
    u_ref[...] = a * jax.nn.sigmoid(gate)
    q_ref[...] = (proj(2 * cw, aw) * qscale).astype(BF16)
    k_ref[...] = proj(2 * cw + aw, aw).astype(BF16)
    v_ref[...] = proj(2 * cw + 2 * aw, aw).astype(BF16)


def _inproj_kv_body(x_ref, g_ref, sh_ref, sc_ref, wk_ref, wv_ref, k_ref, v_ref):
    hb = _norm_mod(x_ref[...], g_ref[...], sh_ref[...], sc_ref[...]).astype(BF16)
    k_ref[...] = jnp.dot(hb, wk_ref[...], preferred_element_type=F32).astype(BF16)
    v_ref[...] = jnp.dot(hb, wv_ref[...], preferred_element_type=F32).astype(BF16)


def _mod_spec(slot, row_of):
    return lambda d: pl.BlockSpec((None, None, 1, d), lambda b, i: (row_of(b), slot, 0, 0))


def _inproj(x, g, mods4, w_bf, cw, aw, qscale, tm):
    bsz, t, d = x.shape
    tok = lambda w, dt: (pl.BlockSpec((None, tm, w), lambda b, i: (b, i, 0)), jax.ShapeDtypeStruct((bsz, t, w), dt))
    outs = [tok(cw, F32), tok(aw, BF16), tok(aw, BF16), tok(aw, BF16)]
    return pl.pallas_call(
        functools.partial(_inproj_body, cw=cw, aw=aw, qscale=qscale),
        grid=(bsz, t // tm),
        in_specs=[pl.BlockSpec((None, tm, d), lambda b, i: (b, i, 0)),
                  pl.BlockSpec((1, d), lambda b, i: (0, 0)),
                  _mod_spec(0, lambda b: b)(d),
                  _mod_spec(1, lambda b: b)(d),
                  pl.BlockSpec(w_bf.shape, lambda b, i: (0, 0))],
        out_specs=[o[0] for o in outs],
        out_shape=[o[1] for o in outs],
        compiler_params=_params(("arbitrary", "arbitrary"), VMEM_LIMIT),
        name="inproj",
    )(x, g, mods4, mods4, w_bf)


def _inproj_ctx(ctx, g, mods4, w_bf, k_lo, aw, ctx_row):
    bsz, l, d = ctx.shape
    assert k_lo % aw == 0
    spec = pl.BlockSpec((None, l, aw), lambda b, i: (b, 0, 0))
    shape = jax.ShapeDtypeStruct((bsz, l, aw), BF16)
    w_cols = lambda j: pl.BlockSpec((d, aw), lambda b, i: (0, k_lo // aw + j))
    return pl.pallas_call(
        _inproj_kv_body,
        grid=(bsz, 1),
        in_specs=[pl.BlockSpec((None, l, d), lambda b, i: (b, 0, 0)),
                  pl.BlockSpec((1, d), lambda b, i: (0, 0)),
                  _mod_spec(0, lambda b: ctx_row)(d),
                  _mod_spec(1, lambda b: ctx_row)(d),
                  w_cols(0), w_cols(1)],
        out_specs=[spec, spec],
        out_shape=[shape, shape],
        compiler_params=_params(("arbitrary", "arbitrary"), VMEM_LIMIT),
        name="inproj_ctx",
    )(ctx, g, mods4, mods4, w_bf, w_bf)


CONV_HALO = 16
CONV_RC = 128
CONV_CW = 128


def _conv_body(prev_ref, cur_ref, next_ref, w_ref, b_ref, lg_ref, lb_ref, o_ref, ext, ysc, *, width, tt):
    i = pl.program_id(1)
    n = pl.num_programs(1)
    ext[0:CONV_HALO, :] = jnp.where(i > 0, prev_ref[...], 0.0)
    ext[CONV_HALO:CONV_HALO + tt, :] = cur_ref[...]
    ext[CONV_HALO + tt:, :] = jnp.where(i < n - 1, next_ref[...], 0.0)
    lead = CONV_HALO - width // 2
    for rc in range(tt // CONV_RC):
        r0 = rc * CONV_RC
        for ch in range(cur_ref.shape[-1] // CONV_CW):
            cs = slice(ch * CONV_CW, (ch + 1) * CONV_CW)
            out = None
            for m in range(SUBLANES):
                nrow = CONV_RC + (SUBLANES if m else 0)
                part = None
                for j in range(width):
                    if (lead + j) % SUBLANES != m:
                        continue
                    a = r0 + lead + j - m
                    term = w_ref[j:j + 1, cs] * ext[a:a + nrow, cs]
                    part = term if part is None else part + term
                piece = part[m:m + CONV_RC]
                out = piece if out is None else out + piece
            ysc[r0:r0 + CONV_RC, cs] = out
    y = ysc[...] + b_ref[...]
    mu = jnp.mean(y, axis=-1, keepdims=True)
    yc = y - mu
    var = jnp.mean(yc * yc, axis=-1, keepdims=True)
    yn = yc * lax.rsqrt(var + EPS) * lg_ref[...] + lb_ref[...]
    o_ref[...] = (yn * jax.nn.sigmoid(yn)).astype(BF16)


def _conv(u, conv_w, conv_b, ln_g, ln_b, tt):
    bsz, t, cw = u.shape
    width = conv_w.shape[0]
    assert width // 2 < CONV_HALO and tt % CONV_HALO == 0 and tt % CONV_RC == 0 and cw % CONV_CW == 0
    hb = tt // CONV_HALO
    nh = t // CONV_HALO
    row = lambda a: a.reshape(1, cw)
    return pl.pallas_call(
        functools.partial(_conv_body, width=width, tt=tt),
        grid=(bsz, t // tt),
        in_specs=[pl.BlockSpec((None, CONV_HALO, cw), lambda b, i: (b, jnp.maximum(i * hb - 1, 0), 0)),
                  pl.BlockSpec((None, tt, cw), lambda b, i: (b, i, 0)),
                  pl.BlockSpec((None, CONV_HALO, cw), lambda b, i: (b, jnp.minimum((i + 1) * hb, nh - 1), 0)),
                  pl.BlockSpec((width, cw), lambda b, i: (0, 0)),
                  pl.BlockSpec((1, cw), lambda b, i: (0, 0)),
                  pl.BlockSpec((1, cw), lambda b, i: (0, 0)),
                  pl.BlockSpec((1, cw), lambda b, i: (0, 0))],
        out_specs=pl.BlockSpec((None, tt, cw), lambda b, i: (b, i, 0)),
        out_shape=jax.ShapeDtypeStruct((bsz, t, cw), BF16),
        scratch_shapes=[pltpu.VMEM((tt + 2 * CONV_HALO, cw), F32), pltpu.VMEM((tt, cw), F32)],
        compiler_params=_params(("arbitrary", "arbitrary")),
        name="conv",
    )(u, u, u, conv_w, row(conv_b), row(ln_g), row(ln_b))


ATT_RB = 8


def _bias_body(rpb_ref, onehot_ref, valid_ref, o_ref):
    t = jnp.dot(rpb_ref[...], onehot_ref[...], preferred_element_type=F32, precision=HIGHEST)
    o_ref[...] = jnp.where(valid_ref[...] > 0.5, t, NEG_INF)


def _bias_table(rpb):
    nh, ndr, ndc = rpb.shape
    qc = np.arange(GRID_W)[:, None]
    kc = np.arange(GRID_W)[None, :]
    c0 = np.clip(qc - NA_COLS // 2, 0, GRID_W - NA_COLS)
    valid = ((kc >= c0) & (kc < c0 + NA_COLS)).astype(np.float32).reshape(1, GRID_W * GRID_W)
    d_col = np.clip(kc - qc + (NA_COLS - 1), 0, 2 * NA_COLS - 2).reshape(1, GRID_W * GRID_W)
    ndc_pad = -(-ndc // SUBLANES) * SUBLANES
    onehot = (np.arange(ndc_pad)[:, None] == d_col).astype(np.float32)
    rpb2 = jnp.pad(rpb.reshape(nh * ndr, ndc), ((0, 0), (0, ndc_pad - ndc)))
    toeplitz = pl.pallas_call(
        _bias_body,
        out_shape=jax.ShapeDtypeStruct((nh * ndr, GRID_W * GRID_W), F32),
        name="bias",
    )(rpb2, onehot, valid).reshape(nh, ndr, GRID_W, GRID_W)
    return jnp.concatenate([toeplitz[:, :-1], toeplitz[:, 1:]], axis=-1)


def _attn_body(q_ref, kp_ref, kc_ref, kn_ref, vp_ref, vc_ref, vn_ref, kx_ref, vx_ref, bias_ref, o_ref,
               kwin, vwin, s_sc, p_sc, m_sc, l_sc, *, rows, kh, n_pairs):
    i = pl.program_id(1)
    blk = ATT_RB * GRID_W
    for w, (a, b) in enumerate(((kp_ref, vp_ref), (kc_ref, vc_ref), (kn_ref, vn_ref))):
        kwin[w * blk:(w + 1) * blk, :] = a[...]
        vwin[w * blk:(w + 1) * blk, :] = b[...]
    lo = lax.broadcasted_iota(I32, (GRID_W, LANES), 1) < LANES // 2
    band = kh * GRID_W
    dyn = jnp.minimum(i, 0)

    def lane_groups(x, op):
        parts = [x[:, g * LANES:(g + 1) * LANES] for g in range(x.shape[-1] // LANES)]
        while len(parts) > 1:
            parts = [op(parts[a], parts[a + 1]) for a in range(0, len(parts) - 1, 2)] + parts[len(parts) & ~1:]
        return parts[0]

    def row_body(rr, carry):
        r = i * ATT_RB + rr
        r0 = jnp.clip(r - kh // 2, 0, rows - kh)
        delta = r - r0
        off = pl.multiple_of((r0 - (i - 1) * ATT_RB) * GRID_W, GRID_W)
        qoff = pl.multiple_of(rr * GRID_W, GRID_W)
        def scores(p):
            cols = slice(p * LANES, (p + 1) * LANES)
            qp = q_ref[pl.ds(qoff, GRID_W), cols]
            zeros = jnp.zeros_like(qp)
            q2 = jnp.concatenate([jnp.where(lo, qp, zeros), jnp.where(lo, zeros, qp)], axis=0)
            bias = jnp.concatenate(
                [jnp.concatenate([bias_ref[2 * p + s, 2 * g + (kh - 1) - delta] for g in range(kh // 2)], axis=-1)
                 for s in range(2)], axis=0)
            s_loc = lax.dot_general(q2, kwin[pl.ds(off, band), cols], _NT, preferred_element_type=F32) + bias
            s_ctx = lax.dot_general(q2, kx_ref[:, cols], _NT, preferred_element_type=F32)
            s_sc[dyn + p, :, 0:band] = s_loc
            s_sc[dyn + p, :, band:] = s_ctx
            m_sc[p] = jnp.maximum(lane_groups(s_loc, jnp.maximum), lane_groups(s_ctx, jnp.maximum))

        def numerators(p):
            m = jnp.max(m_sc[p], axis=-1, keepdims=True)
            e = jnp.exp(s_sc[dyn + p] - m)
            p_sc[dyn + p] = e.astype(BF16)
            l_sc[p] = lane_groups(e, jnp.add)

        def values(p):
            cols = slice(p * LANES, (p + 1) * LANES)
            o = (jnp.dot(p_sc[dyn + p, :, 0:band], vwin[pl.ds(off, band), cols], preferred_element_type=F32)
                 + jnp.dot(p_sc[dyn + p, :, band:], vx_ref[:, cols], preferred_element_type=F32))
            o = o / jnp.sum(l_sc[p], axis=-1, keepdims=True)
            o_ref[pl.ds(qoff, GRID_W), cols] = jnp.where(lo, o[0:GRID_W], o[GRID_W:]).astype(BF16)

        for stage in (scores, numerators, values):
            for p in range(n_pairs):
                stage(p)
        return carry

    lax.fori_loop(0, ATT_RB, row_body, 0, unroll=True)


def _attn(q, k, v, kx, vx, bias):
    bsz, t, aw = q.shape
    l = kx.shape[1]
    rows = t // GRID_W
    kh = (bias.shape[1] + 2) // 2
    nb = rows // ATT_RB
    blk = ATT_RB * GRID_W
    assert kh == ATT_RB and rows % ATT_RB == 0 and aw % LANES == 0 and 2 * GRID_W == LANES
    cur = pl.BlockSpec((None, blk, aw), lambda b, i: (b, i, 0))
    prev = pl.BlockSpec((None, blk, aw), lambda b, i: (b, jnp.maximum(i - 1, 0), 0))
    nxt = pl.BlockSpec((None, blk, aw), lambda b, i: (b, jnp.minimum(i + 1, nb - 1), 0))
    cx = pl.BlockSpec((None, l, aw), lambda b, i: (b, 0, 0))
    nh = bias.shape[0]
    nkeys = kh * GRID_W + l
    assert l % LANES == 0 and nh == 2 * (aw // LANES)
    return pl.pallas_call(
        functools.partial(_attn_body, rows=rows, kh=kh, n_pairs=aw // LANES),
        grid=(bsz, nb),
        in_specs=[cur, prev, cur, nxt, prev, cur, nxt, cx, cx,
                  pl.BlockSpec(bias.shape, lambda b, i: (0, 0, 0, 0))],
        out_specs=cur,
        out_shape=jax.ShapeDtypeStruct((bsz, t, aw), BF16),
        scratch_shapes=[pltpu.VMEM((3 * blk, aw), BF16), pltpu.VMEM((3 * blk, aw), BF16),
                        pltpu.VMEM((nh // 2, 2 * GRID_W, nkeys), F32), pltpu.VMEM((nh // 2, 2 * GRID_W, nkeys), BF16),
                        pltpu.VMEM((nh // 2, 2 * GRID_W, LANES), F32), pltpu.VMEM((nh // 2, 2 * GRID_W, LANES), F32)],
        compiler_params=_params(("arbitrary", "arbitrary"), VMEM_LIMIT),
        name="attn",
    )(q, k, k, k, v, v, v, kx, vx, bias)


OUT_GROUPS = 4


def _outproj_body(yc_ref, ya_ref, x_ref, wo_ref, g1_ref, gf_ref, sh_ref, sc_ref, wr_ref,
                  x1_ref, h2_ref, aff_ref, *, cw, tm, ne):
    rg = tm // OUT_GROUPS

    def project(g):
        rows = slice(g * rg, (g + 1) * rg)
        acc = (jnp.dot(yc_ref[rows, :], wo_ref[0:cw, :], preferred_element_type=F32)
               + jnp.dot(ya_ref[rows, :], wo_ref[cw:, :], preferred_element_type=F32))
        x1 = x_ref[rows, :] + g1_ref[...] * acc
        x1_ref[rows, :] = x1
        return x1

    def route(g, x1):
        h2 = _norm_mod(x1, gf_ref[...], sh_ref[...], sc_ref[...])
        for s in range(h2.shape[-1] // LANES):
            _tile_rows(h2_ref, g * rg, rg, s)[...] = h2[:, s * LANES:(s + 1) * LANES]
        hi = h2.astype(BF16)
        lo = (h2 - hi.astype(F32)).astype(BF16)
        part_hi = jnp.dot(hi, wr_ref[...], preferred_element_type=F32)
        part_lo = jnp.dot(lo, wr_ref[...], preferred_element_type=F32)
        logits = part_hi + pltpu.roll(part_hi, LANES - ne, axis=1) + part_lo
        logits = jnp.where(lax.broadcasted_iota(I32, logits.shape, 1) < ne, logits, NEG_INF)
        e = jnp.exp(logits - jnp.max(logits, axis=-1, keepdims=True))
        aff = (e / jnp.sum(e, axis=-1, keepdims=True)).T[0:ne, :]
        aff_ref[:, g * rg:(g + 1) * rg] = aff

    x1s = [project(g) for g in range(OUT_GROUPS)]
    for g in range(OUT_GROUPS):
        route(g, x1s[g])


def _outproj(y_conv, y_attn, x, wo_bf, mods4, g_ffn, w_router, tm):
    bsz, t, d = x.shape
    cw = y_conv.shape[-1]
    ne = w_router.shape[1]
    assert d == SUBLANES * LANES and 2 * ne <= LANES
    nt = t // tm
    w_hi = w_router.astype(BF16)
    w_lo = (w_router - w_hi.astype(F32)).astype(BF16)
    wr_t = jnp.zeros((d, LANES), BF16).at[:, 0:ne].set(w_hi).at[:, ne:2 * ne].set(w_lo)
    return pl.pallas_call(
        functools.partial(_outproj_body, cw=cw, tm=tm, ne=ne),
        grid=(bsz, nt),
        in_specs=[pl.BlockSpec((None, tm, cw), lambda b, i: (b, i, 0)),
                  pl.BlockSpec((None, tm, y_attn.shape[-1]), lambda b, i: (b, i, 0)),
                  pl.BlockSpec((None, tm, d), lambda b, i: (b, i, 0)),
                  pl.BlockSpec(wo_bf.shape, lambda b, i: (0, 0)),
                  _mod_spec(2, lambda b: b)(d),
                  pl.BlockSpec((1, d), lambda b, i: (0, 0)),
                  _mod_spec(3, lambda b: b)(d),
                  _mod_spec(4, lambda b: b)(d),
                  pl.BlockSpec(wr_t.shape, lambda b, i: (0, 0))],
        out_specs=[pl.BlockSpec((None, tm, d), lambda b, i: (b, i, 0)),
                   pl.BlockSpec((tm * SUBLANES, LANES), lambda b, i: (b * nt + i, 0)),
                   pl.BlockSpec((None, ne, tm), lambda b, i: (b, 0, i))],
        out_shape=[jax.ShapeDtypeStruct((bsz, t, d), F32),
                   jax.ShapeDtypeStruct((bsz * t * SUBLANES, LANES), F32),
                   jax.ShapeDtypeStruct((bsz, ne, t), F32)],
        compiler_params=_params(("arbitrary", "arbitrary"), VMEM_LIMIT),
        name="outproj",
    )(y_conv, y_attn, x, wo_bf, mods4, g_ffn, mods4, mods4, wr_t)


def _route_body(affe_ref, affr_ref, u_ref, ones_ref, l_ref, tok_ref, *, cap, ne, nblk):
    aff_e = affe_ref[...]

    def search(it, bits):
        cand = bits | jnp.left_shift(jnp.int32(1), 30 - it)
        cnt = jnp.sum(jnp.where(aff_e >= pltpu.bitcast(cand, F32), 1.0, 0.0), axis=1, keepdims=True)
        return jnp.where(cnt >= cap, cand, bits)

    tau = pltpu.bitcast(lax.fori_loop(0, 31, search, jnp.zeros((ne, 1), I32)), F32)
    n_gt = jnp.sum(jnp.where(aff_e > tau, 1.0, 0.0), axis=1, keepdims=True)
    need = cap - n_gt

    def per_row(col):
        return jnp.concatenate([jnp.broadcast_to(col[e:e + 1, :], (nblk, 1)) for e in range(ne)], axis=0)

    tau_r = per_row(tau)
    need_r = per_row(need)
    kr = affr_ref[...]

    def scan(mf):
        mb = mf.astype(BF16)
        incl = jnp.dot(mb, u_ref[...], preferred_element_type=F32)
        tot = jnp.dot(mb, ones_ref[...], preferred_element_type=F32)
        off = jnp.dot(l_ref[...], tot.astype(BF16), preferred_element_type=F32)
        return incl, tot, off

    eq = jnp.where(kr == tau_r, 1.0, 0.0)
    incl_e, _, off_e = scan(eq)
    rank_eq = off_e + incl_e - eq
    sel = jnp.where((kr > tau_r) | ((eq > 0.5) & (rank_eq < need_r)), 1.0, 0.0)
    incl, _, off = scan(sel)

    nbits = LANES.bit_length() - 1
    lane = lax.broadcasted_iota(I32, kr.shape, 1)
    row = lax.broadcasted_iota(I32, kr.shape, 0)
    dist = lane - (incl.astype(I32) - 1)
    word = jnp.where(sel > 0.5, (1 << (2 * nbits)) | (dist << nbits) | lane, 0)

    def bit(x, pos):
        return (x >> pos) & 1

    for k in range(nbits):
        arriving = pltpu.roll(word, LANES - (1 << k), axis=1)
        take = bit(arriving, 2 * nbits) & bit(arriving, nbits + k)
        stay = bit(word, 2 * nbits) & (1 - bit(word, nbits + k))
        word = jnp.where(take == 1, arriving, jnp.where(stay == 1, word, 0))

    off_i = off.astype(I32)
    shift = off_i & (LANES - 1)
    for k in range(nbits):
        word = jnp.where(bit(shift, k) == 1, pltpu.roll(word, 1 << k, axis=1), word)
    valid = bit(word, 2 * nbits) == 1
    wrapped = lane < shift
    blk_bits = nblk.bit_length() - 1
    q0 = (row >> blk_bits) * (cap // LANES) + (off_i >> nbits)
    digits = (jnp.where(valid, row & (nblk - 1), 0), jnp.where(valid, word & (LANES - 1), 0))

    def place(sel_rows, q):
        onehot_t = jnp.where(q == lane, 1.0, 0.0).astype(BF16)
        outs = []
        for dgt in digits:
            x = jnp.where(sel_rows, dgt, 0).astype(F32).astype(BF16)
            outs.append(lax.dot_general(onehot_t, x, (((0,), (0,)), ((), ())), preferred_element_type=F32))
        return outs[0] * LANES + outs[1]

    tok_ref[...] = (place(jnp.logical_not(wrapped), q0) + place(wrapped, q0 + 1)).astype(I32)


def _route(aff, cap):
    bsz, ne, t = aff.shape
    nblk = t // LANES
    nr = ne * nblk
    nq = ne * cap // LANES
    assert nq == LANES and cap % LANES == 0 and nblk & (nblk - 1) == 0 and nblk <= 256
    aff_r = aff.reshape(bsz, nr, LANES)
    ar = np.arange(LANES)
    upper = jnp.asarray(ar[:, None] <= ar[None, :], BF16)
    ones = jnp.ones((LANES, LANES), BF16)
    rr = np.arange(nr)
    lower = jnp.asarray((rr[:, None] // nblk == rr[None, :] // nblk) & (rr[None, :] < rr[:, None]), BF16)
    const = lambda a: pl.BlockSpec(a.shape, lambda b: (0,) * a.ndim)
    return pl.pallas_call(
        functools.partial(_route_body, cap=cap, ne=ne, nblk=nblk),
        grid=(bsz,),
        in_specs=[pl.BlockSpec((None, ne, t), lambda b: (b, 0, 0)),
                  pl.BlockSpec((None, nr, LANES), lambda b: (b, 0, 0)),
                  const(upper), const(ones), const(lower)],
        out_specs=pl.BlockSpec((None, nq, LANES), lambda b: (b, 0, 0)),
        out_shape=jax.ShapeDtypeStruct((bsz, nq, LANES), I32),
        compiler_params=_params(("arbitrary",), VMEM_LIMIT),
        name="route",
    )(aff, aff_r, upper, ones, lower)


MOE_HC = 512
MOE_TM = 512
ROW_UNROLL = 8


def _gather_copy(h2_hbm, xs3, sem, token, slot):
    src = h2_hbm.at[pl.ds(pl.multiple_of(token * SUBLANES, SUBLANES), SUBLANES), :]
    dst = xs3.at[pl.ds(pl.multiple_of(slot * SUBLANES, SUBLANES), SUBLANES), :]
    return pltpu.make_async_copy(src, dst, sem)


def _moe_body(tok_ref, nxt_ref, h2_hbm, wg_ref, wu_ref, wd_ref, y_ref, xs3, xsb, wgb, wub, wdb, sem,
              *, bsz, t, cap, ne, nch):
    e = pl.program_id(0)
    c = pl.program_id(1)
    n_rows = bsz * cap
    n_tiles = n_rows // MOE_TM
    per_tile = n_rows // (nch * n_tiles)
    n_lane_groups = xsb.shape[-1] // LANES

    def fetch(ref, b, slot):
        _gather_copy(h2_hbm, xs3, sem, b * t + ref[b, 0, slot], b * cap + slot).start()

    def wait_staging():
        pltpu.make_async_copy(h2_hbm.at[pl.ds(0, n_rows * SUBLANES), :], xs3, sem).wait()

    @pl.when((e == 0) & (c == 0))
    def _first_expert():
        for b in range(bsz):
            def rows(g, carry):
                for k in range(ROW_UNROLL):
                    fetch(tok_ref, b, g * ROW_UNROLL + k)
                return carry

            lax.fori_loop(0, cap // ROW_UNROLL, rows, 0)

    def tiles(first_chunk):
        wgb[...] = wg_ref[...].astype(BF16)
        wub[...] = wu_ref[...].astype(BF16)
        wdb[...] = wd_ref[...].astype(BF16)
        for m in range(n_tiles):
            r0 = m * MOE_TM
            if first_chunk:
                xm = jnp.concatenate([_tile_rows(xs3, r0, MOE_TM, s)[...].astype(BF16)
                                      for s in range(n_lane_groups)], axis=-1)
                xsb[r0:r0 + MOE_TM, :] = xm
            else:
                xm = xsb[r0:r0 + MOE_TM, :]
            gate = jnp.dot(xm, wgb[...], preferred_element_type=F32)
            first = (c * n_tiles + m) * per_tile
            b = first // cap
            for k in range(per_tile):
                fetch(nxt_ref, b, first - b * cap + k)
            up = jnp.dot(xm, wub[...], preferred_element_type=F32)
            hid = (gate * jax.nn.sigmoid(gate) * up).astype(BF16)
            contrib = jnp.dot(hid, wdb[...], preferred_element_type=F32)
            if first_chunk:
                y_ref[r0:r0 + MOE_TM, :] = contrib
            else:
                y_ref[r0:r0 + MOE_TM, :] += contrib

    @pl.when(c == 0)
    def _():
        wait_staging()
        tiles(True)

    @pl.when(c > 0)
    def _():
        tiles(False)

    @pl.when((e == ne - 1) & (c == nch - 1))
    def _drain():
        wait_staging()


def _moe(tok, h2_tiles, w_gate, w_up, w_down, bsz, t, cap):
    ne, d, dh = w_gate.shape
    nch = dh // MOE_HC
    n_rows = bsz * cap
    per_tile = n_rows // (nch * (n_rows // MOE_TM))
    assert cap % ROW_UNROLL == 0 and n_rows % MOE_TM == 0 and cap % per_tile == 0 and n_rows // nch <= MOE_TM
    assert d == SUBLANES * LANES
    tok_spec = lambda ahead: pl.BlockSpec((bsz, None, 1, cap),
                                          lambda e, c: (0, jnp.minimum(e + ahead, ne - 1), 0, 0),
                                          memory_space=pltpu.SMEM)
    return pl.pallas_call(
        functools.partial(_moe_body, bsz=bsz, t=t, cap=cap, ne=ne, nch=nch),
        grid=(ne, nch),
        in_specs=[tok_spec(0), tok_spec(1),
                  pl.BlockSpec(memory_space=pl.ANY),
                  pl.BlockSpec((None, d, MOE_HC), lambda e, c: (e, 0, c)),
                  pl.BlockSpec((None, d, MOE_HC), lambda e, c: (e, 0, c)),
                  pl.BlockSpec((None, MOE_HC, d), lambda e, c: (e, c, 0))],
        out_specs=pl.BlockSpec((None, n_rows, d), lambda e, c: (e, 0, 0)),
        out_shape=jax.ShapeDtypeStruct((ne, n_rows, d), F32),
        scratch_shapes=[pltpu.VMEM((n_rows * SUBLANES, LANES), F32),
                        pltpu.VMEM((n_rows, d), BF16),
                        pltpu.VMEM((d, MOE_HC), BF16),
                        pltpu.VMEM((d, MOE_HC), BF16),
                        pltpu.VMEM((MOE_HC, d), BF16),
                        pltpu.SemaphoreType.DMA],
        compiler_params=_params(("arbitrary", "arbitrary"), VMEM_LIMIT),
        name="moe",
    )(tok, tok, h2_tiles, w_gate, w_up, w_down)


COMBINE_CHUNK = 128


def _combine_body(tok_ref, aff_ref, y_ref, x1_ref, g2_ref, gf_ref, o_ref, acc, ysc, *, cap, ne, tm):
    s = pl.program_id(1)

    @pl.when(s == 0)
    def _():
        acc[...] = jnp.zeros_like(acc)

    @pl.when(s < ne)
    def _scatter():
        def tile_of(row):
            start = row * SUBLANES
            return pl.ds(start if isinstance(start, int) else pl.multiple_of(start, SUBLANES), SUBLANES)

        for r0 in range(0, cap, COMBINE_CHUNK):
            for g in range(SUBLANES):
                _tile_rows(ysc, r0, COMBINE_CHUNK, g)[...] = y_ref[r0:r0 + COMBINE_CHUNK, g * LANES:(g + 1) * LANES]
            for base in range(r0, r0 + COMBINE_CHUNK, ROW_UNROLL):
                tokens = [tok_ref[0, base + k] for k in range(ROW_UNROLL)]
                new = [acc[tile_of(tk), :] + aff_ref[0, tk] * ysc[tile_of(base + k), :]
                       for k, tk in enumerate(tokens)]
                for tk, val in zip(tokens, new):
                    acc[tile_of(tk), :] = val

    @pl.when(s >= ne)
    def _finish():
        first = pl.multiple_of((s - ne) * tm, tm)
        moe = jnp.concatenate([_tile_rows(acc, first, tm, g)[...] for g in range(SUBLANES)], axis=-1)
        x = x1_ref[...] + g2_ref[...] * moe
        ms = jnp.mean(x * x, axis=-1, keepdims=True)
        o_ref[...] = x * lax.rsqrt(ms + EPS) * gf_ref[...]


def _combine(tok, aff, y, x1, mods4, g_final, cap, tm):
    bsz, t, d = x1.shape
    ne = aff.shape[1]
    nt = t // tm
    assert d == SUBLANES * LANES
    expert = lambda s: jnp.minimum(s, ne - 1)
    tile = lambda s: jnp.maximum(s - ne, 0)
    smem = lambda n: pl.BlockSpec((None, None, 1, n), lambda b, s: (b, expert(s), 0, 0), memory_space=pltpu.SMEM)
    token_tile = pl.BlockSpec((None, tm, d), lambda b, s: (b, tile(s), 0))
    return pl.pallas_call(
        functools.partial(_combine_body, cap=cap, ne=ne, tm=tm),
        grid=(bsz, ne + nt),
        in_specs=[smem(cap), smem(t),
                  pl.BlockSpec((None, cap, d), lambda b, s: (expert(s), b, 0)),
                  token_tile,
                  _mod_spec(5, lambda b: b)(d),
                  pl.BlockSpec((1, d), lambda b, s: (0, 0))],
        out_specs=token_tile,
        out_shape=jax.ShapeDtypeStruct((bsz, t, d), F32),
        scratch_shapes=[pltpu.VMEM((t * SUBLANES, LANES), F32),
                        pltpu.VMEM((cap * SUBLANES, LANES), F32)],
        compiler_params=_params(("arbitrary", "arbitrary"), VMEM_LIMIT),
        name="combine",
    )(tok, aff, y, x1, mods4, g_final)


def _layer(x, ctx, mods4, ctx_row, norm_mix_g, w_in, conv_w, conv_b, conv_ln_g, conv_ln_b, rpb, w_out,
           norm_ffn_g, w_router, w_gate, w_up, w_down, norm_final_g):
    bsz, t, d = x.shape
    cw = conv_w.shape[1]
    n_heads = rpb.shape[0]
    aw = (w_in.shape[1] - 2 * cw) // 3
    head_dim = aw // n_heads
    rows = t // GRID_W
    ne = w_router.shape[1]
    cap = EC_FACTOR * t // ne
    row = lambda a: a.reshape(1, -1)

    w_in_bf = w_in.astype(BF16)
    u, q, k, v = _inproj(x, row(norm_mix_g), mods4, w_in_bf, cw, aw, head_dim ** -0.5, tm=1024)
    kx, vx = _inproj_ctx(ctx, row(norm_mix_g), mods4, w_in_bf, 2 * cw + aw, aw, ctx_row)
    y_conv = _conv(u, conv_w, conv_b, conv_ln_g, conv_ln_b, tt=1024)
    assert rows >= (rpb.shape[1] + 1) // 2
    y_attn = _attn(q, k, v, kx, vx, _bias_table(rpb))
    x1, h2_tiles, aff = _outproj(y_conv, y_attn, x, w_out.astype(BF16), mods4, row(norm_ffn_g),
                                 w_router, tm=1024)
    tok = _route(aff, cap).reshape(bsz, ne, 1, cap)
    y = _moe(tok, h2_tiles, w_gate, w_up, w_down, bsz, t, cap)
    return _combine(tok, aff.reshape(bsz, ne, 1, t), y, x1, mods4, row(norm_final_g), cap, tm=512)


def kernel(x, c, ctx, c_ctx, w_mod, b_mod, norm_mix_g, w_in, conv_w, conv_b, conv_ln_g, conv_ln_b, rpb,
           w_out, norm_ffn_g, w_router, w_gate, w_up, w_down, norm_final_g):
    bsz, t, d = x.shape
    depth = w_mod.shape[0]
    assert depth == 1
    assert bsz + 1 <= SUBLANES
    c8 = jnp.zeros((SUBLANES, d), F32).at[:bsz].set(c).at[bsz].set(c_ctx)
    mods4 = _adaln(c8, w_mod[0], b_mod[0].reshape(1, -1)).reshape(SUBLANES, N_MOD, 1, d)
    return _layer(x, ctx, mods4, bsz, norm_mix_g[0], w_in[0], conv_w[0], conv_b[0], conv_ln_g[0],
                  conv_ln_b[0], rpb[0], w_out[0], norm_ffn_g[0], w_router[0], w_gate[0], w_up[0],
                  w_down[0], norm_final_g)
```

```python
import functools

import jax
import jax.numpy as jnp
import numpy as np
from jax import lax
from jax.experimental import pallas as pl
from jax.experimental.pallas import tpu as pltpu

F32 = jnp.float32
BF16 = jnp.bfloat16
I32 = jnp.int32
HIGHEST = lax.Precision.HIGHEST

GRID_W = 64
NA_COLS = 16
EC_FACTOR = 2
N_MOD = 6
EPS = 1e-6
NEG_INF = -1e30
LANES = 128
SUBLANES = 8
VMEM_LIMIT = 56 * 1024 * 1024

_NT = (((1,), (1,)), ((), ()))


def _params(sem, vmem=None):
    return pltpu.CompilerParams(dimension_semantics=sem, vmem_limit_bytes=vmem)


def _tile_rows(ref, first, n, s):
    return ref.at[pl.ds(first * SUBLANES + s, n, stride=SUBLANES), :]


def _adaln_body(c_ref, w_ref, b_ref, o_ref):
    c = c_ref[...]
    s = c * jax.nn.sigmoid(c)
    o_ref[...] = jnp.dot(s, w_ref[...], preferred_element_type=F32, precision=HIGHEST) + b_ref[...]


def _adaln(c8, w_mod, b_mod):
    d, n = w_mod.shape
    bn = 512
    return pl.pallas_call(
        _adaln_body,
        grid=(n // bn,),
        in_specs=[pl.BlockSpec((SUBLANES, d), lambda j: (0, 0)),
                  pl.BlockSpec((d, bn), lambda j: (0, j)),
                  pl.BlockSpec((1, bn), lambda j: (0, j))],
        out_specs=pl.BlockSpec((SUBLANES, bn), lambda j: (0, j)),
        out_shape=jax.ShapeDtypeStruct((SUBLANES, n), F32),
        compiler_params=_params(("arbitrary",)),
        name="adaln",
    )(c8, w_mod, b_mod)


def _norm_mod(x, g, shift, scale):
    ms = jnp.mean(x * x, axis=-1, keepdims=True)
    h = x * lax.rsqrt(ms + EPS) * g
    return h * (1.0 + scale) + shift


def _inproj_body(x_ref, g_ref, sh_ref, sc_ref, w_ref, u_ref, q_ref, k_ref, v_ref, *, cw, aw, qscale):
    hb = _norm_mod(x_ref[...], g_ref[...], sh_ref[...], sc_ref[...]).astype(BF16)

    def proj(lo, width):
        return jnp.dot(hb, w_ref[:, lo:lo + width], preferred_element_type=F32)

    a = proj(0, cw)
    gate = proj(cw, cw)
    u_ref[...] = a * jax.nn.sigmoid(gate)
    q_ref[...] = (proj(2 * cw, aw) * qscale).astype(BF16)
    k_ref[...] = proj(2 * cw + aw, aw).astype(BF16)
    v_ref[...] = proj(2 * cw + 2 * aw, aw).astype(BF16)


def _inproj_kv_body(x_ref, g_ref, sh_ref, sc_ref, wk_ref, wv_ref, k_ref, v_ref):
    hb = _norm_mod(x_ref[...], g_ref[...], sh_ref[...], sc_ref[...]).astype(BF16)
    k_ref[...] = jnp.dot(hb, wk_ref[...], preferred_element_type=F32).astype(BF16)
    v_ref[...] = jnp.dot(hb, wv_ref[...], preferred_element_type=F32).astype(BF16)


def _mod_spec(slot, row_of):
    return lambda d: pl.BlockSpec((None, None, 1, d), lambda b, i: (row_of(b), slot, 0, 0))


def _inproj(x, g, mods4, w_bf, cw, aw, qscale, tm):
    bsz, t, d = x.shape
    tok = lambda w, dt: (pl.BlockSpec((None, tm, w), lambda b, i: (b, i, 0)), jax.ShapeDtypeStruct((bsz, t, w), dt))
    outs = [tok(cw, F32), tok(aw, BF16), tok(aw, BF16), tok(aw, BF16)]
    return pl.pallas_call(
        functools.partial(_inproj_body, cw=cw, aw=aw, qscale=qscale),
        grid=(bsz, t // tm),
        in_specs=[pl.BlockSpec((None, tm, d), lambda b, i: (b, i, 0)),
                  pl.BlockSpec((1, d), lambda b, i: (0, 0)),
                  _mod_spec(0, lambda b: b)(d),
                  _mod_spec(1, lambda b: b)(d),
                  pl.BlockSpec(w_bf.shape, lambda b, i: (0, 0))],
        out_specs=[o[0] for o in outs],
        out_shape=[o[1] for o in outs],
        compiler_params=_params(("arbitrary", "arbitrary"), VMEM_LIMIT),
        name="inproj",
    )(x, g, mods4, mods4, w_bf)


def _inproj_ctx(ctx, g, mods4, w_bf, k_lo, aw, ctx_row):
    bsz, l, d = ctx.shape
    assert k_lo % aw == 0
    spec = pl.BlockSpec((None, l, aw), lambda b, i: (b, 0, 0))
    shape = jax.ShapeDtypeStruct((bsz, l, aw), BF16)
    w_cols = lambda j: pl.BlockSpec((d, aw), lambda b, i: (0, k_lo // aw + j))
    return pl.pallas_call(
        _inproj_kv_body,
        grid=(bsz, 1),
        in_specs=[pl.BlockSpec((None, l, d), lambda b, i: (b, 0, 0)),
                  pl.BlockSpec((1, d), lambda b, i: (0, 0)),
                  _mod_spec(0, lambda b: ctx_row)(d),
                  _mod_spec(1, lambda b: ctx_row)(d),
                  w_cols(0), w_cols(1)],
        out_specs=[spec, spec],
        out_shape=[shape, shape],
        compiler_params=_params(("arbitrary", "arbitrary"), VMEM_LIMIT),
        name="inproj_ctx",
    )(ctx, g, mods4, mods4, w_bf, w_bf)


CONV_HALO = 16
CONV_RC = 128
CONV_CW = 128


def _conv_body(prev_ref, cur_ref, next_ref, w_ref, b_ref, lg_ref, lb_ref, o_ref, ext, ysc, *, width, tt):
    i = pl.program_id(1)
    n = pl.num_programs(1)
    ext[0:CONV_HALO, :] = jnp.where(i > 0, prev_ref[...], 0.0)
    ext[CONV_HALO:CONV_HALO + tt, :] = cur_ref[...]
    ext[CONV_HALO + tt:, :] = jnp.where(i < n - 1, next_ref[...], 0.0)
    lead = CONV_HALO - width // 2
    for rc in range(tt // CONV_RC):
        r0 = rc * CONV_RC
        for ch in range(cur_ref.shape[-1] // CONV_CW):
            cs = slice(ch * CONV_CW, (ch + 1) * CONV_CW)
            out = None
            for m in range(SUBLANES):
                nrow = CONV_RC + (SUBLANES if m else 0)
                part = None
                for j in range(width):
                    if (lead + j) % SUBLANES != m:
                        continue
                    a = r0 + lead + j - m
                    term = w_ref[j:j + 1, cs] * ext[a:a + nrow, cs]
                    part = term if part is None else part + term
                piece = part[m:m + CONV_RC]
                out = piece if out is None else out + piece
            ysc[r0:r0 + CONV_RC, cs] = out
    y = ysc[...] + b_ref[...]
    mu = jnp.mean(y, axis=-1, keepdims=True)
    yc = y - mu
    var = jnp.mean(yc * yc, axis=-1, keepdims=True)
    yn = yc * lax.rsqrt(var + EPS) * lg_ref[...] + lb_ref[...]
    o_ref[...] = (yn * jax.nn.sigmoid(yn)).astype(BF16)


def _conv(u, conv_w, conv_b, ln_g, ln_b, tt):
    bsz, t, cw = u.shape
    width = conv_w.shape[0]
    assert width // 2 < CONV_HALO and tt % CONV_HALO == 0 and tt % CONV_RC == 0 and cw % CONV_CW == 0
    hb = tt // CONV_HALO
    nh = t // CONV_HALO
    row = lambda a: a.reshape(1, cw)
    return pl.pallas_call(
        functools.partial(_conv_body, width=width, tt=tt),
        grid=(bsz, t // tt),
        in_specs=[pl.BlockSpec((None, CONV_HALO, cw), lambda b, i: (b, jnp.maximum(i * hb - 1, 0), 0)),
                  pl.BlockSpec((None, tt, cw), lambda b, i: (b, i, 0)),
                  pl.BlockSpec((None, CONV_HALO, cw), lambda b, i: (b, jnp.minimum((i + 1) * hb, nh - 1), 0)),
                  pl.BlockSpec((width, cw), lambda b, i: (0, 0)),
                  pl.BlockSpec((1, cw), lambda b, i: (0, 0)),
                  pl.BlockSpec((1, cw), lambda b, i: (0, 0)),
                  pl.BlockSpec((1, cw), lambda b, i: (0, 0))],
        out_specs=pl.BlockSpec((None, tt, cw), lambda b, i: (b, i, 0)),
        out_shape=jax.ShapeDtypeStruct((bsz, t, cw), BF16),
        scratch_shapes=[pltpu.VMEM((tt + 2 * CONV_HALO, cw), F32), pltpu.VMEM((tt, cw), F32)],
        compiler_params=_params(("arbitrary", "arbitrary")),
        name="conv",
    )(u, u, u, conv_w, row(conv_b), row(ln_g), row(ln_b))


ATT_RB = 8


def _bias_body(rpb_ref, onehot_ref, valid_ref, o_ref):
    t = jnp.dot(rpb_ref[...], onehot_ref[...], preferred_element_type=F32, precision=HIGHEST)
    o_ref[...] = jnp.where(valid_ref[...] > 0.5, t, NEG_INF)


def _bias_table(rpb):
    nh, ndr, ndc = rpb.shape
    qc = np.arange(GRID_W)[:, None]
    kc = np.arange(GRID_W)[None, :]
    c0 = np.clip(qc - NA_COLS // 2, 0, GRID_W - NA_COLS)
    valid = ((kc >= c0) & (kc < c0 + NA_COLS)).astype(np.float32).reshape(1, GRID_W * GRID_W)
    d_col = np.clip(kc - qc + (NA_COLS - 1), 0, 2 * NA_COLS - 2).reshape(1, GRID_W * GRID_W)
    ndc_pad = -(-ndc // SUBLANES) * SUBLANES
    onehot = (np.arange(ndc_pad)[:, None] == d_col).astype(np.float32)
    rpb2 = jnp.pad(rpb.reshape(nh * ndr, ndc), ((0, 0), (0, ndc_pad - ndc)))
    toeplitz = pl.pallas_call(
        _bias_body,
        out_shape=jax.ShapeDtypeStruct((nh * ndr, GRID_W * GRID_W), F32),
        name="bias",
    )(rpb2, onehot, valid).reshape(nh, ndr, GRID_W, GRID_W)
    return jnp.concatenate([toeplitz[:, :-1], toeplitz[:, 1:]], axis=-1)


def _attn_body(q_ref, kp_ref, kc_ref, kn_ref, vp_ref, vc_ref, vn_ref, kx_ref, vx_ref, bias_ref, o_ref,
               kwin, vwin, s_sc, p_sc, m_sc, l_sc, *, rows, kh, n_pairs):
    i = pl.program_id(1)
    blk = ATT_RB * GRID_W
    for w, (a, b) in enumerate(((kp_ref, vp_ref), (kc_ref, vc_ref), (kn_ref, vn_ref))):
        kwin[w * blk:(w + 1) * blk, :] = a[...]
        vwin[w * blk:(w + 1) * blk, :] = b[...]
    lo = lax.broadcasted_iota(I32, (GRID_W, LANES), 1) < LANES // 2
    band = kh * GRID_W

    def lane_groups(x, op):
        parts = [x[:, g * LANES:(g + 1) * LANES] for g in range(x.shape[-1] // LANES)]
        while len(parts) > 1:
            parts = [op(parts[a], parts[a + 1]) for a in range(0, len(parts) - 1, 2)] + parts[len(parts) & ~1:]
        return parts[0]

    def row_body(rr, carry):
        r = i * ATT_RB + rr
        r0 = jnp.clip(r - kh // 2, 0, rows - kh)
        delta = r - r0
        off = pl.multiple_of((r0 - (i - 1) * ATT_RB) * GRID_W, GRID_W)
        qoff = pl.multiple_of(rr * GRID_W, GRID_W)
        def scores(p):
            cols = slice(p * LANES, (p + 1) * LANES)
            qp = q_ref[pl.ds(qoff, GRID_W), cols]
            zeros = jnp.zeros_like(qp)
            q2 = jnp.concatenate([jnp.where(lo, qp, zeros), jnp.where(lo, zeros, qp)], axis=0)
            bias = jnp.concatenate(
                [jnp.concatenate([bias_ref[2 * p + s, 2 * g + (kh - 1) - delta] for g in range(kh // 2)], axis=-1)
                 for s in range(2)], axis=0)
            s_loc = lax.dot_general(q2, kwin[pl.ds(off, band), cols], _NT, preferred_element_type=F32) + bias
            s_ctx = lax.dot_general(q2, kx_ref[:, cols], _NT, preferred_element_type=F32)
            s_sc[p, :, 0:band] = s_loc
            s_sc[p, :, band:] = s_ctx
            m_sc[p] = jnp.maximum(lane_groups(s_loc, jnp.maximum), lane_groups(s_ctx, jnp.maximum))

        def numerators(p):
            m = jnp.max(m_sc[p], axis=-1, keepdims=True)
            e = jnp.exp(s_sc[p] - m)
            p_sc[p] = e.astype(BF16)
            l_sc[p] = lane_groups(e, jnp.add)

        def values(p):
            cols = slice(p * LANES, (p + 1) * LANES)
            o = (jnp.dot(p_sc[p, :, 0:band], vwin[pl.ds(off, band), cols], preferred_element_type=F32)
                 + jnp.dot(p_sc[p, :, band:], vx_ref[:, cols], preferred_element_type=F32))
            o = o / jnp.sum(l_sc[p], axis=-1, keepdims=True)
            o_ref[pl.ds(qoff, GRID_W), cols] = jnp.where(lo, o[0:GRID_W], o[GRID_W:]).astype(BF16)

        for stage in (scores, numerators, values):
            for p in range(n_pairs):
                stage(p)
        return carry

    lax.fori_loop(0, ATT_RB, row_body, 0, unroll=True)


def _attn(q, k, v, kx, vx, bias):
    bsz, t, aw = q.shape
    l = kx.shape[1]
    rows = t // GRID_W
    kh = (bias.shape[1] + 2) // 2
    nb = rows // ATT_RB
    blk = ATT_RB * GRID_W
    assert kh == ATT_RB and rows % ATT_RB == 0 and aw % LANES == 0 and 2 * GRID_W == LANES
    cur = pl.BlockSpec((None, blk, aw), lambda b, i: (b, i, 0))
    prev = pl.BlockSpec((None, blk, aw), lambda b, i: (b, jnp.maximum(i - 1, 0), 0))
    nxt = pl.BlockSpec((None, blk, aw), lambda b, i: (b, jnp.minimum(i + 1, nb - 1), 0))
    cx = pl.BlockSpec((None, l, aw), lambda b, i: (b, 0, 0))
    nh = bias.shape[0]
    nkeys = kh * GRID_W + l
    assert l % LANES == 0 and nh == 2 * (aw // LANES)
    return pl.pallas_call(
        functools.partial(_attn_body, rows=rows, kh=kh, n_pairs=aw // LANES),
        grid=(bsz, nb),
        in_specs=[cur, prev, cur, nxt, prev, cur, nxt, cx, cx,
                  pl.BlockSpec(bias.shape, lambda b, i: (0, 0, 0, 0))],
        out_specs=cur,
        out_shape=jax.ShapeDtypeStruct((bsz, t, aw), BF16),
        scratch_shapes=[pltpu.VMEM((3 * blk, aw), BF16), pltpu.VMEM((3 * blk, aw), BF16),
                        pltpu.VMEM((nh // 2, 2 * GRID_W, nkeys), F32), pltpu.VMEM((nh // 2, 2 * GRID_W, nkeys), BF16),
                        pltpu.VMEM((nh // 2, 2 * GRID_W, LANES), F32), pltpu.VMEM((nh // 2, 2 * GRID_W, LANES), F32)],
        compiler_params=_params(("arbitrary", "arbitrary"), VMEM_LIMIT),
        name="attn",
    )(q, k, k, k, v, v, v, kx, vx, bias)


OUT_GROUPS = 4


def _outproj_body(yc_ref, ya_ref, x_ref, wo_ref, g1_ref, gf_ref, sh_ref, sc_ref, wr_ref,
                  x1_ref, h2_ref, aff_ref, *, cw, tm, ne):
    rg = tm // OUT_GROUPS

    def project(g):
        rows = slice(g * rg, (g + 1) * rg)
        acc = (jnp.dot(yc_ref[rows, :], wo_ref[0:cw, :], preferred_element_type=F32)
               + jnp.dot(ya_ref[rows, :], wo_ref[cw:, :], preferred_element_type=F32))
        x1 = x_ref[rows, :] + g1_ref[...] * acc
        x1_ref[rows, :] = x1
        return x1

    def route(g, x1):
        h2 = _norm_mod(x1, gf_ref[...], sh_ref[...], sc_ref[...])
        for s in range(h2.shape[-1] // LANES):
            _tile_rows(h2_ref, g * rg, rg, s)[...] = h2[:, s * LANES:(s + 1) * LANES]
        hi = h2.astype(BF16)
        lo = (h2 - hi.astype(F32)).astype(BF16)
        part_hi = jnp.dot(hi, wr_ref[...], preferred_element_type=F32)
        part_lo = jnp.dot(lo, wr_ref[...], preferred_element_type=F32)
        logits = part_hi + pltpu.roll(part_hi, LANES - ne, axis=1) + part_lo
        logits = jnp.where(lax.broadcasted_iota(I32, logits.shape, 1) < ne, logits, NEG_INF)
        e = jnp.exp(logits - jnp.max(logits, axis=-1, keepdims=True))
        aff = (e / jnp.sum(e, axis=-1, keepdims=True)).T[0:ne, :]
        aff_ref[:, g * rg:(g + 1) * rg] = aff

    x1s = [project(g) for g in range(OUT_GROUPS)]
    for g in range(OUT_GROUPS):
        route(g, x1s[g])


def _outproj(y_conv, y_attn, x, wo_bf, mods4, g_ffn, w_router, tm):
    bsz, t, d = x.shape
    cw = y_conv.shape[-1]
    ne = w_router.shape[1]
    assert d == SUBLANES * LANES and 2 * ne <= LANES
    nt = t // tm
    w_hi = w_router.astype(BF16)
    w_lo = (w_router - w_hi.astype(F32)).astype(BF16)
    wr_t = jnp.zeros((d, LANES), BF16).at[:, 0:ne].set(w_hi).at[:, ne:2 * ne].set(w_lo)
    return pl.pallas_call(
        functools.partial(_outproj_body, cw=cw, tm=tm, ne=ne),
        grid=(bsz, nt),
        in_specs=[pl.BlockSpec((None, tm, cw), lambda b, i: (b, i, 0)),
                  pl.BlockSpec((None, tm, y_attn.shape[-1]), lambda b, i: (b, i, 0)),
                  pl.BlockSpec((None, tm, d), lambda b, i: (b, i, 0)),
                  pl.BlockSpec(wo_bf.shape, lambda b, i: (0, 0)),
                  _mod_spec(2, lambda b: b)(d),
                  pl.BlockSpec((1, d), lambda b, i: (0, 0)),
                  _mod_spec(3, lambda b: b)(d),
                  _mod_spec(4, lambda b: b)(d),
                  pl.BlockSpec(wr_t.shape, lambda b, i: (0, 0))],
        out_specs=[pl.BlockSpec((None, tm, d), lambda b, i: (b, i, 0)),
                   pl.BlockSpec((tm * SUBLANES, LANES), lambda b, i: (b * nt + i, 0)),
                   pl.BlockSpec((None, ne, tm), lambda b, i: (b, 0, i))],
        out_shape=[jax.ShapeDtypeStruct((bsz, t, d), F32),
                   jax.ShapeDtypeStruct((bsz * t * SUBLANES, LANES), F32),
                   jax.ShapeDtypeStruct((bsz, ne, t), F32)],
        compiler_params=_params(("arbitrary", "arbitrary"), VMEM_LIMIT),
        name="outproj",
    )(y_conv, y_attn, x, wo_bf, mods4, g_ffn, mods4, mods4, wr_t)


def _route_body(affe_ref, affr_ref, u_ref, ones_ref, l_ref, tok_ref, *, cap, ne, nblk):
    aff_e = affe_ref[...]

    def search(it, bits):
        cand = bits | jnp.left_shift(jnp.int32(1), 30 - it)
        cnt = jnp.sum(jnp.where(aff_e >= pltpu.bitcast(cand, F32), 1.0, 0.0), axis=1, keepdims=True)
        return jnp.where(cnt >= cap, cand, bits)

    tau = pltpu.bitcast(lax.fori_loop(0, 31, search, jnp.zeros((ne, 1), I32)), F32)
    n_gt = jnp.sum(jnp.where(aff_e > tau, 1.0, 0.0), axis=1, keepdims=True)
    need = cap - n_gt

    def per_row(col):
        return jnp.concatenate([jnp.broadcast_to(col[e:e + 1, :], (nblk, 1)) for e in range(ne)], axis=0)

    tau_r = per_row(tau)
    need_r = per_row(need)
    kr = affr_ref[...]

    def scan(mf):
        mb = mf.astype(BF16)
        incl = jnp.dot(mb, u_ref[...], preferred_element_type=F32)
        tot = jnp.dot(mb, ones_ref[...], preferred_element_type=F32)
        off = jnp.dot(l_ref[...], tot.astype(BF16), preferred_element_type=F32)
        return incl, tot, off

    eq = jnp.where(kr == tau_r, 1.0, 0.0)
    incl_e, _, off_e = scan(eq)
    rank_eq = off_e + incl_e - eq
    sel = jnp.where((kr > tau_r) | ((eq > 0.5) & (rank_eq < need_r)), 1.0, 0.0)
    incl, _, off = scan(sel)

    nbits = LANES.bit_length() - 1
    lane = lax.broadcasted_iota(I32, kr.shape, 1)
    row = lax.broadcasted_iota(I32, kr.shape, 0)
    dist = lane - (incl.astype(I32) - 1)
    word = jnp.where(sel > 0.5, (1 << (2 * nbits)) | (dist << nbits) | lane, 0)

    def bit(x, pos):
        return (x >> pos) & 1

    for k in range(nbits):
        arriving = pltpu.roll(word, LANES - (1 << k), axis=1)
        take = bit(arriving, 2 * nbits) & bit(arriving, nbits + k)
        stay = bit(word, 2 * nbits) & (1 - bit(word, nbits + k))
        word = jnp.where(take == 1, arriving, jnp.where(stay == 1, word, 0))

    off_i = off.astype(I32)
    shift = off_i & (LANES - 1)
    for k in range(nbits):
        word = jnp.where(bit(shift, k) == 1, pltpu.roll(word, 1 << k, axis=1), word)
    valid = bit(word, 2 * nbits) == 1
    wrapped = lane < shift
    blk_bits = nblk.bit_length() - 1
    q0 = (row >> blk_bits) * (cap // LANES) + (off_i >> nbits)
    digits = (jnp.where(valid, row & (nblk - 1), 0), jnp.where(valid, word & (LANES - 1), 0))

    def place(sel_rows, q):
        onehot_t = jnp.where(q == lane, 1.0, 0.0).astype(BF16)
        outs = []
        for dgt in digits:
            x = jnp.where(sel_rows, dgt, 0).astype(F32).astype(BF16)
            outs.append(lax.dot_general(onehot_t, x, (((0,), (0,)), ((), ())), preferred_element_type=F32))
        return outs[0] * LANES + outs[1]

    tok_ref[...] = (place(jnp.logical_not(wrapped), q0) + place(wrapped, q0 + 1)).astype(I32)


def _route(aff, cap):
    bsz, ne, t = aff.shape
    nblk = t // LANES
    nr = ne * nblk
    nq = ne * cap // LANES
    assert nq == LANES and cap % LANES == 0 and nblk & (nblk - 1) == 0 and nblk <= 256
    aff_r = aff.reshape(bsz, nr, LANES)
    ar = np.arange(LANES)
    upper = jnp.asarray(ar[:, None] <= ar[None, :], BF16)
    ones = jnp.ones((LANES, LANES), BF16)
    rr = np.arange(nr)
    lower = jnp.asarray((rr[:, None] // nblk == rr[None, :] // nblk) & (rr[None, :] < rr[:, None]), BF16)
    const = lambda a: pl.BlockSpec(a.shape, lambda b: (0,) * a.ndim)
    return pl.pallas_call(
        functools.partial(_route_body, cap=cap, ne=ne, nblk=nblk),
        grid=(bsz,),
        in_specs=[pl.BlockSpec((None, ne, t), lambda b: (b, 0, 0)),
                  pl.BlockSpec((None, nr, LANES), lambda b: (b, 0, 0)),
                  const(upper), const(ones), const(lower)],
        out_specs=pl.BlockSpec((None, nq, LANES), lambda b: (b, 0, 0)),
        out_shape=jax.ShapeDtypeStruct((bsz, nq, LANES), I32),
        compiler_params=_params(("arbitrary",), VMEM_LIMIT),
        name="route",
    )(aff, aff_r, upper, ones, lower)


MOE_HC = 512
MOE_TM = 512
ROW_UNROLL = 8


def _gather_copy(h2_hbm, xs3, sem, token, slot):
    src = h2_hbm.at[pl.ds(pl.multiple_of(token * SUBLANES, SUBLANES), SUBLANES), :]
    dst = xs3.at[pl.ds(pl.multiple_of(slot * SUBLANES, SUBLANES), SUBLANES), :]
    return pltpu.make_async_copy(src, dst, sem)


def _moe_body(tok_ref, nxt_ref, h2_hbm, wg_ref, wu_ref, wd_ref, y_ref, xs3, xsb, wgb, wub, wdb, sem,
              *, bsz, t, cap, ne, nch):
    e = pl.program_id(0)
    c = pl.program_id(1)
    n_rows = bsz * cap
    n_tiles = n_rows // MOE_TM
    per_tile = n_rows // (nch * n_tiles)
    n_lane_groups = xsb.shape[-1] // LANES

    def fetch(ref, b, slot, priority=0):
        _gather_copy(h2_hbm, xs3, sem, b * t + ref[b, 0, slot], b * cap + slot).start(priority=priority)

    def wait_staging():
        pltpu.make_async_copy(h2_hbm.at[pl.ds(0, n_rows * SUBLANES), :], xs3, sem).wait()

    @pl.when((e == 0) & (c == 0))
    def _first_expert():
        for b in range(bsz):
            def rows(g, carry):
                for k in range(ROW_UNROLL):
                    fetch(tok_ref, b, g * ROW_UNROLL + k)
                return carry

            lax.fori_loop(0, cap // ROW_UNROLL, rows, 0)

    def tiles(first_chunk):
        wgb[...] = wg_ref[...].astype(BF16)
        wub[...] = wu_ref[...].astype(BF16)
        wdb[...] = wd_ref[...].astype(BF16)
        for m in range(n_tiles):
            r0 = m * MOE_TM
            if first_chunk:
                xm = jnp.concatenate([_tile_rows(xs3, r0, MOE_TM, s)[...].astype(BF16)
                                      for s in range(n_lane_groups)], axis=-1)
                xsb[r0:r0 + MOE_TM, :] = xm
            else:
                xm = xsb[r0:r0 + MOE_TM, :]
            gate = jnp.dot(xm, wgb[...], preferred_element_type=F32)
            first = (c * n_tiles + m) * per_tile
            b = first // cap
            for k in range(per_tile):
                fetch(nxt_ref, b, first - b * cap + k, priority=1)
            up = jnp.dot(xm, wub[...], preferred_element_type=F32)
            hid = (gate * jax.nn.sigmoid(gate) * up).astype(BF16)
            contrib = jnp.dot(hid, wdb[...], preferred_element_type=F32)
            if first_chunk:
                y_ref[r0:r0 + MOE_TM, :] = contrib
            else:
                y_ref[r0:r0 + MOE_TM, :] += contrib

    @pl.when(c == 0)
    def _():
        wait_staging()
        tiles(True)

    @pl.when(c > 0)
    def _():
        tiles(False)

    @pl.when((e == ne - 1) & (c == nch - 1))
    def _drain():
        wait_staging()


def _moe(tok, h2_tiles, w_gate, w_up, w_down, bsz, t, cap):
    ne, d, dh = w_gate.shape
    nch = dh // MOE_HC
    n_rows = bsz * cap
    per_tile = n_rows // (nch * (n_rows // MOE_TM))
    assert cap % ROW_UNROLL == 0 and n_rows % MOE_TM == 0 and cap % per_tile == 0 and n_rows // nch <= MOE_TM
    assert d == SUBLANES * LANES
    tok_spec = lambda ahead: pl.BlockSpec((bsz, None, 1, cap),
                                          lambda e, c: (0, jnp.minimum(e + ahead, ne - 1), 0, 0),
                                          memory_space=pltpu.SMEM)
    return pl.pallas_call(
        functools.partial(_moe_body, bsz=bsz, t=t, cap=cap, ne=ne, nch=nch),
        grid=(ne, nch),
        in_specs=[tok_spec(0), tok_spec(1),
                  pl.BlockSpec(memory_space=pl.ANY),
                  pl.BlockSpec((None, d, MOE_HC), lambda e, c: (e, 0, c)),
                  pl.BlockSpec((None, d, MOE_HC), lambda e, c: (e, 0, c)),
                  pl.BlockSpec((None, MOE_HC, d), lambda e, c: (e, c, 0))],
        out_specs=pl.BlockSpec((None, n_rows, d), lambda e, c: (e, 0, 0)),
        out_shape=jax.ShapeDtypeStruct((ne, n_rows, d), F32),
        scratch_shapes=[pltpu.VMEM((n_rows * SUBLANES, LANES), F32),
                        pltpu.VMEM((n_rows, d), BF16),
                        pltpu.VMEM((d, MOE_HC), BF16),
                        pltpu.VMEM((d, MOE_HC), BF16),
                        pltpu.VMEM((MOE_HC, d), BF16),
                        pltpu.SemaphoreType.DMA],
        compiler_params=_params(("arbitrary", "arbitrary"), VMEM_LIMIT),
        name="moe",
    )(tok, tok, h2_tiles, w_gate, w_up, w_down)


COMBINE_CHUNK = 128


def _combine_body(tok_ref, aff_ref, y_ref, x1_ref, g2_ref, gf_ref, o_ref, acc, ysc, *, cap, ne, tm):
    s = pl.program_id(1)

    @pl.when(s == 0)
    def _():
        acc[...] = jnp.zeros_like(acc)

    @pl.when(s < ne)
    def _scatter():
        def tile_of(row):
            start = row * SUBLANES
            return pl.ds(start if isinstance(start, int) else pl.multiple_of(start, SUBLANES), SUBLANES)

        for r0 in range(0, cap, COMBINE_CHUNK):
            for g in range(SUBLANES):
                _tile_rows(ysc, r0, COMBINE_CHUNK, g)[...] = y_ref[r0:r0 + COMBINE_CHUNK, g * LANES:(g + 1) * LANES]
            for base in range(r0, r0 + COMBINE_CHUNK, ROW_UNROLL):
                tokens = [tok_ref[0, base + k] for k in range(ROW_UNROLL)]
                new = [acc[tile_of(tk), :] + aff_ref[0, tk] * ysc[tile_of(base + k), :]
                       for k, tk in enumerate(tokens)]
                for tk, val in zip(tokens, new):
                    acc[tile_of(tk), :] = val

    @pl.when(s >= ne)
    def _finish():
        first = pl.multiple_of((s - ne) * tm, tm)
        moe = jnp.concatenate([_tile_rows(acc, first, tm, g)[...] for g in range(SUBLANES)], axis=-1)
        x = x1_ref[...] + g2_ref[...] * moe
        ms = jnp.mean(x * x, axis=-1, keepdims=True)
        o_ref[...] = x * lax.rsqrt(ms + EPS) * gf_ref[...]


def _combine(tok, aff, y, x1, mods4, g_final, cap, tm):
    bsz, t, d = x1.shape
    ne = aff.shape[1]
    nt = t // tm
    assert d == SUBLANES * LANES
    expert = lambda s: jnp.minimum(s, ne - 1)
    tile = lambda s: jnp.maximum(s - ne, 0)
    smem = lambda n: pl.BlockSpec((None, None, 1, n), lambda b, s: (b, expert(s), 0, 0), memory_space=pltpu.SMEM)
    token_tile = pl.BlockSpec((None, tm, d), lambda b, s: (b, tile(s), 0))
    return pl.pallas_call(
        functools.partial(_combine_body, cap=cap, ne=ne, tm=tm),
        grid=(bsz, ne + nt),
        in_specs=[smem(cap), smem(t),
                  pl.BlockSpec((None, cap, d), lambda b, s: (expert(s), b, 0)),
                  token_tile,
                  _mod_spec(5, lambda b: b)(d),
                  pl.BlockSpec((1, d), lambda b, s: (0, 0))],
        out_specs=token_tile,
        out_shape=jax.ShapeDtypeStruct((bsz, t, d), F32),
        scratch_shapes=[pltpu.VMEM((t * SUBLANES, LANES), F32),
                        pltpu.VMEM((cap * SUBLANES, LANES), F32)],
        compiler_params=_params(("arbitrary", "arbitrary"), VMEM_LIMIT),
        name="combine",
    )(tok, aff, y, x1, mods4, g_final)


def _layer(x, ctx, mods4, ctx_row, norm_mix_g, w_in, conv_w, conv_b, conv_ln_g, conv_ln_b, rpb, w_out,
           norm_ffn_g, w_router, w_gate, w_up, w_down, norm_final_g):
    bsz, t, d = x.shape
    cw = conv_w.shape[1]
    n_heads = rpb.shape[0]
    aw = (w_in.shape[1] - 2 * cw) // 3
    head_dim = aw // n_heads
    rows = t // GRID_W
    ne = w_router.shape[1]
    cap = EC_FACTOR * t // ne
    row = lambda a: a.reshape(1, -1)

    w_in_bf = w_in.astype(BF16)
    u, q, k, v = _inproj(x, row(norm_mix_g), mods4, w_in_bf, cw, aw, head_dim ** -0.5, tm=1024)
    kx, vx = _inproj_ctx(ctx, row(norm_mix_g), mods4, w_in_bf, 2 * cw + aw, aw, ctx_row)
    y_conv = _conv(u, conv_w, conv_b, conv_ln_g, conv_ln_b, tt=1024)
    assert rows >= (rpb.shape[1] + 1) // 2
    y_attn = _attn(q, k, v, kx, vx, _bias_table(rpb))
    x1, h2_tiles, aff = _outproj(y_conv, y_attn, x, w_out.astype(BF16), mods4, row(norm_ffn_g),
                                 w_router, tm=1024)
    tok = _route(aff, cap).reshape(bsz, ne, 1, cap)
    y = _moe(tok, h2_tiles, w_gate, w_up, w_down, bsz, t, cap)
    return _combine(tok, aff.reshape(bsz, ne, 1, t), y, x1, mods4, row(norm_final_g), cap, tm=512)


def kernel(x, c, ctx, c_ctx, w_mod, b_mod, norm_mix_g, w_in, conv_w, conv_b, conv_ln_g, conv_ln_b, rpb,
           w_out, norm_ffn_g, w_router, w_gate, w_up, w_down, norm_final_g):
    bsz, t, d = x.shape
    depth = w_mod.shape[0]
    assert depth == 1
    assert bsz + 1 <= SUBLANES
    c8 = jnp.zeros((SUBLANES, d), F32).at[:bsz].set(c).at[bsz].set(c_ctx)
    mods4 = _adaln(c8, w_mod[0], b_mod[0].reshape(1, -1)).reshape(SUBLANES, N_MOD, 1, d)
    return _layer(x, ctx, mods4, bsz, norm_mix_g[0], w_in[0], conv_w[0], conv_b[0], conv_ln_g[0],
                  conv_ln_b[0], rpb[0], w_out[0], norm_ffn_g[0], w_router[0], w_gate[0], w_up[0],
                  w_down[0], norm_final_g)
```

```python
import functools

import jax
import jax.numpy as jnp
import numpy as np
from jax import lax
from jax.experimental import pallas as pl
from jax.experimental.pallas import tpu as pltpu

F32 = jnp.float32
BF16 = jnp.bfloat16
I32 = jnp.int32
HIGHEST = lax.Precision.HIGHEST

GRID_W = 64
NA_COLS = 16
EC_FACTOR = 2
N_MOD = 6
EPS = 1e-6
NEG_INF = -1e30
LANES = 128
SUBLANES = 8
VMEM_LIMIT = 56 * 1024 * 1024

_NT = (((1,), (1,)), ((), ()))


def _params(sem, vmem=None):
    return pltpu.CompilerParams(dimension_semantics=sem, vmem_limit_bytes=vmem)


def _tile_rows(ref, first, n, s):
    return ref.at[pl.ds(first * SUBLANES + s, n, stride=SUBLANES), :]


def _adaln_body(c_ref, w_ref, b_ref, o_ref):
    c = c_ref[...]
    s = c * jax.nn.sigmoid(c)
    o_ref[...] = jnp.dot(s, w_ref[...], preferred_element_type=F32, precision=HIGHEST) + b_ref[...]


def _adaln(c8, w_mod, b_mod):
    d, n = w_mod.shape
    bn = 512
    return pl.pallas_call(
        _adaln_body,
        grid=(n // bn,),
        in_specs=[pl.BlockSpec((SUBLANES, d), lambda j: (0, 0)),
                  pl.BlockSpec((d, bn), lambda j: (0, j)),
                  pl.BlockSpec((1, bn), lambda j: (0, j))],
        out_specs=pl.BlockSpec((SUBLANES, bn), lambda j: (0, j)),
        out_shape=jax.ShapeDtypeStruct((SUBLANES, n), F32),
        compiler_params=_params(("arbitrary",)),
        name="adaln",
    )(c8, w_mod, b_mod)


def _norm_mod(x, g, shift, scale):
    ms = jnp.mean(x * x, axis=-1, keepdims=True)
    h = x * lax.rsqrt(ms + EPS) * g
    return h * (1.0 + scale) + shift


def _inproj_body(x_ref, g_ref, sh_ref, sc_ref, w_ref, u_ref, q_ref, k_ref, v_ref, *, cw, aw, qscale):
    hb = _norm_mod(x_ref[...], g_ref[...], sh_ref[...], sc_ref[...]).astype(BF16)

    def proj(lo, width):
        return jnp.dot(hb, w_ref[:, lo:lo + width], preferred_element_type=F32)

    a = proj(0, cw)
    gate = proj(cw, cw)
    u_ref[...] = a * jax.nn.sigmoid(gate)
    q_ref[...] = (proj(2 * cw, aw) * qscale).astype(BF16)
    k_ref[...] = proj(2 * cw + aw, aw).astype(BF16)
    v_ref[...] = proj(2 * cw + 2 * aw, aw).astype(BF16)


def _inproj_kv_body(x_ref, g_ref, sh_ref, sc_ref, wk_ref, wv_ref, k_ref, v_ref):
    hb = _norm_mod(x_ref[...], g_ref[...], sh_ref[...], sc_ref[...]).astype(BF16)
    k_ref[...] = jnp.dot(hb, wk_ref[...], preferred_element_type=F32).astype(BF16)
    v_ref[...] = jnp.dot(hb, wv_ref[...], preferred_element_type=F32).astype(BF16)


def _mod_spec(slot, row_of):
    return lambda d: pl.BlockSpec((None, None, 1, d), lambda b, i: (row_of(b), slot, 0, 0))


def _inproj(x, g, mods4, w_bf, cw, aw, qscale, tm):
    bsz, t, d = x.shape
    tok = lambda w, dt: (pl.BlockSpec((None, tm, w), lambda b, i: (b, i, 0)), jax.ShapeDtypeStruct((bsz, t, w), dt))
    outs = [tok(cw, F32), tok(aw, BF16), tok(aw, BF16), tok(aw, BF16)]
    return pl.pallas_call(
        functools.partial(_inproj_body, cw=cw, aw=aw, qscale=qscale),
        grid=(bsz, t // tm),
        in_specs=[pl.BlockSpec((None, tm, d), lambda b, i: (b, i, 0)),
                  pl.BlockSpec((1, d), lambda b, i: (0, 0)),
                  _mod_spec(0, lambda b: b)(d),
                  _mod_spec(1, lambda b: b)(d),
                  pl.BlockSpec(w_bf.shape, lambda b, i: (0, 0))],
        out_specs=[o[0] for o in outs],
        out_shape=[o[1] for o in outs],
        compiler_params=_params(("arbitrary", "arbitrary"), VMEM_LIMIT),
        name="inproj",
    )(x, g, mods4, mods4, w_bf)


def _inproj_ctx(ctx, g, mods4, w_bf, k_lo, aw, ctx_row):
    bsz, l, d = ctx.shape
    assert k_lo % aw == 0
    spec = pl.BlockSpec((None, l, aw), lambda b, i: (b, 0, 0))
    shape = jax.ShapeDtypeStruct((bsz, l, aw), BF16)
    w_cols = lambda j: pl.BlockSpec((d, aw), lambda b, i: (0, k_lo // aw + j))
    return pl.pallas_call(
        _inproj_kv_body,
        grid=(bsz, 1),
        in_specs=[pl.BlockSpec((None, l, d), lambda b, i: (b, 0, 0)),
                  pl.BlockSpec((1, d), lambda b, i: (0, 0)),
                  _mod_spec(0, lambda b: ctx_row)(d),
                  _mod_spec(1, lambda b: ctx_row)(d),
                  w_cols(0), w_cols(1)],
        out_specs=[spec, spec],
        out_shape=[shape, shape],
        compiler_params=_params(("arbitrary", "arbitrary"), VMEM_LIMIT),
        name="inproj_ctx",
    )(ctx, g, mods4, mods4, w_bf, w_bf)


CONV_HALO = 16
CONV_RC = 128
CONV_CW = 128


def _conv_body(prev_ref, cur_ref, next_ref, w_ref, b_ref, lg_ref, lb_ref, o_ref, ext, ysc, *, width, tt):
    i = pl.program_id(1)
    n = pl.num_programs(1)
    ext[0:CONV_HALO, :] = jnp.where(i > 0, prev_ref[...], 0.0)
    ext[CONV_HALO:CONV_HALO + tt, :] = cur_ref[...]
    ext[CONV_HALO + tt:, :] = jnp.where(i < n - 1, next_ref[...], 0.0)
    lead = CONV_HALO - width // 2
    for rc in range(tt // CONV_RC):
        r0 = rc * CONV_RC
        for ch in range(cur_ref.shape[-1] // CONV_CW):
            cs = slice(ch * CONV_CW, (ch + 1) * CONV_CW)
            out = None
            for m in range(SUBLANES):
                nrow = CONV_RC + (SUBLANES if m else 0)
                part = None
                for j in range(width):
                    if (lead + j) % SUBLANES != m:
                        continue
                    a = r0 + lead + j - m
                    term = w_ref[j:j + 1, cs] * ext[a:a + nrow, cs]
                    part = term if part is None else part + term
                piece = part[m:m + CONV_RC]
                out = piece if out is None else out + piece
            ysc[r0:r0 + CONV_RC, cs] = out
    y = ysc[...] + b_ref[...]
    mu = jnp.mean(y, axis=-1, keepdims=True)
    yc = y - mu
    var = jnp.mean(yc * yc, axis=-1, keepdims=True)
    yn = yc * lax.rsqrt(var + EPS) * lg_ref[...] + lb_ref[...]
    o_ref[...] = (yn * jax.nn.sigmoid(yn)).astype(BF16)


def _conv(u, conv_w, conv_b, ln_g, ln_b, tt):
    bsz, t, cw = u.shape
    width = conv_w.shape[0]
    assert width // 2 < CONV_HALO and tt % CONV_HALO == 0 and tt % CONV_RC == 0 and cw % CONV_CW == 0
    hb = tt // CONV_HALO
    nh = t // CONV_HALO
    row = lambda a: a.reshape(1, cw)
    return pl.pallas_call(
        functools.partial(_conv_body, width=width, tt=tt),
        grid=(bsz, t // tt),
        in_specs=[pl.BlockSpec((None, CONV_HALO, cw), lambda b, i: (b, jnp.maximum(i * hb - 1, 0), 0)),
                  pl.BlockSpec((None, tt, cw), lambda b, i: (b, i, 0)),
                  pl.BlockSpec((None, CONV_HALO, cw), lambda b, i: (b, jnp.minimum((i + 1) * hb, nh - 1), 0)),
                  pl.BlockSpec((width, cw), lambda b, i: (0, 0)),
                  pl.BlockSpec((1, cw), lambda b, i: (0, 0)),
                  pl.BlockSpec((1, cw), lambda b, i: (0, 0)),
                  pl.BlockSpec((1, cw), lambda b, i: (0, 0))],
        out_specs=pl.BlockSpec((None, tt, cw), lambda b, i: (b, i, 0)),
        out_shape=jax.ShapeDtypeStruct((bsz, t, cw), BF16),
        scratch_shapes=[pltpu.VMEM((tt + 2 * CONV_HALO, cw), F32), pltpu.VMEM((tt, cw), F32)],
        compiler_params=_params(("arbitrary", "arbitrary")),
        name="conv",
    )(u, u, u, conv_w, row(conv_b), row(ln_g), row(ln_b))


ATT_RB = 8


def _bias_body(rpb_ref, onehot_ref, valid_ref, o_ref):
    t = jnp.dot(rpb_ref[...], onehot_ref[...], preferred_element_type=F32, precision=HIGHEST)
    o_ref[...] = jnp.where(valid_ref[...] > 0.5, t, NEG_INF)


def _bias_table(rpb):
    nh, ndr, ndc = rpb.shape
    qc = np.arange(GRID_W)[:, None]
    kc = np.arange(GRID_W)[None, :]
    c0 = np.clip(qc - NA_COLS // 2, 0, GRID_W - NA_COLS)
    valid = ((kc >= c0) & (kc < c0 + NA_COLS)).astype(np.float32).reshape(1, GRID_W * GRID_W)
    d_col = np.clip(kc - qc + (NA_COLS - 1), 0, 2 * NA_COLS - 2).reshape(1, GRID_W * GRID_W)
    ndc_pad = -(-ndc // SUBLANES) * SUBLANES
    onehot = (np.arange(ndc_pad)[:, None] == d_col).astype(np.float32)
    rpb2 = jnp.pad(rpb.reshape(nh * ndr, ndc), ((0, 0), (0, ndc_pad - ndc)))
    toeplitz = pl.pallas_call(
        _bias_body,
        out_shape=jax.ShapeDtypeStruct((nh * ndr, GRID_W * GRID_W), F32),
        name="bias",
    )(rpb2, onehot, valid).reshape(nh, ndr, GRID_W, GRID_W)
    return jnp.concatenate([toeplitz[:, :-1], toeplitz[:, 1:]], axis=-1)


def _attn_body(q_ref, kp_ref, kc_ref, kn_ref, vp_ref, vc_ref, vn_ref, kx_ref, vx_ref, bias_ref, o_ref,
               kwin, vwin, s_sc, p_sc, m_sc, l_sc, *, rows, kh, n_pairs):
    i = pl.program_id(1)
    blk = ATT_RB * GRID_W
    for w, (a, b) in enumerate(((kp_ref, vp_ref), (kc_ref, vc_ref), (kn_ref, vn_ref))):
        kwin[w * blk:(w + 1) * blk, :] = a[...]
        vwin[w * blk:(w + 1) * blk, :] = b[...]
    lo = lax.broadcasted_iota(I32, (GRID_W, LANES), 1) < LANES // 2
    band = kh * GRID_W

    def lane_groups(x, op):
        parts = [x[:, g * LANES:(g + 1) * LANES] for g in range(x.shape[-1] // LANES)]
        while len(parts) > 1:
            parts = [op(parts[a], parts[a + 1]) for a in range(0, len(parts) - 1, 2)] + parts[len(parts) & ~1:]
        return parts[0]

    def row_body(rr, carry):
        r = i * ATT_RB + rr
        r0 = jnp.clip(r - kh // 2, 0, rows - kh)
        delta = r - r0
        off = pl.multiple_of((r0 - (i - 1) * ATT_RB) * GRID_W, GRID_W)
        qoff = pl.multiple_of(rr * GRID_W, GRID_W)
        def scores(p):
            cols = slice(p * LANES, (p + 1) * LANES)
            qp = q_ref[pl.ds(qoff, GRID_W), cols]
            zeros = jnp.zeros_like(qp)
            q2 = jnp.concatenate([jnp.where(lo, qp, zeros), jnp.where(lo, zeros, qp)], axis=0)
            bias = jnp.concatenate(
                [jnp.concatenate([bias_ref[2 * p + s, 2 * g + (kh - 1) - delta] for g in range(kh // 2)], axis=-1)
                 for s in range(2)], axis=0)
            s_loc = lax.dot_general(q2, kwin[pl.ds(off, band), cols], _NT, preferred_element_type=F32) + bias
            s_ctx = lax.dot_general(q2, kx_ref[:, cols], _NT, preferred_element_type=F32)
            s_sc[p, :, 0:band] = s_loc
            s_sc[p, :, band:] = s_ctx
            m_sc[p] = jnp.maximum(lane_groups(s_loc, jnp.maximum), lane_groups(s_ctx, jnp.maximum))

        def numerators(p):
            m = jnp.max(m_sc[p], axis=-1, keepdims=True)
            e = jnp.exp(s_sc[p] - m)
            p_sc[p] = e.astype(BF16)
            l_sc[p] = lane_groups(e, jnp.add)

        def values(p):
            cols = slice(p * LANES, (p + 1) * LANES)
            o = (jnp.dot(p_sc[p, :, 0:band], vwin[pl.ds(off, band), cols], preferred_element_type=F32)
                 + jnp.dot(p_sc[p, :, band:], vx_ref[:, cols], preferred_element_type=F32))
            o = o / jnp.sum(l_sc[p], axis=-1, keepdims=True)
            o_ref[pl.ds(qoff, GRID_W), cols] = jnp.where(lo, o[0:GRID_W], o[GRID_W:]).astype(BF16)

        for stage in (scores, numerators, values):
            for p in range(n_pairs):
                stage(p)
        return carry

    lax.fori_loop(0, ATT_RB, row_body, 0, unroll=True)


def _attn(q, k, v, kx, vx, bias):
    bsz, t, aw = q.shape
    l = kx.shape[1]
    rows = t // GRID_W
    kh = (bias.shape[1] + 2) // 2
    nb = rows // ATT_RB
    blk = ATT_RB * GRID_W
    assert kh == ATT_RB and rows % ATT_RB == 0 and aw % LANES == 0 and 2 * GRID_W == LANES
    cur = pl.BlockSpec((None, blk, aw), lambda b, i: (b, i, 0))
    prev = pl.BlockSpec((None, blk, aw), lambda b, i: (b, jnp.maximum(i - 1, 0), 0))
    nxt = pl.BlockSpec((None, blk, aw), lambda b, i: (b, jnp.minimum(i + 1, nb - 1), 0))
    cx = pl.BlockSpec((None, l, aw), lambda b, i: (b, 0, 0))
    nh = bias.shape[0]
    nkeys = kh * GRID_W + l
    assert l % LANES == 0 and nh == 2 * (aw // LANES)
    return pl.pallas_call(
        functools.partial(_attn_body, rows=rows, kh=kh, n_pairs=aw // LANES),
        grid=(bsz, nb),
        in_specs=[cur, prev, cur, nxt, prev, cur, nxt, cx, cx,
                  pl.BlockSpec(bias.shape, lambda b, i: (0, 0, 0, 0))],
        out_specs=cur,
        out_shape=jax.ShapeDtypeStruct((bsz, t, aw), BF16),
        scratch_shapes=[pltpu.VMEM((3 * blk, aw), BF16), pltpu.VMEM((3 * blk, aw), BF16),
                        pltpu.VMEM((nh // 2, 2 * GRID_W, nkeys), F32), pltpu.VMEM((nh // 2, 2 * GRID_W, nkeys), BF16),
                        pltpu.VMEM((nh // 2, 2 * GRID_W, LANES), F32), pltpu.VMEM((nh // 2, 2 * GRID_W, LANES), F32)],
        compiler_params=_params(("arbitrary", "arbitrary"), VMEM_LIMIT),
        name="attn",
    )(q, k, k, k, v, v, v, kx, vx, bias)


OUT_GROUPS = 4


def _outproj_body(yc_ref, ya_ref, x_ref, wo_ref, g1_ref, gf_ref, sh_ref, sc_ref, wr_ref,
                  x1_ref, h2_ref, aff_ref, *, cw, tm, ne):
    rg = tm // OUT_GROUPS

    def project(g):
        rows = slice(g * rg, (g + 1) * rg)
        acc = (jnp.dot(yc_ref[rows, :], wo_ref[0:cw, :], preferred_element_type=F32)
               + jnp.dot(ya_ref[rows, :], wo_ref[cw:, :], preferred_element_type=F32))
        x1 = x_ref[rows, :] + g1_ref[...] * acc
        x1_ref[rows, :] = x1
        return x1

    def route(g, x1):
        h2 = _norm_mod(x1, gf_ref[...], sh_ref[...], sc_ref[...])
        for s in range(h2.shape[-1] // LANES):
            _tile_rows(h2_ref, g * rg, rg, s)[...] = h2[:, s * LANES:(s + 1) * LANES]
        hi = h2.astype(BF16)
        lo = (h2 - hi.astype(F32)).astype(BF16)
        part_hi = jnp.dot(hi, wr_ref[...], preferred_element_type=F32)
        part_lo = jnp.dot(lo, wr_ref[...], preferred_element_type=F32)
        logits = part_hi + pltpu.roll(part_hi, LANES - ne, axis=1) + part_lo
        logits = jnp.where(lax.broadcasted_iota(I32, logits.shape, 1) < ne, logits, NEG_INF)
        e = jnp.exp(logits - jnp.max(logits, axis=-1, keepdims=True))
        aff = (e / jnp.sum(e, axis=-1, keepdims=True)).T[0:ne, :]
        aff_ref[:, g * rg:(g + 1) * rg] = aff

    x1s = [project(g) for g in range(OUT_GROUPS)]
    for g in range(OUT_GROUPS):
        route(g, x1s[g])


def _outproj(y_conv, y_attn, x, wo_bf, mods4, g_ffn, w_router, tm):
    bsz, t, d = x.shape
    cw = y_conv.shape[-1]
    ne = w_router.shape[1]
    assert d == SUBLANES * LANES and 2 * ne <= LANES
    nt = t // tm
    w_hi = w_router.astype(BF16)
    w_lo = (w_router - w_hi.astype(F32)).astype(BF16)
    wr_t = jnp.zeros((d, LANES), BF16).at[:, 0:ne].set(w_hi).at[:, ne:2 * ne].set(w_lo)
    return pl.pallas_call(
        functools.partial(_outproj_body, cw=cw, tm=tm, ne=ne),
        grid=(bsz, nt),
        in_specs=[pl.BlockSpec((None, tm, cw), lambda b, i: (b, i, 0)),
                  pl.BlockSpec((None, tm, y_attn.shape[-1]), lambda b, i: (b, i, 0)),
                  pl.BlockSpec((None, tm, d), lambda b, i: (b, i, 0)),
                  pl.BlockSpec(wo_bf.shape, lambda b, i: (0, 0)),
                  _mod_spec(2, lambda b: b)(d),
                  pl.BlockSpec((1, d), lambda b, i: (0, 0)),
                  _mod_spec(3, lambda b: b)(d),
                  _mod_spec(4, lambda b: b)(d),
                  pl.BlockSpec(wr_t.shape, lambda b, i: (0, 0))],
        out_specs=[pl.BlockSpec((None, tm, d), lambda b, i: (b, i, 0)),
                   pl.BlockSpec((tm * SUBLANES, LANES), lambda b, i: (b * nt + i, 0)),
                   pl.BlockSpec((None, ne, tm), lambda b, i: (b, 0, i))],
        out_shape=[jax.ShapeDtypeStruct((bsz, t, d), F32),
                   jax.ShapeDtypeStruct((bsz * t * SUBLANES, LANES), F32),
                   jax.ShapeDtypeStruct((bsz, ne, t), F32)],
        compiler_params=_params(("arbitrary", "arbitrary"), VMEM_LIMIT),
        name="outproj",
    )(y_conv, y_attn, x, wo_bf, mods4, g_ffn, mods4, mods4, wr_t)


def _route_body(affe_ref, affr_ref, u_ref, ones_ref, l_ref, tok_ref, *, cap, ne, nblk):
    aff_e = affe_ref[...]

    def search(it, bits):
        cand = bits | jnp.left_shift(jnp.int32(1), 30 - it)
        cnt = jnp.sum(jnp.where(aff_e >= pltpu.bitcast(cand, F32), 1.0, 0.0), axis=1, keepdims=True)
        return jnp.where(cnt >= cap, cand, bits)

    tau = pltpu.bitcast(lax.fori_loop(0, 31, search, jnp.zeros((ne, 1), I32)), F32)
    n_gt = jnp.sum(jnp.where(aff_e > tau, 1.0, 0.0), axis=1, keepdims=True)
    need = cap - n_gt

    def per_row(col):
        return jnp.concatenate([jnp.broadcast_to(col[e:e + 1, :], (nblk, 1)) for e in range(ne)], axis=0)

    tau_r = per_row(tau)
    need_r = per_row(need)
    kr = affr_ref[...]

    def scan(mf):
        mb = mf.astype(BF16)
        incl = jnp.dot(mb, u_ref[...], preferred_element_type=F32)
        tot = jnp.dot(mb, ones_ref[...], preferred_element_type=F32)
        off = jnp.dot(l_ref[...], tot.astype(BF16), preferred_element_type=F32)
        return incl, tot, off

    eq = jnp.where(kr == tau_r, 1.0, 0.0)
    incl_e, _, off_e = scan(eq)
    rank_eq = off_e + incl_e - eq
    sel = jnp.where((kr > tau_r) | ((eq > 0.5) & (rank_eq < need_r)), 1.0, 0.0)
    incl, _, off = scan(sel)

    nbits = LANES.bit_length() - 1
    lane = lax.broadcasted_iota(I32, kr.shape, 1)
    row = lax.broadcasted_iota(I32, kr.shape, 0)
    dist = lane - (incl.astype(I32) - 1)
    word = jnp.where(sel > 0.5, (1 << (2 * nbits)) | (dist << nbits) | lane, 0)

    def bit(x, pos):
        return (x >> pos) & 1

    for k in range(nbits):
        arriving = pltpu.roll(word, LANES - (1 << k), axis=1)
        take = bit(arriving, 2 * nbits) & bit(arriving, nbits + k)
        stay = bit(word, 2 * nbits) & (1 - bit(word, nbits + k))
        word = jnp.where(take == 1, arriving, jnp.where(stay == 1, word, 0))

    off_i = off.astype(I32)
    shift = off_i & (LANES - 1)
    for k in range(nbits):
        word = jnp.where(bit(shift, k) == 1, pltpu.roll(word, 1 << k, axis=1), word)
    valid = bit(word, 2 * nbits) == 1
    wrapped = lane < shift
    blk_bits = nblk.bit_length() - 1
    q0 = (row >> blk_bits) * (cap // LANES) + (off_i >> nbits)
    digits = (jnp.where(valid, row & (nblk - 1), 0), jnp.where(valid, word & (LANES - 1), 0))

    def place(sel_rows, q):
        onehot_t = jnp.where(q == lane, 1.0, 0.0).astype(BF16)
        outs = []
        for dgt in digits:
            x = jnp.where(sel_rows, dgt, 0).astype(F32).astype(BF16)
            outs.append(lax.dot_general(onehot_t, x, (((0,), (0,)), ((), ())), preferred_element_type=F32))
        return outs[0] * LANES + outs[1]

    tok_ref[...] = (place(jnp.logical_not(wrapped), q0) + place(wrapped, q0 + 1)).astype(I32)


def _route(aff, cap):
    bsz, ne, t = aff.shape
    nblk = t // LANES
    nr = ne * nblk
    nq = ne * cap // LANES
    assert nq == LANES and cap % LANES == 0 and nblk & (nblk - 1) == 0 and nblk <= 256
    aff_r = aff.reshape(bsz, nr, LANES)
    ar = np.arange(LANES)
    upper = jnp.asarray(ar[:, None] <= ar[None, :], BF16)
    ones = jnp.ones((LANES, LANES), BF16)
    rr = np.arange(nr)
    lower = jnp.asarray((rr[:, None] // nblk == rr[None, :] // nblk) & (rr[None, :] < rr[:, None]), BF16)
    const = lambda a: pl.BlockSpec(a.shape, lambda b: (0,) * a.ndim)
    return pl.pallas_call(
        functools.partial(_route_body, cap=cap, ne=ne, nblk=nblk),
        grid=(bsz,),
        in_specs=[pl.BlockSpec((None, ne, t), lambda b: (b, 0, 0)),
                  pl.BlockSpec((None, nr, LANES), lambda b: (b, 0, 0)),
                  const(upper), const(ones), const(lower)],
        out_specs=pl.BlockSpec((None, nq, LANES), lambda b: (b, 0, 0)),
        out_shape=jax.ShapeDtypeStruct((bsz, nq, LANES), I32),
        compiler_params=_params(("arbitrary",), VMEM_LIMIT),
        name="route",
    )(aff, aff_r, upper, ones, lower)


MOE_HC = 512
MOE_TM = 512
ROW_UNROLL = 8


def _gather_copy(h2_hbm, xs3, sem, token, slot):
    src = h2_hbm.at[pl.ds(pl.multiple_of(token * SUBLANES, SUBLANES), SUBLANES), :]
    dst = xs3.at[pl.ds(pl.multiple_of(slot * SUBLANES, SUBLANES), SUBLANES), :]
    return pltpu.make_async_copy(src, dst, sem)


def _moe_body(tok_ref, nxt_ref, h2_hbm, wg_ref, wu_ref, wd_ref, y_ref, xs3, xsb, yacc, wgb, wub, wdb, sem,
              *, bsz, t, cap, ne, nch):
    e = pl.program_id(0)
    c = pl.program_id(1)
    n_rows = bsz * cap
    n_tiles = n_rows // MOE_TM
    per_tile = n_rows // (nch * n_tiles)
    n_lane_groups = xsb.shape[-1] // LANES

    def fetch(ref, b, slot, priority=0):
        _gather_copy(h2_hbm, xs3, sem, b * t + ref[b, 0, slot], b * cap + slot).start(priority=priority)

    def wait_staging():
        pltpu.make_async_copy(h2_hbm.at[pl.ds(0, n_rows * SUBLANES), :], xs3, sem).wait()

    @pl.when((e == 0) & (c == 0))
    def _first_expert():
        for b in range(bsz):
            def rows(g, carry):
                for k in range(ROW_UNROLL):
                    fetch(tok_ref, b, g * ROW_UNROLL + k)
                return carry

            lax.fori_loop(0, cap // ROW_UNROLL, rows, 0)

    def tiles(first_chunk, last_chunk=False):
        wgb[...] = wg_ref[...].astype(BF16)
        wub[...] = wu_ref[...].astype(BF16)
        wdb[...] = wd_ref[...].astype(BF16)
        for m in range(n_tiles):
            r0 = m * MOE_TM
            if first_chunk:
                xm = jnp.concatenate([_tile_rows(xs3, r0, MOE_TM, s)[...].astype(BF16)
                                      for s in range(n_lane_groups)], axis=-1)
                xsb[r0:r0 + MOE_TM, :] = xm
            else:
                xm = xsb[r0:r0 + MOE_TM, :]
            gate = jnp.dot(xm, wgb[...], preferred_element_type=F32)
            first = (c * n_tiles + m) * per_tile
            b = first // cap
            for k in range(per_tile):
                fetch(nxt_ref, b, first - b * cap + k, priority=1)
            up = jnp.dot(xm, wub[...], preferred_element_type=F32)
            hid = (gate * jax.nn.sigmoid(gate) * up).astype(BF16)
            contrib = jnp.dot(hid, wdb[...], preferred_element_type=F32)
            if first_chunk:
                yacc[r0:r0 + MOE_TM, :] = contrib
            elif last_chunk:
                y_ref[r0:r0 + MOE_TM, :] = (yacc[r0:r0 + MOE_TM, :] + contrib).astype(BF16)
            else:
                yacc[r0:r0 + MOE_TM, :] += contrib

    @pl.when(c == 0)
    def _():
        wait_staging()
        tiles(True)

    @pl.when((c > 0) & (c < nch - 1))
    def _():
        tiles(False)

    @pl.when(c == nch - 1)
    def _():
        tiles(False, last_chunk=True)

    @pl.when((e == ne - 1) & (c == nch - 1))
    def _drain():
        wait_staging()


def _moe(tok, h2_tiles, w_gate, w_up, w_down, bsz, t, cap):
    ne, d, dh = w_gate.shape
    nch = dh // MOE_HC
    n_rows = bsz * cap
    per_tile = n_rows // (nch * (n_rows // MOE_TM))
    assert cap % ROW_UNROLL == 0 and n_rows % MOE_TM == 0 and cap % per_tile == 0 and n_rows // nch <= MOE_TM
    assert d == SUBLANES * LANES and nch >= 2
    tok_spec = lambda ahead: pl.BlockSpec((bsz, None, 1, cap),
                                          lambda e, c: (0, jnp.minimum(e + ahead, ne - 1), 0, 0),
                                          memory_space=pltpu.SMEM)
    return pl.pallas_call(
        functools.partial(_moe_body, bsz=bsz, t=t, cap=cap, ne=ne, nch=nch),
        grid=(ne, nch),
        in_specs=[tok_spec(0), tok_spec(1),
                  pl.BlockSpec(memory_space=pl.ANY),
                  pl.BlockSpec((None, d, MOE_HC), lambda e, c: (e, 0, c)),
                  pl.BlockSpec((None, d, MOE_HC), lambda e, c: (e, 0, c)),
                  pl.BlockSpec((None, MOE_HC, d), lambda e, c: (e, c, 0))],
        out_specs=pl.BlockSpec((None, n_rows, d), lambda e, c: (e, 0, 0)),
        out_shape=jax.ShapeDtypeStruct((ne, n_rows, d), BF16),
        scratch_shapes=[pltpu.VMEM((n_rows * SUBLANES, LANES), F32),
                        pltpu.VMEM((n_rows, d), BF16),
                        pltpu.VMEM((n_rows, d), F32),
                        pltpu.VMEM((d, MOE_HC), BF16),
                        pltpu.VMEM((d, MOE_HC), BF16),
                        pltpu.VMEM((MOE_HC, d), BF16),
                        pltpu.SemaphoreType.DMA],
        compiler_params=_params(("arbitrary", "arbitrary"), VMEM_LIMIT),
        name="moe",
    )(tok, tok, h2_tiles, w_gate, w_up, w_down)


COMBINE_CHUNK = 128


def _combine_body(tok_ref, aff_ref, y_ref, x1_ref, g2_ref, gf_ref, o_ref, acc, ysc, *, cap, ne, tm):
    s = pl.program_id(1)

    @pl.when(s == 0)
    def _():
        acc[...] = jnp.zeros_like(acc)

    @pl.when(s < ne)
    def _scatter():
        def tile_of(row):
            start = row * SUBLANES
            return pl.ds(start if isinstance(start, int) else pl.multiple_of(start, SUBLANES), SUBLANES)

        for r0 in range(0, cap, COMBINE_CHUNK):
            for g in range(SUBLANES):
                _tile_rows(ysc, r0, COMBINE_CHUNK, g)[...] = (
                    y_ref[r0:r0 + COMBINE_CHUNK, g * LANES:(g + 1) * LANES].astype(F32))
            for base in range(r0, r0 + COMBINE_CHUNK, ROW_UNROLL):
                tokens = [tok_ref[0, base + k] for k in range(ROW_UNROLL)]
                new = [acc[tile_of(tk), :] + aff_ref[0, tk] * ysc[tile_of(base + k), :]
                       for k, tk in enumerate(tokens)]
                for tk, val in zip(tokens, new):
                    acc[tile_of(tk), :] = val

    @pl.when(s >= ne)
    def _finish():
        first = pl.multiple_of((s - ne) * tm, tm)
        moe = jnp.concatenate([_tile_rows(acc, first, tm, g)[...] for g in range(SUBLANES)], axis=-1)
        x = x1_ref[...] + g2_ref[...] * moe
        ms = jnp.mean(x * x, axis=-1, keepdims=True)
        o_ref[...] = x * lax.rsqrt(ms + EPS) * gf_ref[...]


def _combine(tok, aff, y, x1, mods4, g_final, cap, tm):
    bsz, t, d = x1.shape
    ne = aff.shape[1]
    nt = t // tm
    assert d == SUBLANES * LANES
    expert = lambda s: jnp.minimum(s, ne - 1)
    tile = lambda s: jnp.maximum(s - ne, 0)
    smem = lambda n: pl.BlockSpec((None, None, 1, n), lambda b, s: (b, expert(s), 0, 0), memory_space=pltpu.SMEM)
    token_tile = pl.BlockSpec((None, tm, d), lambda b, s: (b, tile(s), 0))
    return pl.pallas_call(
        functools.partial(_combine_body, cap=cap, ne=ne, tm=tm),
        grid=(bsz, ne + nt),
        in_specs=[smem(cap), smem(t),
                  pl.BlockSpec((None, cap, d), lambda b, s: (expert(s), b, 0)),
                  token_tile,
                  _mod_spec(5, lambda b: b)(d),
                  pl.BlockSpec((1, d), lambda b, s: (0, 0))],
        out_specs=token_tile,
        out_shape=jax.ShapeDtypeStruct((bsz, t, d), F32),
        scratch_shapes=[pltpu.VMEM((t * SUBLANES, LANES), F32),
                        pltpu.VMEM((cap * SUBLANES, LANES), F32)],
        compiler_params=_params(("arbitrary", "arbitrary"), VMEM_LIMIT),
        name="combine",
    )(tok, aff, y, x1, mods4, g_final)


def _layer(x, ctx, mods4, ctx_row, norm_mix_g, w_in, conv_w, conv_b, conv_ln_g, conv_ln_b, rpb, w_out,
           norm_ffn_g, w_router, w_gate, w_up, w_down, norm_final_g):
    bsz, t, d = x.shape
    cw = conv_w.shape[1]
    n_heads = rpb.shape[0]
    aw = (w_in.shape[1] - 2 * cw) // 3
    head_dim = aw // n_heads
    rows = t // GRID_W
    ne = w_router.shape[1]
    cap = EC_FACTOR * t // ne
    row = lambda a: a.reshape(1, -1)

    w_in_bf = w_in.astype(BF16)
    u, q, k, v = _inproj(x, row(norm_mix_g), mods4, w_in_bf, cw, aw, head_dim ** -0.5, tm=1024)
    kx, vx = _inproj_ctx(ctx, row(norm_mix_g), mods4, w_in_bf, 2 * cw + aw, aw, ctx_row)
    y_conv = _conv(u, conv_w, conv_b, conv_ln_g, conv_ln_b, tt=1024)
    assert rows >= (rpb.shape[1] + 1) // 2
    y_attn = _attn(q, k, v, kx, vx, _bias_table(rpb))
    x1, h2_tiles, aff = _outproj(y_conv, y_attn, x, w_out.astype(BF16), mods4, row(norm_ffn_g),
                                 w_router, tm=1024)
    tok = _route(aff, cap).reshape(bsz, ne, 1, cap)
    y = _moe(tok, h2_tiles, w_gate, w_up, w_down, bsz, t, cap)
    return _combine(tok, aff.reshape(bsz, ne, 1, t), y, x1, mods4, row(norm_final_g), cap, tm=512)


def kernel(x, c, ctx, c_ctx, w_mod, b_mod, norm_mix_g, w_in, conv_w, conv_b, conv_ln_g, conv_ln_b, rpb,
           w_out, norm_ffn_g, w_router, w_gate, w_up, w_down, norm_final_g):
    bsz, t, d = x.shape
    depth = w_mod.shape[0]
    assert depth == 1
    assert bsz + 1 <= SUBLANES
    c8 = jnp.zeros((SUBLANES, d), F32).at[:bsz].set(c).at[bsz].set(c_ctx)
    mods4 = _adaln(c8, w_mod[0], b_mod[0].reshape(1, -1)).reshape(SUBLANES, N_MOD, 1, d)
    return _layer(x, ctx, mods4, bsz, norm_mix_g[0], w_in[0], conv_w[0], conv_b[0], conv_ln_g[0],
                  conv_ln_b[0], rpb[0], w_out[0], norm_ffn_g[0], w_router[0], w_gate[0], w_up[0],
                  w_down[0], norm_final_g)
```
